```python
import math
import jax
import jax.numpy as jnp
from jax import lax
import numpy as np

D_MODEL = 2048
BATCH = 16
SEQ = 2048
DEPTH = 4

CHUNK = 64
N_MIXERS = 2
EPS = 1e-6

DN_QK_HEADS = 16
DN_V_HEADS = 32
DN_HEAD_DIM = 128
DN_CONV = 4
DN_KEY_DIM = DN_QK_HEADS * DN_HEAD_DIM
DN_VAL_DIM = DN_V_HEADS * DN_HEAD_DIM
DN_CONV_DIM = 2 * DN_KEY_DIM + DN_VAL_DIM
DN_IN_DIM = DN_CONV_DIM + DN_VAL_DIM + 2 * DN_V_HEADS

DSA_HEADS = 16
DSA_HEAD_DIM = 128
Q_LORA = 512
KV_LORA = 256
IDX_HEADS = 16
IDX_DIM = 128
IDX_TOPK = 256
QBLOCK = 128
DSA_IN_DIM = Q_LORA + KV_LORA + IDX_DIM + IDX_HEADS
DSA_UQ_DIM = DSA_HEADS * DSA_HEAD_DIM + IDX_HEADS * IDX_DIM

D_FF = 4 * D_MODEL
N_DN = (DEPTH + 1) // 2
N_DSA = DEPTH // 2

kernel_name = "hybrid_deltanet_dsa_streaming_trunk"


def rms_norm(x, g):
    xf = x.astype(jnp.float32)
    y = xf * lax.rsqrt(jnp.mean(xf * xf, axis=-1, keepdims=True) + EPS)
    return (y * g.astype(jnp.float32)).astype(x.dtype)


def _l2norm(t):
    return t * lax.rsqrt(jnp.sum(t * t, axis=-1, keepdims=True) + EPS)


def _causal_conv(u, w):
    k_taps = w.shape[0]
    t_len = u.shape[1]
    up = jnp.pad(u, ((0, 0), (k_taps - 1, 0), (0, 0)))
    y = up[:, 0:t_len] * w[0]
    for j in range(1, k_taps):
        y = y + up[:, j:j + t_len] * w[j]
    return y


def _chunk_gated_delta_rule(q, k, v, g, beta):
    b_sz, t_len, n_h, dk = q.shape
    dv = v.shape[-1]
    n_c = t_len // CHUNK

    def chunks(t):
        t = t.reshape((b_sz, n_c, CHUNK, n_h) + t.shape[3:])
        return jnp.moveaxis(t, (1, 3), (0, 2))

    q, k, v, g, beta = map(chunks, (q, k, v, g, beta))
    gc = jnp.cumsum(g, axis=-1)
    pos = jnp.arange(CHUNK)
    lower = pos[:, None] >= pos[None, :]
    strict = pos[:, None] > pos[None, :]
    diff = gc[..., :, None] - gc[..., None, :]
    decay = jnp.where(lower, jnp.exp(jnp.where(lower, diff, 0.0)), 0.0)
    kb = k * beta[..., None]
    vb = v * beta[..., None]
    l_mat = jnp.where(strict, jnp.einsum("nbhid,nbhjd->nbhij", kb, k) * decay, 0.0)
    eye = jnp.eye(CHUNK, dtype=jnp.float32)
    t_mat = lax.linalg.triangular_solve(eye + l_mat, jnp.broadcast_to(eye, l_mat.shape),
                                        left_side=True, lower=True, unit_diagonal=True)
    u = t_mat @ vb
    w = t_mat @ (kb * jnp.exp(gc)[..., None])
    intra = jnp.where(lower, jnp.einsum("nbhid,nbhjd->nbhij", q, k) * decay, 0.0)
    q_dec = q * jnp.exp(gc)[..., None]
    g_last = gc[..., -1]
    k_dec = k * jnp.exp(g_last[..., None] - gc)[..., None]

    def step(state, inp):
        q_i, k_i, u_i, w_i, a_i, gl = inp
        v_new = u_i - w_i @ state
        o_i = q_i @ state + a_i @ v_new
        state = state * jnp.exp(gl)[..., None, None] + jnp.einsum("bhcd,bhce->bhde", k_i, v_new)
        return state, o_i

    s0 = jnp.zeros((b_sz, n_h, dk, dv), jnp.float32)
    _, o = lax.scan(step, s0, (q_dec, k_dec, u, w, intra, g_last))
    return jnp.moveaxis(o, (0, 2), (1, 3)).reshape(b_sz, t_len, n_h, dv)


def gated_deltanet(h, w_in, conv_w, a_log, dt_bias, out_norm, w_out):
    b_sz, t_len, _ = h.shape
    f32 = jnp.float32
    proj = h @ w_in
    s1 = DN_CONV_DIM
    s2 = s1 + DN_VAL_DIM
    s3 = s2 + DN_V_HEADS
    qkv, z, b, a = jnp.split(proj, [s1, s2, s3], axis=-1)
    qkv = jax.nn.silu(_causal_conv(qkv, conv_w)).astype(f32)
    q, k, v = jnp.split(qkv, [DN_KEY_DIM, 2 * DN_KEY_DIM], axis=-1)
    rep = DN_V_HEADS // DN_QK_HEADS
    q = jnp.repeat(_l2norm(q.reshape(b_sz, t_len, DN_QK_HEADS, DN_HEAD_DIM)), rep, axis=2) * (DN_HEAD_DIM ** -0.5)
    k = jnp.repeat(_l2norm(k.reshape(b_sz, t_len, DN_QK_HEADS, DN_HEAD_DIM)), rep, axis=2)
    v = v.reshape(b_sz, t_len, DN_V_HEADS, DN_HEAD_DIM)
    beta = jax.nn.sigmoid(b.astype(f32))
    g = -jnp.exp(a_log.astype(f32)) * jax.nn.softplus(a.astype(f32) + dt_bias.astype(f32))
    o = _chunk_gated_delta_rule(q, k, v, g, beta)
    o = rms_norm(o, out_norm) * jax.nn.silu(z.reshape(b_sz, t_len, DN_V_HEADS, DN_HEAD_DIM).astype(f32))
    return o.reshape(b_sz, t_len, DN_VAL_DIM).astype(h.dtype) @ w_out


def dsa_attention(h, w_in, q_norm, kv_norm, kidx_norm, w_uq, w_uk, w_uv, w_out):
    b_sz, t_len, _ = h.shape
    f32 = jnp.float32
    proj = h @ w_in
    q_lat, c_kv, k_idx, w_idx = jnp.split(
        proj, [Q_LORA, Q_LORA + KV_LORA, Q_LORA + KV_LORA + IDX_DIM], axis=-1)
    q_all = rms_norm(q_lat, q_norm) @ w_uq
    q, q_idx = jnp.split(q_all, [DSA_HEADS * DSA_HEAD_DIM], axis=-1)
    q = q.reshape(b_sz, t_len, DSA_HEADS, DSA_HEAD_DIM)
    q_idx = q_idx.reshape(b_sz, t_len, IDX_HEADS, IDX_DIM).astype(f32)
    c_kv = rms_norm(c_kv, kv_norm).astype(f32)
    k_idx = rms_norm(k_idx, kidx_norm).astype(f32)
    w_idx = w_idx.astype(f32) * (IDX_HEADS ** -0.5 * IDX_DIM ** -0.5)
    q_abs = jnp.einsum("bthd,hdr->bthr", q, w_uk).astype(f32) * (DSA_HEAD_DIM ** -0.5)
    top_k = min(IDX_TOPK, t_len // 4)
    key_chunk = jnp.arange(t_len) // CHUNK
    n_blk = t_len // QBLOCK

    def blocks(t):
        return jnp.moveaxis(t.reshape((b_sz, n_blk, QBLOCK) + t.shape[2:]), 1, 0)

    def attend_block(args):
        qa_b, qi_b, wi_b, blk = args
        q_chunk = (blk * QBLOCK + jnp.arange(QBLOCK)) // CHUNK
        admissible = key_chunk[None, :] <= q_chunk[:, None]
        rel = jax.nn.relu(jnp.einsum("bthd,bsd->bths", qi_b, k_idx))
        score = jnp.einsum("bths,bth->bts", rel, wi_b)
        score = jnp.where(admissible[None], score, -jnp.inf)
        _, sel = lax.top_k(score, top_k)
        valid = key_chunk[sel] <= q_chunk[None, :, None]
        kv_sel = jax.vmap(lambda c, i: c[i])(c_kv, sel)
        logits = jnp.einsum("bthr,btkr->bthk", qa_b, kv_sel)
        logits = jnp.where(valid[:, :, None, :], logits, -jnp.inf)
        p = jax.nn.softmax(logits, axis=-1)
        return jnp.einsum("bthk,btkr->bthr", p, kv_sel)

    o_lat = lax.map(attend_block, (blocks(q_abs), blocks(q_idx), blocks(w_idx), jnp.arange(n_blk)))
    o_lat = jnp.moveaxis(o_lat, 0, 1).reshape(b_sz, t_len, DSA_HEADS, KV_LORA)
    o = jnp.einsum("bthr,hrd->bthd", o_lat.astype(h.dtype), w_uv)
    return o.reshape(b_sz, t_len, DSA_HEADS * DSA_HEAD_DIM) @ w_out


def squared_relu_mlp(h, w_up, w_down):
    return jnp.square(jax.nn.relu(h @ w_up)) @ w_down


def _dense(key, shape, fan_in):
    return jax.random.normal(key, shape, jnp.float32) * (fan_in ** -0.5)


def _gain(key, shape):
    return 1.0 + 0.02 * jax.random.normal(key, shape, jnp.float32)


def setup_inputs(seed: int = 0) -> dict:
    key = jax.random.key(seed)
    ks = jax.random.split(key, 20)
    x = jax.random.normal(ks[0], (BATCH, SEQ, D_MODEL), jnp.float32)
    dt = jnp.exp(jax.random.uniform(ks[7], (N_DN, DN_V_HEADS), dtype=jnp.float32,
                                    minval=math.log(1e-3), maxval=math.log(1e-1)))
    return {
        "x": x,
        "norm_mix": _gain(ks[1], (DEPTH, D_MODEL)),
        "norm_mlp": _gain(ks[2], (DEPTH, D_MODEL)),
        "norm_final": _gain(ks[3], (D_MODEL,)),
        "dn_w_in": _dense(ks[4], (N_DN, D_MODEL, DN_IN_DIM), D_MODEL),
        "dn_conv_w": _dense(ks[5], (N_DN, DN_CONV, DN_CONV_DIM), DN_CONV),
        "dn_a_log": jnp.log(jax.random.uniform(ks[6], (N_DN, DN_V_HEADS), dtype=jnp.float32,
                                               minval=1.0, maxval=16.0)),
        "dn_dt_bias": dt + jnp.log(-jnp.expm1(-dt)),
        "dn_out_norm": _gain(ks[8], (N_DN, DN_HEAD_DIM)),
        "dn_w_out": _dense(ks[9], (N_DN, DN_VAL_DIM, D_MODEL), DN_VAL_DIM),
        "dsa_w_in": _dense(ks[10], (N_DSA, D_MODEL, DSA_IN_DIM), D_MODEL),
        "dsa_q_norm": _gain(ks[11], (N_DSA, Q_LORA)),
        "dsa_kv_norm": _gain(ks[12], (N_DSA, KV_LORA)),
        "dsa_kidx_norm": _gain(ks[13], (N_DSA, IDX_DIM)),
        "dsa_w_uq": _dense(ks[14], (N_DSA, Q_LORA, DSA_UQ_DIM), Q_LORA),
        "dsa_w_uk": _dense(ks[15], (N_DSA, DSA_HEADS, DSA_HEAD_DIM, KV_LORA), KV_LORA),
        "dsa_w_uv": _dense(ks[16], (N_DSA, DSA_HEADS, KV_LORA, DSA_HEAD_DIM), KV_LORA),
        "dsa_w_out": _dense(ks[17], (N_DSA, DSA_HEADS * DSA_HEAD_DIM, D_MODEL), DSA_HEADS * DSA_HEAD_DIM),
        "mlp_w_up": _dense(ks[18], (DEPTH, D_MODEL, D_FF), D_MODEL),
        "mlp_w_down": _dense(ks[19], (DEPTH, D_FF, D_MODEL), D_FF),
    }


def reference(x, norm_mix, norm_mlp, norm_final, dn_w_in, dn_conv_w, dn_a_log, dn_dt_bias,
              dn_out_norm, dn_w_out, dsa_w_in, dsa_q_norm, dsa_kv_norm, dsa_kidx_norm,
              dsa_w_uq, dsa_w_uk, dsa_w_uv, dsa_w_out, mlp_w_up, mlp_w_down):
    h = x
    for i in range(DEPTH):
        j = i // N_MIXERS
        u = rms_norm(h, norm_mix[i])
        if i % N_MIXERS == 0:
            mix = gated_deltanet(u, dn_w_in[j], dn_conv_w[j], dn_a_log[j], dn_dt_bias[j],
                                 dn_out_norm[j], dn_w_out[j])
        else:
            mix = dsa_attention(u, dsa_w_in[j], dsa_q_norm[j], dsa_kv_norm[j], dsa_kidx_norm[j],
                                dsa_w_uq[j], dsa_w_uk[j], dsa_w_uv[j], dsa_w_out[j])
        h = h + mix
        h = h + squared_relu_mlp(rms_norm(h, norm_mlp[i]), mlp_w_up[i], mlp_w_down[i])
    return rms_norm(h, norm_final)
```

```python
import functools

import jax
import jax.numpy as jnp
from jax import lax
from jax.experimental import pallas as pl
from jax.experimental.pallas import tpu as pltpu

F32 = jnp.float32
BF16 = jnp.bfloat16
EPS = 1e-6
CHUNK = 64
QBLOCK = 128
DN_QK_HEADS = 16
DN_V_HEADS = 32
DN_HEAD_DIM = 128
DN_KEY_DIM = DN_QK_HEADS * DN_HEAD_DIM
DN_VAL_DIM = DN_V_HEADS * DN_HEAD_DIM
DN_CONV_DIM = 2 * DN_KEY_DIM + DN_VAL_DIM
DSA_HEADS = 16
DSA_HEAD_DIM = 128
Q_LORA = 512
KV_LORA = 256
IDX_HEADS = 16
IDX_DIM = 128
IDX_TOPK = 256
LANES = 128
VMEM_LIMIT_BYTES = 48 * 1024 * 1024
INT_MIN = -(2 ** 31)
MASKED = -1e30
M_INIT = -1e20


def _params(*sem):
    return pltpu.CompilerParams(dimension_semantics=sem, vmem_limit_bytes=VMEM_LIMIT_BYTES)


def _sigmoid(x):
    return 1.0 / (1.0 + jnp.exp(-x))


def _norm_body(x_ref, g_ref, o_ref):
    x = x_ref[...]
    y = x * lax.rsqrt(jnp.mean(x * x, axis=-1, keepdims=True) + EPS) * g_ref[...]
    o_ref[...] = y.astype(o_ref.dtype)


def rms_norm(x, g, out_dtype, tm=512):
    n, d = x.shape
    return pl.pallas_call(
        _norm_body,
        grid=(n // tm,),
        in_specs=[pl.BlockSpec((tm, d), lambda i: (i, 0)), pl.BlockSpec((1, d), lambda i: (0, 0))],
        out_specs=pl.BlockSpec((tm, d), lambda i: (i, 0)),
        out_shape=jax.ShapeDtypeStruct((n, d), out_dtype),
        compiler_params=_params("parallel"),
        name="rms_norm",
    )(x, g.reshape(1, d))


def _mm_body(a_ref, w_ref, *rest, nk, relu2, residual):
    if residual:
        r_ref, o_ref = rest[0], rest[1]
    else:
        r_ref, o_ref = None, rest[0]
    acc_ref = rest[-1] if nk > 1 else None

    def finish(acc):
        if relu2:
            acc = jnp.square(jnp.maximum(acc, 0.0))
        if residual:
            acc = acc + r_ref[...]
        o_ref[...] = acc.astype(o_ref.dtype)

    part = jnp.dot(a_ref[...], w_ref[...], preferred_element_type=F32)
    if nk == 1:
        finish(part)
        return
    k = pl.program_id(2)

    @pl.when(k == 0)
    def _():
        acc_ref[...] = part

    @pl.when(k > 0)
    def _():
        acc_ref[...] += part

    @pl.when(k == nk - 1)
    def _():
        finish(acc_ref[...])


def matmul(a, w, out_dtype, *, relu2=False, residual=None, tm=1024, tn=1024, tk=2048):
    m, kdim = a.shape
    n = w.shape[1]
    tm, tn, tk = min(tm, m), min(tn, n), min(tk, kdim)
    assert m % tm == 0 and n % tn == 0 and kdim % tk == 0
    nk = kdim // tk
    in_specs = [pl.BlockSpec((tm, tk), lambda i, j, k: (i, k)), pl.BlockSpec((tk, tn), lambda i, j, k: (k, j))]
    args = [a, w]
    if residual is not None:
        in_specs.append(pl.BlockSpec((tm, tn), lambda i, j, k: (i, j)))
        args.append(residual)
    return pl.pallas_call(
        functools.partial(_mm_body, nk=nk, relu2=relu2, residual=residual is not None),
        grid=(m // tm, n // tn, nk),
        in_specs=in_specs,
        out_specs=pl.BlockSpec((tm, tn), lambda i, j, k: (i, j)),
        out_shape=jax.ShapeDtypeStruct((m, n), out_dtype),
        scratch_shapes=[pltpu.VMEM((tm, tn), F32)] if nk > 1 else [],
        compiler_params=_params("parallel", "parallel", "arbitrary"),
        name="matmul",
    )(*args)


def _dn_gates_body(x_ref, alog_ref, dt_ref, o_ref):
    x = x_ref[...]
    rows = x.shape[0]
    g = -jnp.exp(alog_ref[...]) * (jnp.maximum(x + dt_ref[...], 0.0)
                                   + jnp.log(1.0 + jnp.exp(-jnp.abs(x + dt_ref[...]))))
    pos = lax.broadcasted_iota(jnp.int32, x.shape, 0) % CHUNK
    s = 1
    while s < CHUNK:
        g = g + jnp.where(pos >= s, pltpu.roll(g, s, axis=0), 0.0)
        s *= 2
    lane = lax.broadcasted_iota(jnp.int32, x.shape, 1)
    o_ref[...] = jnp.where(lane < DN_V_HEADS, _sigmoid(x), g)


def dn_gates(pba, alog_pad, dt_pad, tm=512):
    n = pba.shape[0]
    return pl.pallas_call(
        _dn_gates_body,
        grid=(n // tm,),
        in_specs=[pl.BlockSpec((tm, LANES), lambda i: (i, 0)),
                  pl.BlockSpec((1, LANES), lambda i: (0, 0)),
                  pl.BlockSpec((1, LANES), lambda i: (0, 0))],
        out_specs=pl.BlockSpec((tm, LANES), lambda i: (i, 0)),
        out_shape=jax.ShapeDtypeStruct((n, LANES), F32),
        compiler_params=_params("parallel"),
        name="dn_gates",
    )(pba, alog_pad, dt_pad)


def _conv_silu(ref, cw_ref, t0, tt):
    cw = cw_ref[...]
    acc = ref[0, t0:t0 + tt, :] * cw[3:4, :]
    for s in (1, 2, 3):
        if t0 == 0:
            x = ref[0, 0:tt, :]
            rows = lax.broadcasted_iota(jnp.int32, x.shape, 0)
            xs = jnp.where(rows >= s, pltpu.roll(x, s, axis=0), 0.0)
        else:
            xs = ref[0, t0 - s:t0 - s + tt, :]
        acc = acc + xs * cw[3 - s:4 - s, :]
    return acc * _sigmoid(acc)


def _l2norm(t):
    return t * lax.rsqrt(jnp.sum(t * t, axis=-1, keepdims=True) + EPS)


def _bdot(a, b):
    return jnp.dot(a.astype(BF16), b.astype(BF16), preferred_element_type=F32)


def _dn_core_body(q_ref, k_ref, v_ref, z_ref, cwq_ref, cwk_ref, cwv_ref, gates_ref, gcr_ref, onorm_ref,
                  o_ref, qs_ref, ks_ref, vs_ref, bb_ref, gb_ref, s_ref, *, t_len):
    j = pl.program_id(1)
    tt = 256
    hd = DN_HEAD_DIM
    for ti in range(t_len // tt):
        t0 = ti * tt
        qs_ref[t0:t0 + tt, :] = _l2norm(_conv_silu(q_ref, cwq_ref, t0, tt)) * (hd ** -0.5)
        ks_ref[t0:t0 + tt, :] = _l2norm(_conv_silu(k_ref, cwk_ref, t0, tt))
        vs_ref[t0:t0 + tt, :] = _conv_silu(v_ref, cwv_ref, t0, tt)
        gt = gates_ref[0, t0:t0 + tt, :]
        lane = lax.broadcasted_iota(jnp.int32, gt.shape, 1)
        for hh in range(2):
            col = 2 * j + hh
            bsel = jnp.sum(jnp.where(lane == col, gt, 0.0), axis=1, keepdims=True)
            gsel = jnp.sum(jnp.where(lane == col + DN_V_HEADS, gt, 0.0), axis=1, keepdims=True)
            bb_ref[hh, t0:t0 + tt, :] = jnp.broadcast_to(bsel, (tt, LANES))
            gb_ref[hh, t0:t0 + tt, :] = jnp.broadcast_to(gsel, (tt, LANES))
    s_ref[...] = jnp.zeros_like(s_ref)

    c = CHUNK
    ri = lax.broadcasted_iota(jnp.int32, (c, c), 0)
    ci = lax.broadcasted_iota(jnp.int32, (c, c), 1)
    lower = ri >= ci
    strict = ri > ci
    eye = jnp.where(ri == ci, 1.0, 0.0)
    pair_mask = strict & ((ri // 2) == (ci // 2))
    merge_masks = [((ri // (2 * s)) == (ci // (2 * s))) & ((ri // s) != (ci // s)) & strict
                   for s in (2, 4, 8, 16, 32)]
    onorm = onorm_ref[...]

    def pair_step(c2, carry):
        base = pl.multiple_of(c2 * 2 * c, 2 * c)
        for sc in range(2):
            r0 = pl.multiple_of(base + sc * c, c)
            kc = ks_ref[pl.ds(r0, c), :]
            qc = qs_ref[pl.ds(r0, c), :]
            kq = lax.dot_general(jnp.concatenate([kc, qc], axis=0).astype(BF16), kc.astype(BF16),
                                 (((1,), (1,)), ((), ())), preferred_element_type=F32)
            kk, qk = kq[:c], kq[c:]
            for hh in range(2):
                beta_b = bb_ref[hh, pl.ds(r0, c), :]
                gcc_b = gb_ref[hh, pl.ds(r0, c), :]
                gcr = gcr_ref[0, 0, hh:hh + 1, pl.ds(base, 2 * c)][:, sc * c:(sc + 1) * c]
                gl_b = gcc_b[c - 1:c, :]
                dm = gcc_b[:, :c] - gcr
                decay = jnp.where(lower, jnp.exp(jnp.where(lower, dm, 0.0)), 0.0)
                lm = jnp.where(strict, kk * beta_b[:, :c] * decay, 0.0)
                am = qk * decay
                tm_ = eye - jnp.where(pair_mask, lm, 0.0)
                for mk in merge_masks:
                    tm_ = tm_ - _bdot(_bdot(tm_, jnp.where(mk, lm, 0.0)), tm_)
                eg = jnp.exp(gcc_b)
                vc = vs_ref[pl.ds(r0, c), hh * hd:(hh + 1) * hd]
                rhs = jnp.concatenate([vc * beta_b, kc * (beta_b * eg)], axis=1)
                uw = _bdot(tm_, rhs)
                st = s_ref[hh]
                stb = st.astype(BF16)
                v_new = uw[:, :hd] - _bdot(uw[:, hd:], stb)
                o = _bdot(qc * eg, stb) + _bdot(am, v_new)
                kdec = kc * jnp.exp(gl_b - gcc_b)
                s_ref[hh] = st * jnp.exp(gl_b) + lax.dot_general(
                    kdec.astype(BF16), v_new.astype(BF16), (((0,), (0,)), ((), ())),
                    preferred_element_type=F32)
                y = o * lax.rsqrt(jnp.mean(o * o, axis=-1, keepdims=True) + EPS) * onorm
                zz = z_ref[0, pl.ds(r0, c), hh * hd:(hh + 1) * hd]
                o_ref[0, pl.ds(r0, c), hh * hd:(hh + 1) * hd] = (y * (zz * _sigmoid(zz))).astype(o_ref.dtype)
        return carry

    lax.fori_loop(0, t_len // (2 * c), pair_step, 0)


def dn_core(proj, conv_w, gates, gcr, out_norm):
    b, t, _ = proj.shape
    hd = DN_HEAD_DIM
    nq = DN_QK_HEADS
    v_blk0 = 2 * DN_KEY_DIM // (2 * hd)
    z_blk0 = DN_CONV_DIM // (2 * hd)
    return pl.pallas_call(
        functools.partial(_dn_core_body, t_len=t),
        grid=(b, nq),
        in_specs=[
            pl.BlockSpec((1, t, hd), lambda bi, j: (bi, 0, j)),
            pl.BlockSpec((1, t, hd), lambda bi, j: (bi, 0, nq + j)),
            pl.BlockSpec((1, t, 2 * hd), lambda bi, j: (bi, 0, v_blk0 + j)),
            pl.BlockSpec((1, t, 2 * hd), lambda bi, j: (bi, 0, z_blk0 + j)),
            pl.BlockSpec((4, hd), lambda bi, j: (0, j)),
            pl.BlockSpec((4, hd), lambda bi, j: (0, nq + j)),
            pl.BlockSpec((4, 2 * hd), lambda bi, j: (0, v_blk0 + j)),
            pl.BlockSpec((1, t, LANES), lambda bi, j: (bi, 0, 0)),
            pl.BlockSpec((1, 1, 2, t), lambda bi, j: (bi, j, 0, 0)),
            pl.BlockSpec((1, hd), lambda bi, j: (0, 0)),
        ],
        out_specs=pl.BlockSpec((1, t, 2 * hd), lambda bi, j: (bi, 0, j)),
        out_shape=jax.ShapeDtypeStruct((b, t, DN_VAL_DIM), BF16),
        scratch_shapes=[
            pltpu.VMEM((t, hd), F32), pltpu.VMEM((t, hd), F32), pltpu.VMEM((t, 2 * hd), F32),
            pltpu.VMEM((2, t, LANES), F32), pltpu.VMEM((2, t, LANES), F32),
            pltpu.VMEM((2, hd, hd), F32),
        ],
        compiler_params=_params("parallel", "arbitrary"),
        name="dn_core",
    )(proj, proj, proj, proj, conv_w, conv_w, conv_w, gates, gcr, out_norm.reshape(1, hd))


def gated_deltanet_layer(h2, b, t, norm_g, w_in, conv_w, a_log, dt_bias, out_norm, w_out):
    n = b * t
    u = rms_norm(h2, norm_g, BF16)
    n_qkvz = DN_CONV_DIM + DN_VAL_DIM
    proj = matmul(u, w_in[:, :n_qkvz].astype(BF16), F32)
    w_ba = jnp.pad(w_in[:, n_qkvz:], ((0, 0), (0, LANES - 2 * DN_V_HEADS))).astype(BF16)
    pba = matmul(u, w_ba, F32)
    pad = lambda p: jnp.pad(p.astype(F32), (DN_V_HEADS, LANES - 2 * DN_V_HEADS)).reshape(1, LANES)
    gates = dn_gates(pba, pad(a_log), pad(dt_bias))
    gcr = gates[:, DN_V_HEADS:2 * DN_V_HEADS].reshape(b, t, DN_QK_HEADS, 2).transpose(0, 2, 3, 1)
    o = dn_core(proj.reshape(b, t, n_qkvz), conv_w, gates.reshape(b, t, LANES), gcr, out_norm)
    return matmul(o.reshape(n, DN_VAL_DIM), w_out.astype(BF16), F32, residual=h2)


def _dsa_prep_body(p_ref, qn_ref, kvn_ref, kin_ref, q_out, kv_out, ki_out, w_out):
    def nrm(x, g):
        return x * lax.rsqrt(jnp.mean(x * x, axis=-1, keepdims=True) + EPS) * g

    a, b2, c2 = Q_LORA, Q_LORA + KV_LORA, Q_LORA + KV_LORA + IDX_DIM
    q_out[...] = nrm(p_ref[:, 0:a], qn_ref[...]).astype(q_out.dtype)
    kv_out[...] = nrm(p_ref[:, a:b2], kvn_ref[...]).astype(kv_out.dtype)
    ki_out[...] = nrm(p_ref[:, b2:c2], kin_ref[...]).astype(ki_out.dtype)
    w_out[...] = p_ref[:, c2:c2 + LANES] * (IDX_HEADS ** -0.5 * IDX_DIM ** -0.5)


def dsa_prep(proj, q_norm, kv_norm, kidx_norm, tm=512):
    n, width = proj.shape
    row = lambda i: (i, 0)
    fix = lambda i: (0, 0)
    return pl.pallas_call(
        _dsa_prep_body,
        grid=(n // tm,),
        in_specs=[pl.BlockSpec((tm, width), row), pl.BlockSpec((1, Q_LORA), fix),
                  pl.BlockSpec((1, KV_LORA), fix), pl.BlockSpec((1, IDX_DIM), fix)],
        out_specs=[pl.BlockSpec((tm, Q_LORA), row), pl.BlockSpec((tm, KV_LORA), row),
                   pl.BlockSpec((tm, IDX_DIM), row), pl.BlockSpec((tm, LANES), row)],
        out_shape=[jax.ShapeDtypeStruct((n, Q_LORA), BF16), jax.ShapeDtypeStruct((n, KV_LORA), BF16),
                   jax.ShapeDtypeStruct((n, IDX_DIM), BF16), jax.ShapeDtypeStruct((n, LANES), F32)],
        compiler_params=_params("parallel"),
        name="dsa_prep",
    )(proj, q_norm.reshape(1, -1), kv_norm.reshape(1, -1), kidx_norm.reshape(1, -1))


def _qabs_body(q_ref, w_ref, o_ref, *, r):
    for h in range(DSA_HEADS):
        res = jnp.dot(q_ref[:, h * DSA_HEAD_DIM:(h + 1) * DSA_HEAD_DIM], w_ref[h],
                      preferred_element_type=F32) * (DSA_HEAD_DIM ** -0.5)
        for rr in range(r):
            o_ref[rr, h] = res[rr * QBLOCK:(rr + 1) * QBLOCK].astype(o_ref.dtype)


def q_absorb(q_all, w_uk, tm=256):
    n = q_all.shape[0]
    r = tm // QBLOCK
    hw = DSA_HEADS * DSA_HEAD_DIM
    return pl.pallas_call(
        functools.partial(_qabs_body, r=r),
        grid=(n // tm,),
        in_specs=[pl.BlockSpec((tm, hw), lambda i: (i, 0)),
                  pl.BlockSpec((DSA_HEADS, DSA_HEAD_DIM, KV_LORA), lambda i: (0, 0, 0))],
        out_specs=pl.BlockSpec((r, DSA_HEADS, QBLOCK, KV_LORA), lambda i: (i, 0, 0, 0)),
        out_shape=jax.ShapeDtypeStruct((n // QBLOCK, DSA_HEADS, QBLOCK, KV_LORA), BF16),
        compiler_params=_params("parallel"),
        name="q_absorb",
    )(q_all, w_uk)


def _vup_body(o_ref, w_ref, out_ref, *, r):
    for h in range(DSA_HEADS):
        for rr in range(r):
            out_ref[rr * QBLOCK:(rr + 1) * QBLOCK, h * DSA_HEAD_DIM:(h + 1) * DSA_HEAD_DIM] = jnp.dot(
                o_ref[rr, h], w_ref[h], preferred_element_type=F32).astype(out_ref.dtype)


def v_up(o_lat, w_uv, tm=256):
    nb = o_lat.shape[0]
    r = tm // QBLOCK
    hw = DSA_HEADS * DSA_HEAD_DIM
    return pl.pallas_call(
        functools.partial(_vup_body, r=r),
        grid=(nb // r,),
        in_specs=[pl.BlockSpec((r, DSA_HEADS, QBLOCK, KV_LORA), lambda i: (i, 0, 0, 0)),
                  pl.BlockSpec((DSA_HEADS, KV_LORA, DSA_HEAD_DIM), lambda i: (0, 0, 0))],
        out_specs=pl.BlockSpec((tm, hw), lambda i: (i, 0)),
        out_shape=jax.ShapeDtypeStruct((nb * QBLOCK, hw), BF16),
        compiler_params=_params("parallel"),
        name="v_up",
    )(o_lat, w_uv)


def _dsa_core_body(qidx_ref, widx_ref, kidx_ref, ckv_ref, qabs_ref, o_ref, keys_ref, bias_ref,
                   *, t_len, top_k, kt):
    i = pl.program_id(1)
    nkt = t_len // kt
    nq = QBLOCK
    w = widx_ref[0]
    qrow = lax.broadcasted_iota(jnp.int32, (nq, kt), 0)
    kcol = lax.broadcasted_iota(jnp.int32, (nq, kt), 1)
    q_chunk = (i * nq + qrow) // CHUNK
    tn = (((1,), (1,)), ((), ()))
    for jt in range(nkt):
        kblk = kidx_ref[0, jt * kt:(jt + 1) * kt, :]
        sc = jnp.zeros((nq, kt), F32)
        for h in range(IDX_HEADS):
            d = lax.dot_general(qidx_ref[0, :, h * IDX_DIM:(h + 1) * IDX_DIM], kblk, tn,
                                preferred_element_type=F32)
            sc = sc + jnp.maximum(d, 0.0) * w[:, h:h + 1]
        bits = pltpu.bitcast(sc, jnp.int32)
        key = bits ^ ((bits >> 31) & 0x7FFFFFFF)
        adm = ((jt * kt + kcol) // CHUNK) <= q_chunk
        keys_ref[:, jt * kt:(jt + 1) * kt] = jnp.where(adm, key, INT_MIN)

    def bisect(it, prefix):
        cand = prefix | lax.shift_left(jnp.int32(1), 31 - it)
        cnt = jnp.sum(jnp.where(keys_ref[...] >= (cand ^ INT_MIN), 1.0, 0.0), axis=1, keepdims=True)
        return jnp.where(cnt >= top_k, cand, prefix)

    prefix = lax.fori_loop(0, 32, bisect, jnp.zeros((nq, 1), jnp.int32))
    tau = jnp.maximum(prefix ^ INT_MIN, INT_MIN + 1)
    bias_ref[...] = jnp.where(keys_ref[...] >= tau, 0.0, MASKED)

    rows = DSA_HEADS * nq
    qa = qabs_ref[0].reshape(rows, KV_LORA)
    m = jnp.full((rows, 1), M_INIT, F32)
    l = jnp.zeros((rows, 1), F32)
    acc = jnp.zeros((rows, KV_LORA), F32)
    for jt in range(nkt):
        ck = ckv_ref[0, jt * kt:(jt + 1) * kt, :]
        s = lax.dot_general(qa, ck, tn, preferred_element_type=F32)
        s = (s.reshape(DSA_HEADS, nq, kt) + bias_ref[:, jt * kt:(jt + 1) * kt][None]).reshape(rows, kt)
        m_new = jnp.maximum(m, jnp.max(s, axis=1, keepdims=True))
        alpha = jnp.exp(m - m_new)
        p = jnp.exp(s - m_new)
        l = alpha * l + jnp.sum(p, axis=1, keepdims=True)
        acc = acc * alpha + jnp.dot(p.astype(BF16), ck, preferred_element_type=F32)
        m = m_new
    o_ref[0] = (acc / l).reshape(DSA_HEADS, nq, KV_LORA).astype(o_ref.dtype)


def dsa_core(q_all, widx, kidx, ckv, qabs, b, t, kt=256):
    nblk = t // QBLOCK
    top_k = min(IDX_TOPK, t // 4)
    hw = IDX_HEADS * IDX_DIM
    return pl.pallas_call(
        functools.partial(_dsa_core_body, t_len=t, top_k=top_k, kt=kt),
        grid=(b, nblk),
        in_specs=[
            pl.BlockSpec((1, QBLOCK, hw), lambda bi, i: (bi, i, 1)),
            pl.BlockSpec((1, QBLOCK, LANES), lambda bi, i: (bi, i, 0)),
            pl.BlockSpec((1, t, IDX_DIM), lambda bi, i: (bi, 0, 0)),
            pl.BlockSpec((1, t, KV_LORA), lambda bi, i: (bi, 0, 0)),
            pl.BlockSpec((1, DSA_HEADS, QBLOCK, KV_LORA), lambda bi, i: (bi * nblk + i, 0, 0, 0)),
        ],
        out_specs=pl.BlockSpec((1, DSA_HEADS, QBLOCK, KV_LORA), lambda bi, i: (bi * nblk + i, 0, 0, 0)),
        out_shape=jax.ShapeDtypeStruct((b * nblk, DSA_HEADS, QBLOCK, KV_LORA), BF16),
        scratch_shapes=[pltpu.VMEM((QBLOCK, t), jnp.int32), pltpu.VMEM((QBLOCK, t), F32)],
        compiler_params=_params("parallel", "arbitrary"),
        name="dsa_core",
    )(q_all.reshape(b, t, -1), widx.reshape(b, t, LANES), kidx.reshape(b, t, IDX_DIM),
      ckv.reshape(b, t, KV_LORA), qabs)


def dsa_layer(h2, b, t, norm_g, w_in, q_norm, kv_norm, kidx_norm, w_uq, w_uk, w_uv, w_out):
    u = rms_norm(h2, norm_g, BF16)
    width = Q_LORA + KV_LORA + IDX_DIM + LANES
    w_in_p = jnp.pad(w_in, ((0, 0), (0, width - w_in.shape[1]))).astype(BF16)
    proj = matmul(u, w_in_p, F32)
    qlat, ckv, kidx, widx = dsa_prep(proj, q_norm, kv_norm, kidx_norm)
    q_all = matmul(qlat, w_uq.astype(BF16), BF16)
    qabs = q_absorb(q_all, w_uk.astype(BF16))
    o_lat = dsa_core(q_all, widx, kidx, ckv, qabs, b, t)
    o = v_up(o_lat, w_uv.astype(BF16))
    return matmul(o, w_out.astype(BF16), F32, residual=h2)


def mlp(h2, norm_g, w_up, w_down):
    u = rms_norm(h2, norm_g, BF16)
    a = matmul(u, w_up.astype(BF16), BF16, relu2=True)
    return matmul(a, w_down.astype(BF16), F32, residual=h2)


def kernel(x, norm_mix, norm_mlp, norm_final, dn_w_in, dn_conv_w, dn_a_log, dn_dt_bias, dn_out_norm, dn_w_out, dsa_w_in, dsa_q_norm, dsa_kv_norm, dsa_kidx_norm, dsa_w_uq, dsa_w_uk, dsa_w_uv, dsa_w_out, mlp_w_up, mlp_w_down):
    b, t, d = x.shape
    h2 = x.reshape(b * t, d)
    for i in range(norm_mix.shape[0]):
        j = i // 2
        if i % 2 == 0:
            h2 = gated_deltanet_layer(h2, b, t, norm_mix[i], dn_w_in[j], dn_conv_w[j], dn_a_log[j],
                                      dn_dt_bias[j], dn_out_norm[j], dn_w_out[j])
        else:
            h2 = dsa_layer(h2, b, t, norm_mix[i], dsa_w_in[j], dsa_q_norm[j], dsa_kv_norm[j],
                           dsa_kidx_norm[j], dsa_w_uq[j], dsa_w_uk[j], dsa_w_uv[j], dsa_w_out[j])
        h2 = mlp(h2, norm_mlp[i], mlp_w_up[i], mlp_w_down[i])
    return rms_norm(h2, norm_final, x.dtype).reshape(b, t, d)
```

```python
import functools

import jax
import jax.numpy as jnp
from jax import lax
from jax.experimental import pallas as pl
from jax.experimental.pallas import tpu as pltpu

F32 = jnp.float32
BF16 = jnp.bfloat16
EPS = 1e-6
CHUNK = 64
QBLOCK = 128
DN_QK_HEADS = 16
DN_V_HEADS = 32
DN_HEAD_DIM = 128
DN_KEY_DIM = DN_QK_HEADS * DN_HEAD_DIM
DN_VAL_DIM = DN_V_HEADS * DN_HEAD_DIM
DN_CONV_DIM = 2 * DN_KEY_DIM + DN_VAL_DIM
DSA_HEADS = 16
DSA_HEAD_DIM = 128
Q_LORA = 512
KV_LORA = 256
IDX_HEADS = 16
IDX_DIM = 128
IDX_TOPK = 256
LANES = 128
VMEM_LIMIT_BYTES = 48 * 1024 * 1024
INT_MIN = -(2 ** 31)
MASKED = -1e30
M_INIT = -1e20


def _params(*sem):
    return pltpu.CompilerParams(dimension_semantics=sem, vmem_limit_bytes=VMEM_LIMIT_BYTES)


def _sigmoid(x):
    return 1.0 / (1.0 + jnp.exp(-x))


def _norm_body(x_ref, g_ref, o_ref):
    x = x_ref[...]
    y = x * lax.rsqrt(jnp.mean(x * x, axis=-1, keepdims=True) + EPS) * g_ref[...]
    o_ref[...] = y.astype(o_ref.dtype)


def rms_norm(x, g, out_dtype, tm=512):
    n, d = x.shape
    return pl.pallas_call(
        _norm_body,
        grid=(n // tm,),
        in_specs=[pl.BlockSpec((tm, d), lambda i: (i, 0)), pl.BlockSpec((1, d), lambda i: (0, 0))],
        out_specs=pl.BlockSpec((tm, d), lambda i: (i, 0)),
        out_shape=jax.ShapeDtypeStruct((n, d), out_dtype),
        compiler_params=_params("parallel"),
        name="rms_norm",
    )(x, g.reshape(1, d))


def _mm_body(a_ref, w_ref, *rest, nk, relu2, residual):
    if residual:
        r_ref, o_ref = rest[0], rest[1]
    else:
        r_ref, o_ref = None, rest[0]
    acc_ref = rest[-1] if nk > 1 else None

    def finish(acc):
        if relu2:
            acc = jnp.square(jnp.maximum(acc, 0.0))
        if residual:
            acc = acc + r_ref[...]
        o_ref[...] = acc.astype(o_ref.dtype)

    part = jnp.dot(a_ref[...], w_ref[...], preferred_element_type=F32)
    if nk == 1:
        finish(part)
        return
    k = pl.program_id(2)

    @pl.when(k == 0)
    def _():
        acc_ref[...] = part

    @pl.when(k > 0)
    def _():
        acc_ref[...] += part

    @pl.when(k == nk - 1)
    def _():
        finish(acc_ref[...])


def matmul(a, w, out_dtype, *, relu2=False, residual=None, tm=1024, tn=1024, tk=2048):
    m, kdim = a.shape
    n = w.shape[1]
    tm, tn, tk = min(tm, m), min(tn, n), min(tk, kdim)
    assert m % tm == 0 and n % tn == 0 and kdim % tk == 0
    nk = kdim // tk
    in_specs = [pl.BlockSpec((tm, tk), lambda i, j, k: (i, k)), pl.BlockSpec((tk, tn), lambda i, j, k: (k, j))]
    args = [a, w]
    if residual is not None:
        in_specs.append(pl.BlockSpec((tm, tn), lambda i, j, k: (i, j)))
        args.append(residual)
    return pl.pallas_call(
        functools.partial(_mm_body, nk=nk, relu2=relu2, residual=residual is not None),
        grid=(m // tm, n // tn, nk),
        in_specs=in_specs,
        out_specs=pl.BlockSpec((tm, tn), lambda i, j, k: (i, j)),
        out_shape=jax.ShapeDtypeStruct((m, n), out_dtype),
        scratch_shapes=[pltpu.VMEM((tm, tn), F32)] if nk > 1 else [],
        compiler_params=_params("parallel", "parallel", "arbitrary"),
        name="matmul",
    )(*args)


def _dn_gates_body(x_ref, alog_ref, dt_ref, o_ref):
    x = x_ref[...]
    g = -jnp.exp(alog_ref[...]) * (jnp.maximum(x + dt_ref[...], 0.0)
                                   + jnp.log(1.0 + jnp.exp(-jnp.abs(x + dt_ref[...]))))
    pos = lax.broadcasted_iota(jnp.int32, x.shape, 0) % CHUNK
    s = 1
    while s < CHUNK:
        g = g + jnp.where(pos >= s, pltpu.roll(g, s, axis=0), 0.0)
        s *= 2
    lane = lax.broadcasted_iota(jnp.int32, x.shape, 1)
    o_ref[...] = jnp.where(lane < DN_V_HEADS, _sigmoid(x), g)


def dn_gates(pba, alog_pad, dt_pad, tm=512):
    n = pba.shape[0]
    return pl.pallas_call(
        _dn_gates_body,
        grid=(n // tm,),
        in_specs=[pl.BlockSpec((tm, LANES), lambda i: (i, 0)),
                  pl.BlockSpec((1, LANES), lambda i: (0, 0)),
                  pl.BlockSpec((1, LANES), lambda i: (0, 0))],
        out_specs=pl.BlockSpec((tm, LANES), lambda i: (i, 0)),
        out_shape=jax.ShapeDtypeStruct((n, LANES), F32),
        compiler_params=_params("parallel"),
        name="dn_gates",
    )(pba, alog_pad, dt_pad)


def _conv_silu(ref, cw_ref, t0, tt):
    cw = cw_ref[...]
    acc = ref[0, t0:t0 + tt, :] * cw[3:4, :]
    for s in (1, 2, 3):
        if t0 == 0:
            x = ref[0, 0:tt, :]
            rows = lax.broadcasted_iota(jnp.int32, x.shape, 0)
            xs = jnp.where(rows >= s, pltpu.roll(x, s, axis=0), 0.0)
        else:
            xs = ref[0, t0 - s:t0 - s + tt, :]
        acc = acc + xs * cw[3 - s:4 - s, :]
    return acc * _sigmoid(acc)


def _l2norm(t):
    return t * lax.rsqrt(jnp.sum(t * t, axis=-1, keepdims=True) + EPS)


def _bmm(a, b):
    return jnp.einsum("nij,njk->nik", a.astype(BF16), b.astype(BF16), preferred_element_type=F32)


def _bmm_nt(a, b):
    return jnp.einsum("nid,njd->nij", a.astype(BF16), b.astype(BF16), preferred_element_type=F32)


DN_GROUP = 8


def _dn_prep_body(q_ref, k_ref, v_ref, cwq_ref, cwk_ref, cwv_ref, gates_ref, gcr_ref,
                  u_ref, w_ref, qd_ref, kd_ref, a_ref, qs_ref, ks_ref, vs_ref, bb_ref, gb_ref, *, t_len):
    j = pl.program_id(1)
    tt = 256
    hd = DN_HEAD_DIM
    for ti in range(t_len // tt):
        t0 = ti * tt
        qs_ref[t0:t0 + tt, :] = _l2norm(_conv_silu(q_ref, cwq_ref, t0, tt)) * (hd ** -0.5)
        ks_ref[t0:t0 + tt, :] = _l2norm(_conv_silu(k_ref, cwk_ref, t0, tt))
        vs_ref[t0:t0 + tt, :] = _conv_silu(v_ref, cwv_ref, t0, tt)
        gt = gates_ref[0, t0:t0 + tt, :]
        lane = lax.broadcasted_iota(jnp.int32, gt.shape, 1)
        for hh in range(2):
            col = 2 * j + hh
            bsel = jnp.sum(jnp.where(lane == col, gt, 0.0), axis=1, keepdims=True)
            gsel = jnp.sum(jnp.where(lane == col + DN_V_HEADS, gt, 0.0), axis=1, keepdims=True)
            bb_ref[hh, t0:t0 + tt, :] = jnp.broadcast_to(bsel, (tt, LANES))
            gb_ref[hh, t0:t0 + tt, :] = jnp.broadcast_to(gsel, (tt, LANES))

    c = CHUNK
    g = DN_GROUP
    rows = g * c
    ri = lax.broadcasted_iota(jnp.int32, (c, c), 0)
    ci = lax.broadcasted_iota(jnp.int32, (c, c), 1)
    lower = ri >= ci
    strict = ri > ci
    eye = jnp.where(ri == ci, 1.0, 0.0)
    pair_mask = strict & ((ri // 2) == (ci // 2))
    merge_masks = [((ri // (2 * s)) == (ci // (2 * s))) & ((ri // s) != (ci // s)) & strict
                   for s in (2, 4, 8, 16, 32)]

    def group_step(gi, carry):
        base = pl.multiple_of(gi * rows, rows)
        kc = ks_ref[pl.ds(base, rows), :].reshape(g, c, hd)
        qc = qs_ref[pl.ds(base, rows), :].reshape(g, c, hd)
        kk = _bmm_nt(kc, kc)
        qk = _bmm_nt(qc, kc)
        for hh in range(2):
            beta_b = bb_ref[hh, pl.ds(base, rows), :].reshape(g, c, LANES)
            gcc_b = gb_ref[hh, pl.ds(base, rows), :].reshape(g, c, LANES)
            grow = gcr_ref[0, 0, hh:hh + 1, pl.ds(base, rows)]
            gcr = jnp.stack([grow[:, n * c:(n + 1) * c] for n in range(g)], axis=0)
            gl_b = gcc_b[:, c - 1:c, :]
            dm = gcc_b[:, :, :c] - gcr
            decay = jnp.where(lower, jnp.exp(jnp.where(lower, dm, 0.0)), 0.0)
            lm = jnp.where(strict, kk * beta_b[:, :, :c] * decay, 0.0)
            tm_ = eye - jnp.where(pair_mask, lm, 0.0)
            for mk in merge_masks:
                tm_ = tm_ - _bmm(_bmm(tm_, jnp.where(mk, lm, 0.0)), tm_)
            eg = jnp.exp(gcc_b)
            vc = vs_ref[pl.ds(base, rows), hh * hd:(hh + 1) * hd].reshape(g, c, hd)
            uw = _bmm(tm_, jnp.concatenate([vc * beta_b, kc * (beta_b * eg)], axis=2))
            cols = slice(hh * hd, (hh + 1) * hd)
            u_ref[0, pl.ds(base, rows), cols] = uw[:, :, :hd].reshape(rows, hd).astype(u_ref.dtype)
            w_ref[0, pl.ds(base, rows), cols] = uw[:, :, hd:].reshape(rows, hd).astype(w_ref.dtype)
            qd_ref[0, pl.ds(base, rows), cols] = (qc * eg).reshape(rows, hd).astype(qd_ref.dtype)
            kd_ref[0, pl.ds(base, rows), cols] = (kc * jnp.exp(gl_b - gcc_b)).reshape(rows, hd).astype(kd_ref.dtype)
            a_ref[0, pl.ds(base, rows), hh * c:(hh + 1) * c] = (qk * decay).reshape(rows, c).astype(a_ref.dtype)
        return carry

    lax.fori_loop(0, t_len // rows, group_step, 0)


def dn_prep(proj, conv_w, gates, gcr):
    b, t, _ = proj.shape
    hd = DN_HEAD_DIM
    nq = DN_QK_HEADS
    v_blk0 = 2 * DN_KEY_DIM // (2 * hd)
    wide = pl.BlockSpec((1, t, 2 * hd), lambda bi, j: (bi, 0, j))
    big = jax.ShapeDtypeStruct((b, t, DN_VAL_DIM), BF16)
    return pl.pallas_call(
        functools.partial(_dn_prep_body, t_len=t),
        grid=(b, nq),
        in_specs=[
            pl.BlockSpec((1, t, hd), lambda bi, j: (bi, 0, j)),
            pl.BlockSpec((1, t, hd), lambda bi, j: (bi, 0, nq + j)),
            pl.BlockSpec((1, t, 2 * hd), lambda bi, j: (bi, 0, v_blk0 + j)),
            pl.BlockSpec((4, hd), lambda bi, j: (0, j)),
            pl.BlockSpec((4, hd), lambda bi, j: (0, nq + j)),
            pl.BlockSpec((4, 2 * hd), lambda bi, j: (0, v_blk0 + j)),
            pl.BlockSpec((1, t, LANES), lambda bi, j: (bi, 0, 0)),
            pl.BlockSpec((1, 1, 2, t), lambda bi, j: (bi, j, 0, 0)),
        ],
        out_specs=[wide, wide, wide, wide, pl.BlockSpec((1, t, 2 * CHUNK), lambda bi, j: (bi, 0, j))],
        out_shape=[big, big, big, big, jax.ShapeDtypeStruct((b, t, DN_V_HEADS * CHUNK), BF16)],
        scratch_shapes=[
            pltpu.VMEM((t, hd), F32), pltpu.VMEM((t, hd), F32), pltpu.VMEM((t, 2 * hd), F32),
            pltpu.VMEM((2, t, LANES), F32), pltpu.VMEM((2, t, LANES), F32),
        ],
        compiler_params=_params("parallel", "arbitrary"),
        name="dn_prep",
    )(proj, proj, proj, conv_w, conv_w, conv_w, gates, gcr)


DN_REC_HEADS = 8
DN_REC_ROWS = 512


def _dn_rec_body(u_ref, w_ref, qd_ref, kd_ref, a_ref, z_ref, gl_ref, onorm_ref, o_ref, s_ref, *, t_len):
    hd = DN_HEAD_DIM
    c = CHUNK

    @pl.when(pl.program_id(2) == 0)
    def _():
        s_ref[...] = jnp.zeros_like(s_ref)

    onorm = onorm_ref[...]

    def step(ci, carry):
        r0 = pl.multiple_of(ci * c, c)
        egl = jnp.exp(gl_ref[0, 0, ci])
        for h in range(DN_REC_HEADS):
            cols = slice(h * hd, (h + 1) * hd)
            st = s_ref[h]
            stb = st.astype(BF16)
            wq = jnp.concatenate([w_ref[0, pl.ds(r0, c), cols], qd_ref[0, pl.ds(r0, c), cols]], axis=0)
            r = jnp.dot(wq, stb, preferred_element_type=F32)
            vb = (u_ref[0, pl.ds(r0, c), cols].astype(F32) - r[:c]).astype(BF16)
            o = r[c:] + jnp.dot(a_ref[0, pl.ds(r0, c), h * c:(h + 1) * c], vb, preferred_element_type=F32)
            s_ref[h] = st * egl[:, h:h + 1] + lax.dot_general(
                kd_ref[0, pl.ds(r0, c), cols], vb, (((0,), (0,)), ((), ())), preferred_element_type=F32)
            y = o * lax.rsqrt(jnp.mean(o * o, axis=-1, keepdims=True) + EPS) * onorm
            zz = z_ref[0, pl.ds(r0, c), cols]
            o_ref[0, pl.ds(r0, c), cols] = (y * (zz * _sigmoid(zz))).astype(o_ref.dtype)
        return carry

    lax.fori_loop(0, t_len // c, step, 0)


def dn_rec(u, w, qd, kd, a, proj, gl, out_norm):
    b, t, _ = u.shape
    hd = DN_HEAD_DIM
    hb = DN_REC_HEADS
    ng = DN_V_HEADS // hb
    tt = min(DN_REC_ROWS, t)
    z_blk0 = DN_CONV_DIM // (hb * hd)
    wide = pl.BlockSpec((1, tt, hb * hd), lambda bi, g, ti: (bi, ti, g))
    return pl.pallas_call(
        functools.partial(_dn_rec_body, t_len=tt),
        grid=(b, ng, t // tt),
        in_specs=[
            wide, wide, wide, wide,
            pl.BlockSpec((1, tt, hb * CHUNK), lambda bi, g, ti: (bi, ti, g)),
            pl.BlockSpec((1, tt, hb * hd), lambda bi, g, ti: (bi, ti, z_blk0 + g)),
            pl.BlockSpec((1, 1, tt // CHUNK, 1, hb), lambda bi, g, ti: (bi, g, ti, 0, 0)),
            pl.BlockSpec((1, hd), lambda bi, g, ti: (0, 0)),
        ],
        out_specs=wide,
        out_shape=jax.ShapeDtypeStruct((b, t, DN_VAL_DIM), BF16),
        scratch_shapes=[pltpu.VMEM((hb, hd, hd), F32)],
        compiler_params=_params("parallel", "parallel", "arbitrary"),
        name="dn_rec",
    )(u, w, qd, kd, a, proj, gl, out_norm.reshape(1, hd))


def gated_deltanet_layer(h2, b, t, norm_g, w_in, conv_w, a_log, dt_bias, out_norm, w_out):
    n = b * t
    u = rms_norm(h2, norm_g, BF16)
    n_qkvz = DN_CONV_DIM + DN_VAL_DIM
    proj = matmul(u, w_in[:, :n_qkvz].astype(BF16), F32).reshape(b, t, n_qkvz)
    w_ba = jnp.pad(w_in[:, n_qkvz:], ((0, 0), (0, LANES - 2 * DN_V_HEADS))).astype(BF16)
    pba = matmul(u, w_ba, F32)
    pad = lambda p: jnp.pad(p.astype(F32), (DN_V_HEADS, LANES - 2 * DN_V_HEADS)).reshape(1, LANES)
    gates = dn_gates(pba, pad(a_log), pad(dt_bias))
    gc = gates[:, DN_V_HEADS:2 * DN_V_HEADS].reshape(b, t, DN_V_HEADS)
    gcr = gc.reshape(b, t, DN_QK_HEADS, 2).transpose(0, 2, 3, 1)
    ng = DN_V_HEADS // DN_REC_HEADS
    gl = gc[:, CHUNK - 1::CHUNK, :].reshape(b, t // CHUNK, ng, 1, DN_REC_HEADS).transpose(0, 2, 1, 3, 4)
    uu, ww, qd, kd, am = dn_prep(proj, conv_w, gates.reshape(b, t, LANES), gcr)
    o = dn_rec(uu, ww, qd, kd, am, proj, gl, out_norm)
    return matmul(o.reshape(n, DN_VAL_DIM), w_out.astype(BF16), F32, residual=h2)


def _dsa_prep_body(p_ref, qn_ref, kvn_ref, kin_ref, q_out, kv_out, ki_out, w_out):
    def nrm(x, g):
        return x * lax.rsqrt(jnp.mean(x * x, axis=-1, keepdims=True) + EPS) * g

    a, b2, c2 = Q_LORA, Q_LORA + KV_LORA, Q_LORA + KV_LORA + IDX_DIM
    q_out[...] = nrm(p_ref[:, 0:a], qn_ref[...]).astype(q_out.dtype)
    kv_out[...] = nrm(p_ref[:, a:b2], kvn_ref[...]).astype(kv_out.dtype)
    ki_out[...] = nrm(p_ref[:, b2:c2], kin_ref[...]).astype(ki_out.dtype)
    w_out[...] = p_ref[:, c2:c2 + LANES] * (IDX_HEADS ** -0.5 * IDX_DIM ** -0.5)


def dsa_prep(proj, q_norm, kv_norm, kidx_norm, tm=512):
    n, width = proj.shape
    row = lambda i: (i, 0)
    fix = lambda i: (0, 0)
    return pl.pallas_call(
        _dsa_prep_body,
        grid=(n // tm,),
        in_specs=[pl.BlockSpec((tm, width), row), pl.BlockSpec((1, Q_LORA), fix),
                  pl.BlockSpec((1, KV_LORA), fix), pl.BlockSpec((1, IDX_DIM), fix)],
        out_specs=[pl.BlockSpec((tm, Q_LORA), row), pl.BlockSpec((tm, KV_LORA), row),
                   pl.BlockSpec((tm, IDX_DIM), row), pl.BlockSpec((tm, LANES), row)],
        out_shape=[jax.ShapeDtypeStruct((n, Q_LORA), BF16), jax.ShapeDtypeStruct((n, KV_LORA), BF16),
                   jax.ShapeDtypeStruct((n, IDX_DIM), BF16), jax.ShapeDtypeStruct((n, LANES), F32)],
        compiler_params=_params("parallel"),
        name="dsa_prep",
    )(proj, q_norm.reshape(1, -1), kv_norm.reshape(1, -1), kidx_norm.reshape(1, -1))


def _qabs_body(q_ref, w_ref, o_ref, *, r):
    for h in range(DSA_HEADS):
        res = jnp.dot(q_ref[:, h * DSA_HEAD_DIM:(h + 1) * DSA_HEAD_DIM], w_ref[h],
                      preferred_element_type=F32) * (DSA_HEAD_DIM ** -0.5)
        for rr in range(r):
            o_ref[rr, h] = res[rr * QBLOCK:(rr + 1) * QBLOCK].astype(o_ref.dtype)


def q_absorb(q_all, w_uk, tm=256):
    n = q_all.shape[0]
    r = tm // QBLOCK
    hw = DSA_HEADS * DSA_HEAD_DIM
    return pl.pallas_call(
        functools.partial(_qabs_body, r=r),
        grid=(n // tm,),
        in_specs=[pl.BlockSpec((tm, hw), lambda i: (i, 0)),
                  pl.BlockSpec((DSA_HEADS, DSA_HEAD_DIM, KV_LORA), lambda i: (0, 0, 0))],
        out_specs=pl.BlockSpec((r, DSA_HEADS, QBLOCK, KV_LORA), lambda i: (i, 0, 0, 0)),
        out_shape=jax.ShapeDtypeStruct((n // QBLOCK, DSA_HEADS, QBLOCK, KV_LORA), BF16),
        compiler_params=_params("parallel"),
        name="q_absorb",
    )(q_all, w_uk)


def _vup_body(o_ref, w_ref, out_ref, *, r):
    for h in range(DSA_HEADS):
        for rr in range(r):
            out_ref[rr * QBLOCK:(rr + 1) * QBLOCK, h * DSA_HEAD_DIM:(h + 1) * DSA_HEAD_DIM] = jnp.dot(
                o_ref[rr, h], w_ref[h], preferred_element_type=F32).astype(out_ref.dtype)


def v_up(o_lat, w_uv, tm=256):
    nb = o_lat.shape[0]
    r = tm // QBLOCK
    hw = DSA_HEADS * DSA_HEAD_DIM
    return pl.pallas_call(
        functools.partial(_vup_body, r=r),
        grid=(nb // r,),
        in_specs=[pl.BlockSpec((r, DSA_HEADS, QBLOCK, KV_LORA), lambda i: (i, 0, 0, 0)),
                  pl.BlockSpec((DSA_HEADS, KV_LORA, DSA_HEAD_DIM), lambda i: (0, 0, 0))],
        out_specs=pl.BlockSpec((tm, hw), lambda i: (i, 0)),
        out_shape=jax.ShapeDtypeStruct((nb * QBLOCK, hw), BF16),
        compiler_params=_params("parallel"),
        name="v_up",
    )(o_lat, w_uv)


def _dsa_core_body(qidx_ref, widx_ref, kidx_ref, ckv_ref, qabs_ref, o_ref,
                   keys_ref, tau_ref, bias_ref, s_ref, p_ref, m_ref, l_ref, acc_ref, *, top_k, kt):
    i = pl.program_id(1)
    nq = QBLOCK
    nkt = keys_ref.shape[0]
    n_act = ((i + 1) * nq + kt - 1) // kt
    wt = widx_ref[0].T
    krow = lax.broadcasted_iota(jnp.int32, (kt, nq), 0)
    qcol = lax.broadcasted_iota(jnp.int32, (kt, nq), 1)
    q_chunk = (i * nq + qcol) // CHUNK
    tn = (((1,), (1,)), ((), ()))
    sub = 8

    def index_tile(jt, carry):
        k0 = pl.multiple_of(jt * kt, kt)
        kblk = kidx_ref[0, pl.ds(k0, kt), :]
        sc = jnp.zeros((kt, nq), F32)
        for h in range(IDX_HEADS):
            d = lax.dot_general(kblk, qidx_ref[0, :, h * IDX_DIM:(h + 1) * IDX_DIM], tn,
                                preferred_element_type=F32)
            sc = sc + jnp.maximum(d, 0.0) * wt[h:h + 1, :]
        bits = pltpu.bitcast(sc, jnp.int32)
        key = bits ^ ((bits >> 31) & 0x7FFFFFFF)
        adm = ((k0 + krow) // CHUNK) <= q_chunk
        keys_ref[jt] = jnp.where(adm, key, INT_MIN)
        return carry

    lax.fori_loop(0, n_act, index_tile, 0)

    for jt in range(nkt):
        @pl.when(jt >= n_act)
        def _():
            keys_ref[jt] = jnp.full((kt, nq), INT_MIN, jnp.int32)

    def bisect_over(ntiles):
        def bisect(it, prefix):
            cand = prefix | lax.shift_left(jnp.int32(1), 31 - it)
            cand_s = cand ^ INT_MIN
            cnt = jnp.zeros((sub, nq), F32)
            for jt in range(ntiles):
                hit = jnp.where(keys_ref[jt] >= cand_s, 1.0, 0.0)
                cnt = cnt + jnp.sum(hit.reshape(kt // sub, sub, nq), axis=0)
            return jnp.where(jnp.sum(cnt, axis=0, keepdims=True) >= top_k, cand, prefix)

        prefix = lax.fori_loop(0, 32, bisect, jnp.zeros((1, nq), jnp.int32))
        tau_ref[...] = jnp.broadcast_to(jnp.maximum(prefix ^ INT_MIN, INT_MIN + 1), tau_ref.shape)

    step = 2 if nkt % 2 == 0 else 1
    for ntiles in range(step, nkt + 1, step):
        @pl.when((n_act > ntiles - step) & (n_act <= ntiles))
        def _():
            bisect_over(ntiles)

    tau = tau_ref[0:1, :]

    rows = DSA_HEADS * nq
    rep = kt // LANES
    m_ref[...] = jnp.full(m_ref.shape, M_INIT, F32)
    l_ref[...] = jnp.zeros_like(l_ref)
    acc_ref[...] = jnp.zeros_like(acc_ref)

    def attend_tile(jt, carry):
        k0 = pl.multiple_of(jt * kt, kt)
        ck = ckv_ref[0, pl.ds(k0, kt), :]
        bias_ref[...] = jnp.where(keys_ref[jt] >= tau, 0.0, MASKED).T
        s_ref[...] = lax.dot_general(qabs_ref[0].reshape(rows, KV_LORA), ck, tn, preferred_element_type=F32)

        for h in range(DSA_HEADS):
            hr = slice(h * nq, (h + 1) * nq)
            s = s_ref[hr, :] + bias_ref[...]
            m_old = m_ref[hr, :]
            m_new = jnp.maximum(m_old, jnp.max(s, axis=1, keepdims=True))
            alpha = jnp.exp(m_old - m_new)
            p = jnp.exp(s - jnp.concatenate([m_new] * rep, axis=1))
            l_ref[hr, :] = alpha * l_ref[hr, :] + jnp.sum(p, axis=1, keepdims=True)
            m_ref[hr, :] = m_new
            acc_ref[hr, :] = acc_ref[hr, :] * jnp.concatenate([alpha] * (KV_LORA // LANES), axis=1)
            p_ref[hr, :] = p.astype(BF16)
        acc_ref[...] += jnp.dot(p_ref[...], ck, preferred_element_type=F32)
        return carry

    lax.fori_loop(0, n_act, attend_tile, 0)
    inv_l = 1.0 / l_ref[...]
    out = acc_ref[...] * jnp.concatenate([inv_l] * (KV_LORA // LANES), axis=1)
    o_ref[0] = out.reshape(DSA_HEADS, nq, KV_LORA).astype(o_ref.dtype)


def dsa_core(q_all, widx, kidx, ckv, qabs, b, t, kt=256):
    nblk = t // QBLOCK
    top_k = min(IDX_TOPK, t // 4)
    hw = IDX_HEADS * IDX_DIM
    rows = DSA_HEADS * QBLOCK
    return pl.pallas_call(
        functools.partial(_dsa_core_body, top_k=top_k, kt=kt),
        grid=(b, nblk),
        in_specs=[
            pl.BlockSpec((1, QBLOCK, hw), lambda bi, i: (bi, i, 1)),
            pl.BlockSpec((1, QBLOCK, LANES), lambda bi, i: (bi, i, 0)),
            pl.BlockSpec((1, t, IDX_DIM), lambda bi, i: (bi, 0, 0)),
            pl.BlockSpec((1, t, KV_LORA), lambda bi, i: (bi, 0, 0)),
            pl.BlockSpec((1, DSA_HEADS, QBLOCK, KV_LORA), lambda bi, i: (bi * nblk + i, 0, 0, 0)),
        ],
        out_specs=pl.BlockSpec((1, DSA_HEADS, QBLOCK, KV_LORA), lambda bi, i: (bi * nblk + i, 0, 0, 0)),
        out_shape=jax.ShapeDtypeStruct((b * nblk, DSA_HEADS, QBLOCK, KV_LORA), BF16),
        scratch_shapes=[pltpu.VMEM((t // kt, kt, QBLOCK), jnp.int32), pltpu.VMEM((8, QBLOCK), jnp.int32),
                        pltpu.VMEM((QBLOCK, kt), F32),
                        pltpu.VMEM((rows, kt), F32), pltpu.VMEM((rows, kt), BF16),
                        pltpu.VMEM((rows, LANES), F32), pltpu.VMEM((rows, LANES), F32),
                        pltpu.VMEM((rows, KV_LORA), F32)],
        compiler_params=_params("parallel", "arbitrary"),
        name="dsa_core",
    )(q_all.reshape(b, t, -1), widx.reshape(b, t, LANES), kidx.reshape(b, t, IDX_DIM),
      ckv.reshape(b, t, KV_LORA), qabs)


def dsa_layer(h2, b, t, norm_g, w_in, q_norm, kv_norm, kidx_norm, w_uq, w_uk, w_uv, w_out):
    u = rms_norm(h2, norm_g, BF16)
    width = Q_LORA + KV_LORA + IDX_DIM + LANES
    w_in_p = jnp.pad(w_in, ((0, 0), (0, width - w_in.shape[1]))).astype(BF16)
    proj = matmul(u, w_in_p, F32)
    qlat, ckv, kidx, widx = dsa_prep(proj, q_norm, kv_norm, kidx_norm)
    q_all = matmul(qlat, w_uq.astype(BF16), BF16)
    qabs = q_absorb(q_all, w_uk.astype(BF16))
    o_lat = dsa_core(q_all, widx, kidx, ckv, qabs, b, t)
    o = v_up(o_lat, w_uv.astype(BF16))
    return matmul(o, w_out.astype(BF16), F32, residual=h2)


def mlp(h2, norm_g, w_up, w_down):
    u = rms_norm(h2, norm_g, BF16)
    a = matmul(u, w_up.astype(BF16), BF16, relu2=True)
    return matmul(a, w_down.astype(BF16), F32, residual=h2)


def kernel(x, norm_mix, norm_mlp, norm_final, dn_w_in, dn_conv_w, dn_a_log, dn_dt_bias, dn_out_norm, dn_w_out, dsa_w_in, dsa_q_norm, dsa_kv_norm, dsa_kidx_norm, dsa_w_uq, dsa_w_uk, dsa_w_uv, dsa_w_out, mlp_w_up, mlp_w_down):
    b, t, d = x.shape
    h2 = x.reshape(b * t, d)
    for i in range(norm_mix.shape[0]):
        j = i // 2
        if i % 2 == 0:
            h2 = gated_deltanet_layer(h2, b, t, norm_mix[i], dn_w_in[j], dn_conv_w[j], dn_a_log[j],
                                      dn_dt_bias[j], dn_out_norm[j], dn_w_out[j])
        else:
            h2 = dsa_layer(h2, b, t, norm_mix[i], dsa_w_in[j], dsa_q_norm[j], dsa_kv_norm[j],
                           dsa_kidx_norm[j], dsa_w_uq[j], dsa_w_uk[j], dsa_w_uv[j], dsa_w_out[j])
        h2 = mlp(h2, norm_mlp[i], mlp_w_up[i], mlp_w_down[i])
    return rms_norm(h2, norm_final, x.dtype).reshape(b, t, d)
```

```python
import functools

import jax
import jax.numpy as jnp
from jax import lax
from jax.experimental import pallas as pl
from jax.experimental.pallas import tpu as pltpu

F32 = jnp.float32
BF16 = jnp.bfloat16
EPS = 1e-6
CHUNK = 64
QBLOCK = 128
DN_QK_HEADS = 16
DN_V_HEADS = 32
DN_HEAD_DIM = 128
DN_KEY_DIM = DN_QK_HEADS * DN_HEAD_DIM
DN_VAL_DIM = DN_V_HEADS * DN_HEAD_DIM
DN_CONV_DIM = 2 * DN_KEY_DIM + DN_VAL_DIM
DSA_HEADS = 16
DSA_HEAD_DIM = 128
Q_LORA = 512
KV_LORA = 256
IDX_HEADS = 16
IDX_DIM = 128
IDX_TOPK = 256
LANES = 128
VMEM_LIMIT_BYTES = 48 * 1024 * 1024
INT_MIN = -(2 ** 31)
MASKED = -1e30
M_INIT = -1e20


def _params(*sem):
    return pltpu.CompilerParams(dimension_semantics=sem, vmem_limit_bytes=VMEM_LIMIT_BYTES)


def _sigmoid(x):
    return 1.0 / (1.0 + jnp.exp(-x))


def _norm_body(x_ref, g_ref, o_ref):
    x = x_ref[...]
    y = x * lax.rsqrt(jnp.mean(x * x, axis=-1, keepdims=True) + EPS) * g_ref[...]
    o_ref[...] = y.astype(o_ref.dtype)


def rms_norm(x, g, out_dtype, tm=512):
    n, d = x.shape
    return pl.pallas_call(
        _norm_body,
        grid=(n // tm,),
        in_specs=[pl.BlockSpec((tm, d), lambda i: (i, 0)), pl.BlockSpec((1, d), lambda i: (0, 0))],
        out_specs=pl.BlockSpec((tm, d), lambda i: (i, 0)),
        out_shape=jax.ShapeDtypeStruct((n, d), out_dtype),
        compiler_params=_params("parallel"),
        name="rms_norm",
    )(x, g.reshape(1, d))


def _mm_body(a_ref, w_ref, *rest, nk, relu2, residual):
    if residual:
        r_ref, o_ref = rest[0], rest[1]
    else:
        r_ref, o_ref = None, rest[0]
    acc_ref = rest[-1] if nk > 1 else None

    def finish(acc):
        if relu2:
            acc = jnp.square(jnp.maximum(acc, 0.0))
        if residual:
            acc = acc + r_ref[...]
        o_ref[...] = acc.astype(o_ref.dtype)

    part = jnp.dot(a_ref[...], w_ref[...], preferred_element_type=F32)
    if nk == 1:
        finish(part)
        return
    k = pl.program_id(2)

    @pl.when(k == 0)
    def _():
        acc_ref[...] = part

    @pl.when(k > 0)
    def _():
        acc_ref[...] += part

    @pl.when(k == nk - 1)
    def _():
        finish(acc_ref[...])


def matmul(a, w, out_dtype, *, relu2=False, residual=None, tm=1024, tn=1024, tk=2048):
    m, kdim = a.shape
    n = w.shape[1]
    tm, tn, tk = min(tm, m), min(tn, n), min(tk, kdim)
    assert m % tm == 0 and n % tn == 0 and kdim % tk == 0
    nk = kdim // tk
    in_specs = [pl.BlockSpec((tm, tk), lambda i, j, k: (i, k)), pl.BlockSpec((tk, tn), lambda i, j, k: (k, j))]
    args = [a, w]
    if residual is not None:
        in_specs.append(pl.BlockSpec((tm, tn), lambda i, j, k: (i, j)))
        args.append(residual)
    return pl.pallas_call(
        functools.partial(_mm_body, nk=nk, relu2=relu2, residual=residual is not None),
        grid=(m // tm, n // tn, nk),
        in_specs=in_specs,
        out_specs=pl.BlockSpec((tm, tn), lambda i, j, k: (i, j)),
        out_shape=jax.ShapeDtypeStruct((m, n), out_dtype),
        scratch_shapes=[pltpu.VMEM((tm, tn), F32)] if nk > 1 else [],
        compiler_params=_params("parallel", "parallel", "arbitrary"),
        name="matmul",
    )(*args)


def _dn_gates_body(x_ref, alog_ref, dt_ref, o_ref):
    x = x_ref[...]
    g = -jnp.exp(alog_ref[...]) * (jnp.maximum(x + dt_ref[...], 0.0)
                                   + jnp.log(1.0 + jnp.exp(-jnp.abs(x + dt_ref[...]))))
    pos = lax.broadcasted_iota(jnp.int32, x.shape, 0) % CHUNK
    s = 1
    while s < CHUNK:
        g = g + jnp.where(pos >= s, pltpu.roll(g, s, axis=0), 0.0)
        s *= 2
    lane = lax.broadcasted_iota(jnp.int32, x.shape, 1)
    o_ref[...] = jnp.where(lane < DN_V_HEADS, _sigmoid(x), g)


def dn_gates(pba, alog_pad, dt_pad, tm=512):
    n = pba.shape[0]
    return pl.pallas_call(
        _dn_gates_body,
        grid=(n // tm,),
        in_specs=[pl.BlockSpec((tm, LANES), lambda i: (i, 0)),
                  pl.BlockSpec((1, LANES), lambda i: (0, 0)),
                  pl.BlockSpec((1, LANES), lambda i: (0, 0))],
        out_specs=pl.BlockSpec((tm, LANES), lambda i: (i, 0)),
        out_shape=jax.ShapeDtypeStruct((n, LANES), F32),
        compiler_params=_params("parallel"),
        name="dn_gates",
    )(pba, alog_pad, dt_pad)


def _conv_silu(ref, cw_ref, t0, tt):
    cw = cw_ref[...]
    acc = ref[0, t0:t0 + tt, :] * cw[3:4, :]
    for s in (1, 2, 3):
        if t0 == 0:
            x = ref[0, 0:tt, :]
            rows = lax.broadcasted_iota(jnp.int32, x.shape, 0)
            xs = jnp.where(rows >= s, pltpu.roll(x, s, axis=0), 0.0)
        else:
            xs = ref[0, t0 - s:t0 - s + tt, :]
        acc = acc + xs * cw[3 - s:4 - s, :]
    return acc * _sigmoid(acc)


def _l2norm(t):
    return t * lax.rsqrt(jnp.sum(t * t, axis=-1, keepdims=True) + EPS)


def _bmm(a, b):
    return jnp.einsum("nij,njk->nik", a.astype(BF16), b.astype(BF16), preferred_element_type=F32)


def _bmm_nt(a, b):
    return jnp.einsum("nid,njd->nij", a.astype(BF16), b.astype(BF16), preferred_element_type=F32)


DN_GROUP = 16


def _dn_prep_body(q_ref, k_ref, v_ref, cwq_ref, cwk_ref, cwv_ref, gates_ref, gcr_ref,
                  u_ref, w_ref, qd_ref, kd_ref, a_ref, qs_ref, ks_ref, vs_ref, bb_ref, gb_ref, *, t_len):
    j = pl.program_id(1)
    tt = 256
    hd = DN_HEAD_DIM
    for ti in range(t_len // tt):
        t0 = ti * tt
        qs_ref[t0:t0 + tt, :] = _l2norm(_conv_silu(q_ref, cwq_ref, t0, tt)) * (hd ** -0.5)
        ks_ref[t0:t0 + tt, :] = _l2norm(_conv_silu(k_ref, cwk_ref, t0, tt))
        vs_ref[t0:t0 + tt, :] = _conv_silu(v_ref, cwv_ref, t0, tt)
        gt = gates_ref[0, t0:t0 + tt, :]
        lane = lax.broadcasted_iota(jnp.int32, gt.shape, 1)
        for hh in range(2):
            col = 2 * j + hh
            bsel = jnp.sum(jnp.where(lane == col, gt, 0.0), axis=1, keepdims=True)
            gsel = jnp.sum(jnp.where(lane == col + DN_V_HEADS, gt, 0.0), axis=1, keepdims=True)
            bb_ref[hh, t0:t0 + tt, :] = jnp.broadcast_to(bsel, (tt, LANES))
            gb_ref[hh, t0:t0 + tt, :] = jnp.broadcast_to(gsel, (tt, LANES))

    c = CHUNK
    g = DN_GROUP
    rows = g * c
    ri = lax.broadcasted_iota(jnp.int32, (c, c), 0)
    ci = lax.broadcasted_iota(jnp.int32, (c, c), 1)
    lower = ri >= ci
    strict = ri > ci
    eye = jnp.where(ri == ci, 1.0, 0.0)
    pair_mask = strict & ((ri // 2) == (ci // 2))
    merge_masks = [((ri // (2 * s)) == (ci // (2 * s))) & ((ri // s) != (ci // s)) & strict
                   for s in (2, 4, 8, 16, 32)]

    def group_step(gi, carry):
        base = pl.multiple_of(gi * rows, rows)
        kc = ks_ref[pl.ds(base, rows), :].reshape(g, c, hd)
        qc = qs_ref[pl.ds(base, rows), :].reshape(g, c, hd)
        kk1 = _bmm_nt(kc, kc)
        qk1 = _bmm_nt(qc, kc)
        two = lambda x: jnp.concatenate([x, x], axis=0)
        kk, qk, kc2, qc2 = two(kk1), two(qk1), two(kc), two(qc)
        beta_b = jnp.concatenate([bb_ref[hh, pl.ds(base, rows), :].reshape(g, c, LANES) for hh in range(2)], axis=0)
        gcc_b = jnp.concatenate([gb_ref[hh, pl.ds(base, rows), :].reshape(g, c, LANES) for hh in range(2)], axis=0)
        grow = gcr_ref[0, 0, :, pl.ds(base, rows)]
        gcr = jnp.stack([grow[hh:hh + 1, n * c:(n + 1) * c] for hh in range(2) for n in range(g)], axis=0)
        vc = jnp.concatenate([vs_ref[pl.ds(base, rows), hh * hd:(hh + 1) * hd].reshape(g, c, hd)
                              for hh in range(2)], axis=0)
        gl_b = gcc_b[:, c - 1:c, :]
        dm = gcc_b[:, :, :c] - gcr
        decay = jnp.where(lower, jnp.exp(jnp.where(lower, dm, 0.0)), 0.0)
        lm = jnp.where(strict, kk * beta_b[:, :, :c] * decay, 0.0)
        tm_ = eye - jnp.where(pair_mask, lm, 0.0)
        for mk in merge_masks:
            tm_ = tm_ - _bmm(_bmm(tm_, jnp.where(mk, lm, 0.0)), tm_)
        eg = jnp.exp(gcc_b)
        uw = _bmm(tm_, jnp.concatenate([vc * beta_b, kc2 * (beta_b * eg)], axis=2))
        qd = qc2 * eg
        kd = kc2 * jnp.exp(gl_b - gcc_b)
        am = qk * decay
        for hh in range(2):
            cols = slice(hh * hd, (hh + 1) * hd)
            inst = slice(hh * g, (hh + 1) * g)
            u_ref[0, pl.ds(base, rows), cols] = uw[inst, :, :hd].reshape(rows, hd).astype(u_ref.dtype)
            w_ref[0, pl.ds(base, rows), cols] = uw[inst, :, hd:].reshape(rows, hd).astype(w_ref.dtype)
            qd_ref[0, pl.ds(base, rows), cols] = qd[inst].reshape(rows, hd).astype(qd_ref.dtype)
            kd_ref[0, pl.ds(base, rows), cols] = kd[inst].reshape(rows, hd).astype(kd_ref.dtype)
            a_ref[0, pl.ds(base, rows), hh * c:(hh + 1) * c] = am[inst].reshape(rows, c).astype(a_ref.dtype)
        return carry

    lax.fori_loop(0, t_len // rows, group_step, 0)


def dn_prep(proj, conv_w, gates, gcr):
    b, t, _ = proj.shape
    hd = DN_HEAD_DIM
    nq = DN_QK_HEADS
    v_blk0 = 2 * DN_KEY_DIM // (2 * hd)
    wide = pl.BlockSpec((1, t, 2 * hd), lambda bi, j: (bi, 0, j))
    big = jax.ShapeDtypeStruct((b, t, DN_VAL_DIM), BF16)
    return pl.pallas_call(
        functools.partial(_dn_prep_body, t_len=t),
        grid=(b, nq),
        in_specs=[
            pl.BlockSpec((1, t, hd), lambda bi, j: (bi, 0, j)),
            pl.BlockSpec((1, t, hd), lambda bi, j: (bi, 0, nq + j)),
            pl.BlockSpec((1, t, 2 * hd), lambda bi, j: (bi, 0, v_blk0 + j)),
            pl.BlockSpec((4, hd), lambda bi, j: (0, j)),
            pl.BlockSpec((4, hd), lambda bi, j: (0, nq + j)),
            pl.BlockSpec((4, 2 * hd), lambda bi, j: (0, v_blk0 + j)),
            pl.BlockSpec((1, t, LANES), lambda bi, j: (bi, 0, 0)),
            pl.BlockSpec((1, 1, 2, t), lambda bi, j: (bi, j, 0, 0)),
        ],
        out_specs=[wide, wide, wide, wide, pl.BlockSpec((1, t, 2 * CHUNK), lambda bi, j: (bi, 0, j))],
        out_shape=[big, big, big, big, jax.ShapeDtypeStruct((b, t, DN_V_HEADS * CHUNK), BF16)],
        scratch_shapes=[
            pltpu.VMEM((t, hd), F32), pltpu.VMEM((t, hd), F32), pltpu.VMEM((t, 2 * hd), F32),
            pltpu.VMEM((2, t, LANES), F32), pltpu.VMEM((2, t, LANES), F32),
        ],
        compiler_params=_params("parallel", "arbitrary"),
        name="dn_prep",
    )(proj, proj, proj, conv_w, conv_w, conv_w, gates, gcr)


DN_REC_HEADS = 8
DN_REC_ROWS = 512


def _dn_rec_body(u_ref, w_ref, qd_ref, kd_ref, a_ref, z_ref, gl_ref, onorm_ref, o_ref, s_ref, *, t_len):
    hd = DN_HEAD_DIM
    c = CHUNK

    @pl.when(pl.program_id(2) == 0)
    def _():
        s_ref[...] = jnp.zeros_like(s_ref)

    onorm = onorm_ref[...]

    def step(ci, carry):
        r0 = pl.multiple_of(ci * c, c)
        egl = jnp.exp(gl_ref[0, 0, ci])
        heads = range(DN_REC_HEADS)
        cols = [slice(h * hd, (h + 1) * hd) for h in heads]
        st = [s_ref[h] for h in heads]
        r = [jnp.dot(jnp.concatenate([w_ref[0, pl.ds(r0, c), cols[h]], qd_ref[0, pl.ds(r0, c), cols[h]]], axis=0),
                     st[h].astype(BF16), preferred_element_type=F32) for h in heads]
        vb = [(u_ref[0, pl.ds(r0, c), cols[h]].astype(F32) - r[h][:c]).astype(BF16) for h in heads]
        upd = [lax.dot_general(kd_ref[0, pl.ds(r0, c), cols[h]], vb[h], (((0,), (0,)), ((), ())),
                               preferred_element_type=F32) for h in heads]
        o = [r[h][c:] + jnp.dot(a_ref[0, pl.ds(r0, c), h * c:(h + 1) * c], vb[h], preferred_element_type=F32)
             for h in heads]
        for h in heads:
            s_ref[h] = st[h] * egl[:, h:h + 1] + upd[h]
        for h in heads:
            y = o[h] * lax.rsqrt(jnp.mean(o[h] * o[h], axis=-1, keepdims=True) + EPS) * onorm
            zz = z_ref[0, pl.ds(r0, c), cols[h]]
            o_ref[0, pl.ds(r0, c), cols[h]] = (y * (zz * _sigmoid(zz))).astype(o_ref.dtype)
        return carry

    lax.fori_loop(0, t_len // c, step, 0)


def dn_rec(u, w, qd, kd, a, proj, gl, out_norm):
    b, t, _ = u.shape
    hd = DN_HEAD_DIM
    hb = DN_REC_HEADS
    ng = DN_V_HEADS // hb
    tt = min(DN_REC_ROWS, t)
    z_blk0 = DN_CONV_DIM // (hb * hd)
    wide = pl.BlockSpec((1, tt, hb * hd), lambda bi, g, ti: (bi, ti, g))
    return pl.pallas_call(
        functools.partial(_dn_rec_body, t_len=tt),
        grid=(b, ng, t // tt),
        in_specs=[
            wide, wide, wide, wide,
            pl.BlockSpec((1, tt, hb * CHUNK), lambda bi, g, ti: (bi, ti, g)),
            pl.BlockSpec((1, tt, hb * hd), lambda bi, g, ti: (bi, ti, z_blk0 + g)),
            pl.BlockSpec((1, 1, tt // CHUNK, 1, hb), lambda bi, g, ti: (bi, g, ti, 0, 0)),
            pl.BlockSpec((1, hd), lambda bi, g, ti: (0, 0)),
        ],
        out_specs=wide,
        out_shape=jax.ShapeDtypeStruct((b, t, DN_VAL_DIM), BF16),
        scratch_shapes=[pltpu.VMEM((hb, hd, hd), F32)],
        compiler_params=_params("parallel", "parallel", "arbitrary"),
        name="dn_rec",
    )(u, w, qd, kd, a, proj, gl, out_norm.reshape(1, hd))


def gated_deltanet_layer(h2, b, t, norm_g, w_in, conv_w, a_log, dt_bias, out_norm, w_out):
    n = b * t
    u = rms_norm(h2, norm_g, BF16)
    n_qkvz = DN_CONV_DIM + DN_VAL_DIM
    proj = matmul(u, w_in[:, :n_qkvz].astype(BF16), F32).reshape(b, t, n_qkvz)
    w_ba = jnp.pad(w_in[:, n_qkvz:], ((0, 0), (0, LANES - 2 * DN_V_HEADS))).astype(BF16)
    pba = matmul(u, w_ba, F32)
    pad = lambda p: jnp.pad(p.astype(F32), (DN_V_HEADS, LANES - 2 * DN_V_HEADS)).reshape(1, LANES)
    gates = dn_gates(pba, pad(a_log), pad(dt_bias))
    gc = gates[:, DN_V_HEADS:2 * DN_V_HEADS].reshape(b, t, DN_V_HEADS)
    gcr = gc.reshape(b, t, DN_QK_HEADS, 2).transpose(0, 2, 3, 1)
    ng = DN_V_HEADS // DN_REC_HEADS
    gl = gc[:, CHUNK - 1::CHUNK, :].reshape(b, t // CHUNK, ng, 1, DN_REC_HEADS).transpose(0, 2, 1, 3, 4)
    uu, ww, qd, kd, am = dn_prep(proj, conv_w, gates.reshape(b, t, LANES), gcr)
    o = dn_rec(uu, ww, qd, kd, am, proj, gl, out_norm)
    return matmul(o.reshape(n, DN_VAL_DIM), w_out.astype(BF16), F32, residual=h2)


def _dsa_prep_body(p_ref, qn_ref, kvn_ref, kin_ref, q_out, kv_out, ki_out, w_out):
    def nrm(x, g):
        return x * lax.rsqrt(jnp.mean(x * x, axis=-1, keepdims=True) + EPS) * g

    a, b2, c2 = Q_LORA, Q_LORA + KV_LORA, Q_LORA + KV_LORA + IDX_DIM
    q_out[...] = nrm(p_ref[:, 0:a], qn_ref[...]).astype(q_out.dtype)
    kv_out[...] = nrm(p_ref[:, a:b2], kvn_ref[...]).astype(kv_out.dtype)
    ki_out[...] = nrm(p_ref[:, b2:c2], kin_ref[...]).astype(ki_out.dtype)
    w_out[...] = p_ref[:, c2:c2 + LANES] * (IDX_HEADS ** -0.5 * IDX_DIM ** -0.5)


def dsa_prep(proj, q_norm, kv_norm, kidx_norm, tm=512):
    n, width = proj.shape
    row = lambda i: (i, 0)
    fix = lambda i: (0, 0)
    return pl.pallas_call(
        _dsa_prep_body,
        grid=(n // tm,),
        in_specs=[pl.BlockSpec((tm, width), row), pl.BlockSpec((1, Q_LORA), fix),
                  pl.BlockSpec((1, KV_LORA), fix), pl.BlockSpec((1, IDX_DIM), fix)],
        out_specs=[pl.BlockSpec((tm, Q_LORA), row), pl.BlockSpec((tm, KV_LORA), row),
                   pl.BlockSpec((tm, IDX_DIM), row), pl.BlockSpec((tm, LANES), row)],
        out_shape=[jax.ShapeDtypeStruct((n, Q_LORA), BF16), jax.ShapeDtypeStruct((n, KV_LORA), BF16),
                   jax.ShapeDtypeStruct((n, IDX_DIM), BF16), jax.ShapeDtypeStruct((n, LANES), F32)],
        compiler_params=_params("parallel"),
        name="dsa_prep",
    )(proj, q_norm.reshape(1, -1), kv_norm.reshape(1, -1), kidx_norm.reshape(1, -1))


def _qabs_body(q_ref, w_ref, o_ref, *, r):
    for h in range(DSA_HEADS):
        res = jnp.dot(q_ref[:, h * DSA_HEAD_DIM:(h + 1) * DSA_HEAD_DIM], w_ref[h],
                      preferred_element_type=F32) * (DSA_HEAD_DIM ** -0.5)
        for rr in range(r):
            o_ref[rr, h] = res[rr * QBLOCK:(rr + 1) * QBLOCK].astype(o_ref.dtype)


def q_absorb(q_all, w_uk, tm=256):
    n = q_all.shape[0]
    r = tm // QBLOCK
    hw = DSA_HEADS * DSA_HEAD_DIM
    return pl.pallas_call(
        functools.partial(_qabs_body, r=r),
        grid=(n // tm,),
        in_specs=[pl.BlockSpec((tm, hw), lambda i: (i, 0)),
                  pl.BlockSpec((DSA_HEADS, DSA_HEAD_DIM, KV_LORA), lambda i: (0, 0, 0))],
        out_specs=pl.BlockSpec((r, DSA_HEADS, QBLOCK, KV_LORA), lambda i: (i, 0, 0, 0)),
        out_shape=jax.ShapeDtypeStruct((n // QBLOCK, DSA_HEADS, QBLOCK, KV_LORA), BF16),
        compiler_params=_params("parallel"),
        name="q_absorb",
    )(q_all, w_uk)


def _vup_body(o_ref, w_ref, out_ref, *, r):
    for h in range(DSA_HEADS):
        for rr in range(r):
            out_ref[rr * QBLOCK:(rr + 1) * QBLOCK, h * DSA_HEAD_DIM:(h + 1) * DSA_HEAD_DIM] = jnp.dot(
                o_ref[rr, h], w_ref[h], preferred_element_type=F32).astype(out_ref.dtype)


def v_up(o_lat, w_uv, tm=256):
    nb = o_lat.shape[0]
    r = tm // QBLOCK
    hw = DSA_HEADS * DSA_HEAD_DIM
    return pl.pallas_call(
        functools.partial(_vup_body, r=r),
        grid=(nb // r,),
        in_specs=[pl.BlockSpec((r, DSA_HEADS, QBLOCK, KV_LORA), lambda i: (i, 0, 0, 0)),
                  pl.BlockSpec((DSA_HEADS, KV_LORA, DSA_HEAD_DIM), lambda i: (0, 0, 0))],
        out_specs=pl.BlockSpec((tm, hw), lambda i: (i, 0)),
        out_shape=jax.ShapeDtypeStruct((nb * QBLOCK, hw), BF16),
        compiler_params=_params("parallel"),
        name="v_up",
    )(o_lat, w_uv)


def _dsa_core_body(qidx_ref, widx_ref, kidx_ref, ckv_ref, qabs_ref, o_ref,
                   keys_ref, tau_ref, bias_ref, s_ref, p_ref, m_ref, l_ref, acc_ref, *, top_k, kt):
    i = pl.program_id(1)
    nq = QBLOCK
    nkt = keys_ref.shape[0]
    n_act = ((i + 1) * nq + kt - 1) // kt
    wt = widx_ref[0].T
    krow = lax.broadcasted_iota(jnp.int32, (kt, nq), 0)
    qcol = lax.broadcasted_iota(jnp.int32, (kt, nq), 1)
    q_chunk = (i * nq + qcol) // CHUNK
    tn = (((1,), (1,)), ((), ()))
    sub = 8

    def index_tile(jt, carry):
        k0 = pl.multiple_of(jt * kt, kt)
        kblk = kidx_ref[0, pl.ds(k0, kt), :]
        sc = jnp.zeros((kt, nq), F32)
        for h in range(IDX_HEADS):
            d = lax.dot_general(kblk, qidx_ref[0, :, h * IDX_DIM:(h + 1) * IDX_DIM], tn,
                                preferred_element_type=F32)
            sc = sc + jnp.maximum(d, 0.0) * wt[h:h + 1, :]
        bits = pltpu.bitcast(sc, jnp.int32)
        key = bits ^ ((bits >> 31) & 0x7FFFFFFF)
        adm = ((k0 + krow) // CHUNK) <= q_chunk
        keys_ref[jt] = jnp.where(adm, key, INT_MIN)
        return carry

    lax.fori_loop(0, n_act, index_tile, 0)

    for jt in range(nkt):
        @pl.when(jt >= n_act)
        def _():
            keys_ref[jt] = jnp.full((kt, nq), INT_MIN, jnp.int32)

    def bisect_over(ntiles):
        def bisect(it, prefix):
            cand = prefix | lax.shift_left(jnp.int32(1), 31 - it)
            cand_s = cand ^ INT_MIN
            cnt = jnp.zeros((sub, nq), F32)
            for jt in range(ntiles):
                hit = jnp.where(keys_ref[jt] >= cand_s, 1.0, 0.0)
                cnt = cnt + jnp.sum(hit.reshape(kt // sub, sub, nq), axis=0)
            return jnp.where(jnp.sum(cnt, axis=0, keepdims=True) >= top_k, cand, prefix)

        prefix = lax.fori_loop(0, 32, bisect, jnp.zeros((1, nq), jnp.int32))
        tau_ref[...] = jnp.broadcast_to(jnp.maximum(prefix ^ INT_MIN, INT_MIN + 1), tau_ref.shape)

    step = 2 if nkt % 2 == 0 else 1
    for ntiles in range(step, nkt + 1, step):
        @pl.when((n_act > ntiles - step) & (n_act <= ntiles))
        def _():
            bisect_over(ntiles)

    tau = tau_ref[0:1, :]

    rows = DSA_HEADS * nq
    rep = kt // LANES
    m_ref[...] = jnp.full(m_ref.shape, M_INIT, F32)
    l_ref[...] = jnp.zeros_like(l_ref)
    acc_ref[...] = jnp.zeros_like(acc_ref)

    def attend_tile(jt, carry):
        k0 = pl.multiple_of(jt * kt, kt)
        ck = ckv_ref[0, pl.ds(k0, kt), :]
        bias_ref[...] = jnp.where(keys_ref[jt] >= tau, 0.0, MASKED).T
        s_ref[...] = lax.dot_general(qabs_ref[0].reshape(rows, KV_LORA), ck, tn, preferred_element_type=F32)

        for h in range(DSA_HEADS):
            hr = slice(h * nq, (h + 1) * nq)
            s = s_ref[hr, :] + bias_ref[...]
            m_old = m_ref[hr, :]
            m_new = jnp.maximum(m_old, jnp.max(s, axis=1, keepdims=True))
            alpha = jnp.exp(m_old - m_new)
            p = jnp.exp(s - jnp.concatenate([m_new] * rep, axis=1))
            l_ref[hr, :] = alpha * l_ref[hr, :] + jnp.sum(p, axis=1, keepdims=True)
            m_ref[hr, :] = m_new
            acc_ref[hr, :] = acc_ref[hr, :] * jnp.concatenate([alpha] * (KV_LORA // LANES), axis=1)
            p_ref[hr, :] = p.astype(BF16)
        acc_ref[...] += jnp.dot(p_ref[...], ck, preferred_element_type=F32)
        return carry

    lax.fori_loop(0, n_act, attend_tile, 0)
    inv_l = 1.0 / l_ref[...]
    out = acc_ref[...] * jnp.concatenate([inv_l] * (KV_LORA // LANES), axis=1)
    o_ref[0] = out.reshape(DSA_HEADS, nq, KV_LORA).astype(o_ref.dtype)


def dsa_core(q_all, widx, kidx, ckv, qabs, b, t, kt=256):
    nblk = t // QBLOCK
    top_k = min(IDX_TOPK, t // 4)
    hw = IDX_HEADS * IDX_DIM
    rows = DSA_HEADS * QBLOCK
    return pl.pallas_call(
        functools.partial(_dsa_core_body, top_k=top_k, kt=kt),
        grid=(b, nblk),
        in_specs=[
            pl.BlockSpec((1, QBLOCK, hw), lambda bi, i: (bi, i, 1)),
            pl.BlockSpec((1, QBLOCK, LANES), lambda bi, i: (bi, i, 0)),
            pl.BlockSpec((1, t, IDX_DIM), lambda bi, i: (bi, 0, 0)),
            pl.BlockSpec((1, t, KV_LORA), lambda bi, i: (bi, 0, 0)),
            pl.BlockSpec((1, DSA_HEADS, QBLOCK, KV_LORA), lambda bi, i: (bi * nblk + i, 0, 0, 0)),
        ],
        out_specs=pl.BlockSpec((1, DSA_HEADS, QBLOCK, KV_LORA), lambda bi, i: (bi * nblk + i, 0, 0, 0)),
        out_shape=jax.ShapeDtypeStruct((b * nblk, DSA_HEADS, QBLOCK, KV_LORA), BF16),
        scratch_shapes=[pltpu.VMEM((t // kt, kt, QBLOCK), jnp.int32), pltpu.VMEM((8, QBLOCK), jnp.int32),
                        pltpu.VMEM((QBLOCK, kt), F32),
                        pltpu.VMEM((rows, kt), F32), pltpu.VMEM((rows, kt), BF16),
                        pltpu.VMEM((rows, LANES), F32), pltpu.VMEM((rows, LANES), F32),
                        pltpu.VMEM((rows, KV_LORA), F32)],
        compiler_params=_params("parallel", "arbitrary"),
        name="dsa_core",
    )(q_all.reshape(b, t, -1), widx.reshape(b, t, LANES), kidx.reshape(b, t, IDX_DIM),
      ckv.reshape(b, t, KV_LORA), qabs)


def dsa_layer(h2, b, t, norm_g, w_in, q_norm, kv_norm, kidx_norm, w_uq, w_uk, w_uv, w_out):
    u = rms_norm(h2, norm_g, BF16)
    width = Q_LORA + KV_LORA + IDX_DIM + LANES
    w_in_p = jnp.pad(w_in, ((0, 0), (0, width - w_in.shape[1]))).astype(BF16)
    proj = matmul(u, w_in_p, F32)
    qlat, ckv, kidx, widx = dsa_prep(proj, q_norm, kv_norm, kidx_norm)
    q_all = matmul(qlat, w_uq.astype(BF16), BF16)
    qabs = q_absorb(q_all, w_uk.astype(BF16))
    o_lat = dsa_core(q_all, widx, kidx, ckv, qabs, b, t)
    o = v_up(o_lat, w_uv.astype(BF16))
    return matmul(o, w_out.astype(BF16), F32, residual=h2)


def mlp(h2, norm_g, w_up, w_down):
    u = rms_norm(h2, norm_g, BF16)
    a = matmul(u, w_up.astype(BF16), BF16, relu2=True)
    return matmul(a, w_down.astype(BF16), F32, residual=h2)


def kernel(x, norm_mix, norm_mlp, norm_final, dn_w_in, dn_conv_w, dn_a_log, dn_dt_bias, dn_out_norm, dn_w_out, dsa_w_in, dsa_q_norm, dsa_kv_norm, dsa_kidx_norm, dsa_w_uq, dsa_w_uk, dsa_w_uv, dsa_w_out, mlp_w_up, mlp_w_down):
    b, t, d = x.shape
    h2 = x.reshape(b * t, d)
    for i in range(norm_mix.shape[0]):
        j = i // 2
        if i % 2 == 0:
            h2 = gated_deltanet_layer(h2, b, t, norm_mix[i], dn_w_in[j], dn_conv_w[j], dn_a_log[j],
                                      dn_dt_bias[j], dn_out_norm[j], dn_w_out[j])
        else:
            h2 = dsa_layer(h2, b, t, norm_mix[i], dsa_w_in[j], dsa_q_norm[j], dsa_kv_norm[j],
                           dsa_kidx_norm[j], dsa_w_uq[j], dsa_w_uk[j], dsa_w_uv[j], dsa_w_out[j])
        h2 = mlp(h2, norm_mlp[i], mlp_w_up[i], mlp_w_down[i])
    return rms_norm(h2, norm_final, x.dtype).reshape(b, t, d)
```

```python
import functools

import jax
import jax.numpy as jnp
from jax import lax
from jax.experimental import pallas as pl
from jax.experimental.pallas import tpu as pltpu

F32 = jnp.float32
BF16 = jnp.bfloat16
EPS = 1e-6
CHUNK = 64
QBLOCK = 128
DN_QK_HEADS = 16
DN_V_HEADS = 32
DN_HEAD_DIM = 128
DN_KEY_DIM = DN_QK_HEADS * DN_HEAD_DIM
DN_VAL_DIM = DN_V_HEADS * DN_HEAD_DIM
DN_CONV_DIM = 2 * DN_KEY_DIM + DN_VAL_DIM
DSA_HEADS = 16
DSA_HEAD_DIM = 128
Q_LORA = 512
KV_LORA = 256
IDX_HEADS = 16
IDX_DIM = 128
IDX_TOPK = 256
LANES = 128
VMEM_LIMIT_BYTES = 48 * 1024 * 1024
INT_MIN = -(2 ** 31)
MASKED = -1e30
M_INIT = -1e20


def _params(*sem):
    return pltpu.CompilerParams(dimension_semantics=sem, vmem_limit_bytes=VMEM_LIMIT_BYTES)


def _sigmoid(x):
    return 1.0 / (1.0 + jnp.exp(-x))


def _norm_body(x_ref, g_ref, o_ref):
    x = x_ref[...]
    y = x * lax.rsqrt(jnp.mean(x * x, axis=-1, keepdims=True) + EPS) * g_ref[...]
    o_ref[...] = y.astype(o_ref.dtype)


def rms_norm(x, g, out_dtype, tm=512):
    n, d = x.shape
    return pl.pallas_call(
        _norm_body,
        grid=(n // tm,),
        in_specs=[pl.BlockSpec((tm, d), lambda i: (i, 0)), pl.BlockSpec((1, d), lambda i: (0, 0))],
        out_specs=pl.BlockSpec((tm, d), lambda i: (i, 0)),
        out_shape=jax.ShapeDtypeStruct((n, d), out_dtype),
        compiler_params=_params("parallel"),
        name="rms_norm",
    )(x, g.reshape(1, d))


MM_SUB_COLS = 512


def _mm_body(a_ref, w_ref, *rest, nk, relu2, residual):
    if residual:
        r_ref, o_ref = rest[0], rest[1]
    else:
        r_ref, o_ref = None, rest[0]
    acc_ref = rest[-1] if nk > 1 else None

    def finish(acc, cols=slice(None)):
        if relu2:
            acc = jnp.square(jnp.maximum(acc, 0.0))
        if residual:
            acc = acc + r_ref[:, cols]
        o_ref[:, cols] = acc.astype(o_ref.dtype)

    sub = min(MM_SUB_COLS, o_ref.shape[1])
    blocks = [slice(c0, c0 + sub) for c0 in range(0, o_ref.shape[1], sub)]

    def k_step(first, last):
        for cols in blocks:
            part = jnp.dot(a_ref[...], w_ref[:, cols], preferred_element_type=F32)
            if not first:
                part = part + acc_ref[:, cols]
            if last:
                finish(part, cols)
            else:
                acc_ref[:, cols] = part

    if nk == 1:
        k_step(True, True)
        return
    k = pl.program_id(2)
    pl.when(k == 0)(functools.partial(k_step, True, False))
    pl.when((k > 0) & (k < nk - 1))(functools.partial(k_step, False, False))
    pl.when(k == nk - 1)(functools.partial(k_step, False, True))


def matmul(a, w, out_dtype, *, relu2=False, residual=None, tm=1024, tn=1024, tk=2048):
    m, kdim = a.shape
    n = w.shape[1]
    tm, tn, tk = min(tm, m), min(tn, n), min(tk, kdim)
    assert m % tm == 0 and n % tn == 0 and kdim % tk == 0
    nk = kdim // tk
    in_specs = [pl.BlockSpec((tm, tk), lambda i, j, k: (i, k)), pl.BlockSpec((tk, tn), lambda i, j, k: (k, j))]
    args = [a, w]
    if residual is not None:
        in_specs.append(pl.BlockSpec((tm, tn), lambda i, j, k: (i, j)))
        args.append(residual)
    return pl.pallas_call(
        functools.partial(_mm_body, nk=nk, relu2=relu2, residual=residual is not None),
        grid=(m // tm, n // tn, nk),
        in_specs=in_specs,
        out_specs=pl.BlockSpec((tm, tn), lambda i, j, k: (i, j)),
        out_shape=jax.ShapeDtypeStruct((m, n), out_dtype),
        scratch_shapes=[pltpu.VMEM((tm, tn), F32)] if nk > 1 else [],
        compiler_params=_params("parallel", "parallel", "arbitrary"),
        name="matmul",
    )(*args)


def _dn_gates_body(x_ref, alog_ref, dt_ref, o_ref):
    x = x_ref[...]
    g = -jnp.exp(alog_ref[...]) * (jnp.maximum(x + dt_ref[...], 0.0)
                                   + jnp.log(1.0 + jnp.exp(-jnp.abs(x + dt_ref[...]))))
    pos = lax.broadcasted_iota(jnp.int32, x.shape, 0) % CHUNK
    s = 1
    while s < CHUNK:
        g = g + jnp.where(pos >= s, pltpu.roll(g, s, axis=0), 0.0)
        s *= 2
    lane = lax.broadcasted_iota(jnp.int32, x.shape, 1)
    o_ref[...] = jnp.where(lane < DN_V_HEADS, _sigmoid(x), g)


def dn_gates(pba, alog_pad, dt_pad, tm=512):
    n = pba.shape[0]
    return pl.pallas_call(
        _dn_gates_body,
        grid=(n // tm,),
        in_specs=[pl.BlockSpec((tm, LANES), lambda i: (i, 0)),
                  pl.BlockSpec((1, LANES), lambda i: (0, 0)),
                  pl.BlockSpec((1, LANES), lambda i: (0, 0))],
        out_specs=pl.BlockSpec((tm, LANES), lambda i: (i, 0)),
        out_shape=jax.ShapeDtypeStruct((n, LANES), F32),
        compiler_params=_params("parallel"),
        name="dn_gates",
    )(pba, alog_pad, dt_pad)


def _conv_silu(ref, cw_ref, t0, tt):
    cw = cw_ref[...]
    acc = ref[0, t0:t0 + tt, :] * cw[3:4, :]
    for s in (1, 2, 3):
        if t0 == 0:
            x = ref[0, 0:tt, :]
            rows = lax.broadcasted_iota(jnp.int32, x.shape, 0)
            xs = jnp.where(rows >= s, pltpu.roll(x, s, axis=0), 0.0)
        else:
            xs = ref[0, t0 - s:t0 - s + tt, :]
        acc = acc + xs * cw[3 - s:4 - s, :]
    return acc * _sigmoid(acc)


def _l2norm(t):
    return t * lax.rsqrt(jnp.sum(t * t, axis=-1, keepdims=True) + EPS)


def _bmm(a, b):
    return jnp.einsum("nij,njk->nik", a.astype(BF16), b.astype(BF16), preferred_element_type=F32)


def _bmm_nt(a, b):
    return jnp.einsum("nid,njd->nij", a.astype(BF16), b.astype(BF16), preferred_element_type=F32)


DN_GROUP = 16


def _dn_prep_body(q_ref, k_ref, v_ref, cwq_ref, cwk_ref, cwv_ref, gates_ref, gcr_ref,
                  u_ref, w_ref, qd_ref, kd_ref, a_ref, qs_ref, ks_ref, vs_ref, bb_ref, gb_ref, *, t_len):
    j = pl.program_id(1)
    tt = 256
    hd = DN_HEAD_DIM
    for ti in range(t_len // tt):
        t0 = ti * tt
        qs_ref[t0:t0 + tt, :] = _l2norm(_conv_silu(q_ref, cwq_ref, t0, tt)) * (hd ** -0.5)
        ks_ref[t0:t0 + tt, :] = _l2norm(_conv_silu(k_ref, cwk_ref, t0, tt))
        vs_ref[t0:t0 + tt, :] = _conv_silu(v_ref, cwv_ref, t0, tt)
        gt = gates_ref[0, t0:t0 + tt, :]
        lane = lax.broadcasted_iota(jnp.int32, gt.shape, 1)
        for hh in range(2):
            col = 2 * j + hh
            bsel = jnp.sum(jnp.where(lane == col, gt, 0.0), axis=1, keepdims=True)
            gsel = jnp.sum(jnp.where(lane == col + DN_V_HEADS, gt, 0.0), axis=1, keepdims=True)
            bb_ref[hh, t0:t0 + tt, :] = jnp.broadcast_to(bsel, (tt, LANES))
            gb_ref[hh, t0:t0 + tt, :] = jnp.broadcast_to(gsel, (tt, LANES))

    c = CHUNK
    g = min(DN_GROUP, t_len // c)
    rows = g * c
    ri = lax.broadcasted_iota(jnp.int32, (c, c), 0)
    ci = lax.broadcasted_iota(jnp.int32, (c, c), 1)
    lower = ri >= ci
    strict = ri > ci
    eye = jnp.where(ri == ci, 1.0, 0.0)
    pair_mask = strict & ((ri // 2) == (ci // 2))
    merge_masks = [((ri // (2 * s)) == (ci // (2 * s))) & ((ri // s) != (ci // s)) & strict
                   for s in (2, 4, 8, 16, 32)]

    def group_step(gi, carry):
        base = pl.multiple_of(gi * rows, rows)
        kc = ks_ref[pl.ds(base, rows), :].reshape(g, c, hd)
        qc = qs_ref[pl.ds(base, rows), :].reshape(g, c, hd)
        kk1 = _bmm_nt(kc, kc)
        qk1 = _bmm_nt(qc, kc)
        two = lambda x: jnp.concatenate([x, x], axis=0)
        kk, qk, kc2, qc2 = two(kk1), two(qk1), two(kc), two(qc)
        beta_b = jnp.concatenate([bb_ref[hh, pl.ds(base, rows), :].reshape(g, c, LANES) for hh in range(2)], axis=0)
        gcc_b = jnp.concatenate([gb_ref[hh, pl.ds(base, rows), :].reshape(g, c, LANES) for hh in range(2)], axis=0)
        grow = gcr_ref[0, 0, :, pl.ds(base, rows)]
        gcr = jnp.stack([grow[hh:hh + 1, n * c:(n + 1) * c] for hh in range(2) for n in range(g)], axis=0)
        vc = jnp.concatenate([vs_ref[pl.ds(base, rows), hh * hd:(hh + 1) * hd].reshape(g, c, hd)
                              for hh in range(2)], axis=0)
        gl_b = gcc_b[:, c - 1:c, :]
        dm = gcc_b[:, :, :c] - gcr
        decay = jnp.where(lower, jnp.exp(jnp.where(lower, dm, 0.0)), 0.0)
        lm = jnp.where(strict, kk * beta_b[:, :, :c] * decay, 0.0)
        tm_ = eye - jnp.where(pair_mask, lm, 0.0)
        for mk in merge_masks:
            tm_ = tm_ - _bmm(_bmm(tm_, jnp.where(mk, lm, 0.0)), tm_)
        eg = jnp.exp(gcc_b)
        uw = _bmm(tm_, jnp.concatenate([vc * beta_b, kc2 * (beta_b * eg)], axis=2))
        qd = qc2 * eg
        kd = kc2 * jnp.exp(gl_b - gcc_b)
        am = qk * decay
        for hh in range(2):
            cols = slice(hh * hd, (hh + 1) * hd)
            inst = slice(hh * g, (hh + 1) * g)
            u_ref[0, pl.ds(base, rows), cols] = uw[inst, :, :hd].reshape(rows, hd).astype(u_ref.dtype)
            w_ref[0, pl.ds(base, rows), cols] = uw[inst, :, hd:].reshape(rows, hd).astype(w_ref.dtype)
            qd_ref[0, pl.ds(base, rows), cols] = qd[inst].reshape(rows, hd).astype(qd_ref.dtype)
            kd_ref[0, pl.ds(base, rows), cols] = kd[inst].reshape(rows, hd).astype(kd_ref.dtype)
            a_ref[0, pl.ds(base, rows), hh * c:(hh + 1) * c] = am[inst].reshape(rows, c).astype(a_ref.dtype)
        return carry

    lax.fori_loop(0, t_len // rows, group_step, 0)


def dn_prep(proj, conv_w, gates, gcr):
    b, t, _ = proj.shape
    hd = DN_HEAD_DIM
    nq = DN_QK_HEADS
    v_blk0 = 2 * DN_KEY_DIM // (2 * hd)
    wide = pl.BlockSpec((1, t, 2 * hd), lambda bi, j: (bi, 0, j))
    big = jax.ShapeDtypeStruct((b, t, DN_VAL_DIM), BF16)
    return pl.pallas_call(
        functools.partial(_dn_prep_body, t_len=t),
        grid=(b, nq),
        in_specs=[
            pl.BlockSpec((1, t, hd), lambda bi, j: (bi, 0, j)),
            pl.BlockSpec((1, t, hd), lambda bi, j: (bi, 0, nq + j)),
            pl.BlockSpec((1, t, 2 * hd), lambda bi, j: (bi, 0, v_blk0 + j)),
            pl.BlockSpec((4, hd), lambda bi, j: (0, j)),
            pl.BlockSpec((4, hd), lambda bi, j: (0, nq + j)),
            pl.BlockSpec((4, 2 * hd), lambda bi, j: (0, v_blk0 + j)),
            pl.BlockSpec((1, t, LANES), lambda bi, j: (bi, 0, 0)),
            pl.BlockSpec((1, 1, 2, t), lambda bi, j: (bi, j, 0, 0)),
        ],
        out_specs=[wide, wide, wide, wide, pl.BlockSpec((1, t, 2 * CHUNK), lambda bi, j: (bi, 0, j))],
        out_shape=[big, big, big, big, jax.ShapeDtypeStruct((b, t, DN_V_HEADS * CHUNK), BF16)],
        scratch_shapes=[
            pltpu.VMEM((t, hd), F32), pltpu.VMEM((t, hd), F32), pltpu.VMEM((t, 2 * hd), F32),
            pltpu.VMEM((2, t, LANES), F32), pltpu.VMEM((2, t, LANES), F32),
        ],
        compiler_params=_params("parallel", "arbitrary"),
        name="dn_prep",
    )(proj, proj, proj, conv_w, conv_w, conv_w, gates, gcr)


DN_REC_HEADS = 16
DN_REC_ROWS = 512


def _dn_rec_body(u_ref, w_ref, qd_ref, kd_ref, a_ref, z_ref, gl_ref, onorm_ref, o_ref, s_ref, *, t_len):
    hd = DN_HEAD_DIM
    c = CHUNK

    @pl.when(pl.program_id(2) == 0)
    def _():
        s_ref[...] = jnp.zeros_like(s_ref)

    onorm = onorm_ref[...]

    def step(ci, carry):
        r0 = pl.multiple_of(ci * c, c)
        egl = jnp.exp(gl_ref[0, 0, ci])
        heads = range(DN_REC_HEADS)
        cols = [slice(h * hd, (h + 1) * hd) for h in heads]
        st = [s_ref[h] for h in heads]
        r = [jnp.dot(jnp.concatenate([w_ref[0, pl.ds(r0, c), cols[h]], qd_ref[0, pl.ds(r0, c), cols[h]]], axis=0),
                     st[h].astype(BF16), preferred_element_type=F32) for h in heads]
        vb = [(u_ref[0, pl.ds(r0, c), cols[h]].astype(F32) - r[h][:c]).astype(BF16) for h in heads]
        upd = [lax.dot_general(kd_ref[0, pl.ds(r0, c), cols[h]], vb[h], (((0,), (0,)), ((), ())),
                               preferred_element_type=F32) for h in heads]
        o = [r[h][c:] + jnp.dot(a_ref[0, pl.ds(r0, c), h * c:(h + 1) * c], vb[h], preferred_element_type=F32)
             for h in heads]
        for h in heads:
            s_ref[h] = st[h] * egl[:, h:h + 1] + upd[h]
        for h in heads:
            y = o[h] * lax.rsqrt(jnp.mean(o[h] * o[h], axis=-1, keepdims=True) + EPS) * onorm
            zz = z_ref[0, pl.ds(r0, c), cols[h]]
            o_ref[0, pl.ds(r0, c), cols[h]] = (y * (zz * _sigmoid(zz))).astype(o_ref.dtype)
        return carry

    lax.fori_loop(0, t_len // c, step, 0)


def dn_rec(u, w, qd, kd, a, proj, gl, out_norm):
    b, t, _ = u.shape
    hd = DN_HEAD_DIM
    hb = DN_REC_HEADS
    ng = DN_V_HEADS // hb
    tt = min(DN_REC_ROWS, t)
    z_blk0 = DN_CONV_DIM // (hb * hd)
    wide = pl.BlockSpec((1, tt, hb * hd), lambda bi, g, ti: (bi, ti, g))
    return pl.pallas_call(
        functools.partial(_dn_rec_body, t_len=tt),
        grid=(b, ng, t // tt),
        in_specs=[
            wide, wide, wide, wide,
            pl.BlockSpec((1, tt, hb * CHUNK), lambda bi, g, ti: (bi, ti, g)),
            pl.BlockSpec((1, tt, hb * hd), lambda bi, g, ti: (bi, ti, z_blk0 + g)),
            pl.BlockSpec((1, 1, tt // CHUNK, 1, hb), lambda bi, g, ti: (bi, g, ti, 0, 0)),
            pl.BlockSpec((1, hd), lambda bi, g, ti: (0, 0)),
        ],
        out_specs=wide,
        out_shape=jax.ShapeDtypeStruct((b, t, DN_VAL_DIM), BF16),
        scratch_shapes=[pltpu.VMEM((hb, hd, hd), F32)],
        compiler_params=_params("parallel", "parallel", "arbitrary"),
        name="dn_rec",
    )(u, w, qd, kd, a, proj, gl, out_norm.reshape(1, hd))


def gated_deltanet_layer(h2, b, t, norm_g, w_in, conv_w, a_log, dt_bias, out_norm, w_out):
    n = b * t
    u = rms_norm(h2, norm_g, BF16)
    n_qkvz = DN_CONV_DIM + DN_VAL_DIM
    proj = matmul(u, w_in[:, :n_qkvz].astype(BF16), F32, tn=2048).reshape(b, t, n_qkvz)
    w_ba = jnp.pad(w_in[:, n_qkvz:], ((0, 0), (0, LANES - 2 * DN_V_HEADS))).astype(BF16)
    pba = matmul(u, w_ba, F32)
    pad = lambda p: jnp.pad(p.astype(F32), (DN_V_HEADS, LANES - 2 * DN_V_HEADS)).reshape(1, LANES)
    gates = dn_gates(pba, pad(a_log), pad(dt_bias))
    gc = gates[:, DN_V_HEADS:2 * DN_V_HEADS].reshape(b, t, DN_V_HEADS)
    gcr = gc.reshape(b, t, DN_QK_HEADS, 2).transpose(0, 2, 3, 1)
    ng = DN_V_HEADS // DN_REC_HEADS
    gl = gc[:, CHUNK - 1::CHUNK, :].reshape(b, t // CHUNK, ng, 1, DN_REC_HEADS).transpose(0, 2, 1, 3, 4)
    uu, ww, qd, kd, am = dn_prep(proj, conv_w, gates.reshape(b, t, LANES), gcr)
    o = dn_rec(uu, ww, qd, kd, am, proj, gl, out_norm)
    return matmul(o.reshape(n, DN_VAL_DIM), w_out.astype(BF16), F32, residual=h2)


def _dsa_prep_body(p_ref, qn_ref, kvn_ref, kin_ref, q_out, kv_out, ki_out, w_out):
    def nrm(x, g):
        return x * lax.rsqrt(jnp.mean(x * x, axis=-1, keepdims=True) + EPS) * g

    a, b2, c2 = Q_LORA, Q_LORA + KV_LORA, Q_LORA + KV_LORA + IDX_DIM
    q_out[...] = nrm(p_ref[:, 0:a], qn_ref[...]).astype(q_out.dtype)
    kv_out[...] = nrm(p_ref[:, a:b2], kvn_ref[...]).astype(kv_out.dtype)
    ki_out[...] = nrm(p_ref[:, b2:c2], kin_ref[...]).astype(ki_out.dtype)
    w_out[...] = p_ref[:, c2:c2 + LANES] * (IDX_HEADS ** -0.5 * IDX_DIM ** -0.5)


def dsa_prep(proj, q_norm, kv_norm, kidx_norm, tm=512):
    n, width = proj.shape
    row = lambda i: (i, 0)
    fix = lambda i: (0, 0)
    return pl.pallas_call(
        _dsa_prep_body,
        grid=(n // tm,),
        in_specs=[pl.BlockSpec((tm, width), row), pl.BlockSpec((1, Q_LORA), fix),
                  pl.BlockSpec((1, KV_LORA), fix), pl.BlockSpec((1, IDX_DIM), fix)],
        out_specs=[pl.BlockSpec((tm, Q_LORA), row), pl.BlockSpec((tm, KV_LORA), row),
                   pl.BlockSpec((tm, IDX_DIM), row), pl.BlockSpec((tm, LANES), row)],
        out_shape=[jax.ShapeDtypeStruct((n, Q_LORA), BF16), jax.ShapeDtypeStruct((n, KV_LORA), BF16),
                   jax.ShapeDtypeStruct((n, IDX_DIM), BF16), jax.ShapeDtypeStruct((n, LANES), F32)],
        compiler_params=_params("parallel"),
        name="dsa_prep",
    )(proj, q_norm.reshape(1, -1), kv_norm.reshape(1, -1), kidx_norm.reshape(1, -1))


def _qabs_body(q_ref, w_ref, o_ref, *, r):
    for h in range(DSA_HEADS):
        res = jnp.dot(q_ref[:, h * DSA_HEAD_DIM:(h + 1) * DSA_HEAD_DIM], w_ref[h],
                      preferred_element_type=F32) * (DSA_HEAD_DIM ** -0.5)
        for rr in range(r):
            o_ref[rr, h] = res[rr * QBLOCK:(rr + 1) * QBLOCK].astype(o_ref.dtype)


def q_absorb(q_all, w_uk, tm=256):
    n = q_all.shape[0]
    r = tm // QBLOCK
    hw = DSA_HEADS * DSA_HEAD_DIM
    return pl.pallas_call(
        functools.partial(_qabs_body, r=r),
        grid=(n // tm,),
        in_specs=[pl.BlockSpec((tm, hw), lambda i: (i, 0)),
                  pl.BlockSpec((DSA_HEADS, DSA_HEAD_DIM, KV_LORA), lambda i: (0, 0, 0))],
        out_specs=pl.BlockSpec((r, DSA_HEADS, QBLOCK, KV_LORA), lambda i: (i, 0, 0, 0)),
        out_shape=jax.ShapeDtypeStruct((n // QBLOCK, DSA_HEADS, QBLOCK, KV_LORA), BF16),
        compiler_params=_params("parallel"),
        name="q_absorb",
    )(q_all, w_uk)


def _vup_body(o_ref, w_ref, out_ref, *, r):
    for h in range(DSA_HEADS):
        for rr in range(r):
            out_ref[rr * QBLOCK:(rr + 1) * QBLOCK, h * DSA_HEAD_DIM:(h + 1) * DSA_HEAD_DIM] = jnp.dot(
                o_ref[rr, h], w_ref[h], preferred_element_type=F32).astype(out_ref.dtype)


def v_up(o_lat, w_uv, tm=256):
    nb = o_lat.shape[0]
    r = tm // QBLOCK
    hw = DSA_HEADS * DSA_HEAD_DIM
    return pl.pallas_call(
        functools.partial(_vup_body, r=r),
        grid=(nb // r,),
        in_specs=[pl.BlockSpec((r, DSA_HEADS, QBLOCK, KV_LORA), lambda i: (i, 0, 0, 0)),
                  pl.BlockSpec((DSA_HEADS, KV_LORA, DSA_HEAD_DIM), lambda i: (0, 0, 0))],
        out_specs=pl.BlockSpec((tm, hw), lambda i: (i, 0)),
        out_shape=jax.ShapeDtypeStruct((nb * QBLOCK, hw), BF16),
        compiler_params=_params("parallel"),
        name="v_up",
    )(o_lat, w_uv)


def _dsa_core_body(qidx_ref, widx_ref, kidx_ref, ckv_ref, qabs_ref, o_ref,
                   keys_ref, tau_ref, bias_ref, s_ref, p_ref, m_ref, l_ref, acc_ref, *, top_k, kt):
    i = pl.program_id(1)
    nq = QBLOCK
    nkt = keys_ref.shape[0]
    n_act = ((i + 1) * nq + kt - 1) // kt
    wt = widx_ref[0].T
    krow = lax.broadcasted_iota(jnp.int32, (kt, nq), 0)
    qcol = lax.broadcasted_iota(jnp.int32, (kt, nq), 1)
    q_chunk = (i * nq + qcol) // CHUNK
    tn = (((1,), (1,)), ((), ()))
    sub = 8

    def index_tile(jt, carry):
        k0 = pl.multiple_of(jt * kt, kt)
        kblk = kidx_ref[0, pl.ds(k0, kt), :]
        sc = jnp.zeros((kt, nq), F32)
        for h in range(IDX_HEADS):
            d = lax.dot_general(kblk, qidx_ref[0, :, h * IDX_DIM:(h + 1) * IDX_DIM], tn,
                                preferred_element_type=F32)
            sc = sc + jnp.maximum(d, 0.0) * wt[h:h + 1, :]
        bits = pltpu.bitcast(sc, jnp.int32)
        key = bits ^ ((bits >> 31) & 0x7FFFFFFF)
        adm = ((k0 + krow) // CHUNK) <= q_chunk
        keys_ref[jt] = jnp.where(adm, key, INT_MIN)
        return carry

    lax.fori_loop(0, n_act, index_tile, 0)

    for jt in range(nkt):
        @pl.when(jt >= n_act)
        def _():
            keys_ref[jt] = jnp.full((kt, nq), INT_MIN, jnp.int32)

    def bisect_over(ntiles):
        def bisect(it, prefix):
            cand = prefix | lax.shift_left(jnp.int32(1), 31 - it)
            cand_s = cand ^ INT_MIN
            cnt = jnp.zeros((sub, nq), F32)
            for jt in range(ntiles):
                hit = jnp.where(keys_ref[jt] >= cand_s, 1.0, 0.0)
                cnt = cnt + jnp.sum(hit.reshape(kt // sub, sub, nq), axis=0)
            return jnp.where(jnp.sum(cnt, axis=0, keepdims=True) >= top_k, cand, prefix)

        prefix = lax.fori_loop(0, 32, bisect, jnp.zeros((1, nq), jnp.int32))
        tau_ref[...] = jnp.broadcast_to(jnp.maximum(prefix ^ INT_MIN, INT_MIN + 1), tau_ref.shape)

    step = 2 if nkt % 2 == 0 else 1
    for ntiles in range(step, nkt + 1, step):
        @pl.when((n_act > ntiles - step) & (n_act <= ntiles))
        def _():
            bisect_over(ntiles)

    tau = tau_ref[0:1, :]

    rows = DSA_HEADS * nq
    rep = kt // LANES
    m_ref[...] = jnp.full(m_ref.shape, M_INIT, F32)
    l_ref[...] = jnp.zeros_like(l_ref)
    acc_ref[...] = jnp.zeros_like(acc_ref)

    def attend_tile(jt, carry):
        k0 = pl.multiple_of(jt * kt, kt)
        ck = ckv_ref[0, pl.ds(k0, kt), :]
        bias_ref[...] = jnp.where(keys_ref[jt] >= tau, 0.0, MASKED).T
        s_ref[...] = lax.dot_general(qabs_ref[0].reshape(rows, KV_LORA), ck, tn, preferred_element_type=F32)

        for h in range(DSA_HEADS):
            hr = slice(h * nq, (h + 1) * nq)
            s = s_ref[hr, :] + bias_ref[...]
            m_old = m_ref[hr, :]
            m_new = jnp.maximum(m_old, jnp.max(s, axis=1, keepdims=True))
            alpha = jnp.exp(m_old - m_new)
            p = jnp.exp(s - jnp.concatenate([m_new] * rep, axis=1))
            l_ref[hr, :] = alpha * l_ref[hr, :] + jnp.sum(p, axis=1, keepdims=True)
            m_ref[hr, :] = m_new
            acc_ref[hr, :] = acc_ref[hr, :] * jnp.concatenate([alpha] * (KV_LORA // LANES), axis=1)
            p_ref[hr, :] = p.astype(BF16)
        acc_ref[...] += jnp.dot(p_ref[...], ck, preferred_element_type=F32)
        return carry

    lax.fori_loop(0, n_act, attend_tile, 0)
    inv_l = 1.0 / l_ref[...]
    out = acc_ref[...] * jnp.concatenate([inv_l] * (KV_LORA // LANES), axis=1)
    o_ref[0] = out.reshape(DSA_HEADS, nq, KV_LORA).astype(o_ref.dtype)


def dsa_core(q_all, widx, kidx, ckv, qabs, b, t, kt=256):
    nblk = t // QBLOCK
    top_k = min(IDX_TOPK, t // 4)
    hw = IDX_HEADS * IDX_DIM
    rows = DSA_HEADS * QBLOCK
    return pl.pallas_call(
        functools.partial(_dsa_core_body, top_k=top_k, kt=kt),
        grid=(b, nblk),
        in_specs=[
            pl.BlockSpec((1, QBLOCK, hw), lambda bi, i: (bi, i, 1)),
            pl.BlockSpec((1, QBLOCK, LANES), lambda bi, i: (bi, i, 0)),
            pl.BlockSpec((1, t, IDX_DIM), lambda bi, i: (bi, 0, 0)),
            pl.BlockSpec((1, t, KV_LORA), lambda bi, i: (bi, 0, 0)),
            pl.BlockSpec((1, DSA_HEADS, QBLOCK, KV_LORA), lambda bi, i: (bi * nblk + i, 0, 0, 0)),
        ],
        out_specs=pl.BlockSpec((1, DSA_HEADS, QBLOCK, KV_LORA), lambda bi, i: (bi * nblk + i, 0, 0, 0)),
        out_shape=jax.ShapeDtypeStruct((b * nblk, DSA_HEADS, QBLOCK, KV_LORA), BF16),
        scratch_shapes=[pltpu.VMEM((t // kt, kt, QBLOCK), jnp.int32), pltpu.VMEM((8, QBLOCK), jnp.int32),
                        pltpu.VMEM((QBLOCK, kt), F32),
                        pltpu.VMEM((rows, kt), F32), pltpu.VMEM((rows, kt), BF16),
                        pltpu.VMEM((rows, LANES), F32), pltpu.VMEM((rows, LANES), F32),
                        pltpu.VMEM((rows, KV_LORA), F32)],
        compiler_params=_params("parallel", "arbitrary"),
        name="dsa_core",
    )(q_all.reshape(b, t, -1), widx.reshape(b, t, LANES), kidx.reshape(b, t, IDX_DIM),
      ckv.reshape(b, t, KV_LORA), qabs)


def dsa_layer(h2, b, t, norm_g, w_in, q_norm, kv_norm, kidx_norm, w_uq, w_uk, w_uv, w_out):
    u = rms_norm(h2, norm_g, BF16)
    width = Q_LORA + KV_LORA + IDX_DIM + LANES
    w_in_p = jnp.pad(w_in, ((0, 0), (0, width - w_in.shape[1]))).astype(BF16)
    proj = matmul(u, w_in_p, F32)
    qlat, ckv, kidx, widx = dsa_prep(proj, q_norm, kv_norm, kidx_norm)
    q_all = matmul(qlat, w_uq.astype(BF16), BF16)
    qabs = q_absorb(q_all, w_uk.astype(BF16))
    o_lat = dsa_core(q_all, widx, kidx, ckv, qabs, b, t)
    o = v_up(o_lat, w_uv.astype(BF16))
    return matmul(o, w_out.astype(BF16), F32, residual=h2)


def mlp(h2, norm_g, w_up, w_down):
    u = rms_norm(h2, norm_g, BF16)
    a = matmul(u, w_up.astype(BF16), BF16, relu2=True, tn=2048)
    return matmul(a, w_down.astype(BF16), F32, residual=h2)


def kernel(x, norm_mix, norm_mlp, norm_final, dn_w_in, dn_conv_w, dn_a_log, dn_dt_bias, dn_out_norm, dn_w_out, dsa_w_in, dsa_q_norm, dsa_kv_norm, dsa_kidx_norm, dsa_w_uq, dsa_w_uk, dsa_w_uv, dsa_w_out, mlp_w_up, mlp_w_down):
    b, t, d = x.shape
    h2 = x.reshape(b * t, d)
    for i in range(norm_mix.shape[0]):
        j = i // 2
        if i % 2 == 0:
            h2 = gated_deltanet_layer(h2, b, t, norm_mix[i], dn_w_in[j], dn_conv_w[j], dn_a_log[j],
                                      dn_dt_bias[j], dn_out_norm[j], dn_w_out[j])
        else:
            h2 = dsa_layer(h2, b, t, norm_mix[i], dsa_w_in[j], dsa_q_norm[j], dsa_kv_norm[j],
                           dsa_kidx_norm[j], dsa_w_uq[j], dsa_w_uk[j], dsa_w_uv[j], dsa_w_out[j])
        h2 = mlp(h2, norm_mlp[i], mlp_w_up[i], mlp_w_down[i])
    return rms_norm(h2, norm_final, x.dtype).reshape(b, t, d)
```

```python
import functools

import jax
import jax.numpy as jnp
from jax import lax
from jax.experimental import pallas as pl
from jax.experimental.pallas import tpu as pltpu

F32 = jnp.float32
BF16 = jnp.bfloat16
EPS = 1e-6
CHUNK = 64
QBLOCK = 128
DN_QK_HEADS = 16
DN_V_HEADS = 32
DN_HEAD_DIM = 128
DN_KEY_DIM = DN_QK_HEADS * DN_HEAD_DIM
DN_VAL_DIM = DN_V_HEADS * DN_HEAD_DIM
DN_CONV_DIM = 2 * DN_KEY_DIM + DN_VAL_DIM
DSA_HEADS = 16
DSA_HEAD_DIM = 128
Q_LORA = 512
KV_LORA = 256
IDX_HEADS = 16
IDX_DIM = 128
IDX_TOPK = 256
LANES = 128
VMEM_LIMIT_BYTES = 56 * 1024 * 1024
INT_MIN = -(2 ** 31)
MASKED = -1e30
M_INIT = -1e20


def _params(*sem):
    return pltpu.CompilerParams(dimension_semantics=sem, vmem_limit_bytes=VMEM_LIMIT_BYTES)


def _sigmoid(x):
    return 1.0 / (1.0 + jnp.exp(-x))


def _norm_body(x_ref, g_ref, o_ref):
    x = x_ref[...]
    y = x * lax.rsqrt(jnp.mean(x * x, axis=-1, keepdims=True) + EPS) * g_ref[...]
    o_ref[...] = y.astype(o_ref.dtype)


def rms_norm(x, g, out_dtype, tm=512):
    n, d = x.shape
    return pl.pallas_call(
        _norm_body,
        grid=(n // tm,),
        in_specs=[pl.BlockSpec((tm, d), lambda i: (i, 0)), pl.BlockSpec((1, d), lambda i: (0, 0))],
        out_specs=pl.BlockSpec((tm, d), lambda i: (i, 0)),
        out_shape=jax.ShapeDtypeStruct((n, d), out_dtype),
        compiler_params=_params("parallel"),
        name="rms_norm",
    )(x, g.reshape(1, d))


MM_SUB_COLS = 512


def _mm_body(a_ref, w_ref, *rest, nk, relu2, residual):
    if residual:
        r_ref, o_ref = rest[0], rest[1]
    else:
        r_ref, o_ref = None, rest[0]
    acc_ref = rest[-1] if nk > 1 else None

    def finish(acc, cols=slice(None)):
        if relu2:
            acc = jnp.square(jnp.maximum(acc, 0.0))
        if residual:
            acc = acc + r_ref[:, cols]
        o_ref[:, cols] = acc.astype(o_ref.dtype)

    sub = min(MM_SUB_COLS, o_ref.shape[1])
    blocks = [slice(c0, c0 + sub) for c0 in range(0, o_ref.shape[1], sub)]

    def k_step(first, last):
        for cols in blocks:
            part = jnp.dot(a_ref[...], w_ref[:, cols], preferred_element_type=F32)
            if not first:
                part = part + acc_ref[:, cols]
            if last:
                finish(part, cols)
            else:
                acc_ref[:, cols] = part

    if nk == 1:
        k_step(True, True)
        return
    k = pl.program_id(2)
    pl.when(k == 0)(functools.partial(k_step, True, False))
    pl.when((k > 0) & (k < nk - 1))(functools.partial(k_step, False, False))
    pl.when(k == nk - 1)(functools.partial(k_step, False, True))


def matmul(a, w, out_dtype, *, relu2=False, residual=None, tm=1024, tn=1024, tk=2048):
    m, kdim = a.shape
    n = w.shape[1]
    tm, tn, tk = min(tm, m), min(tn, n), min(tk, kdim)
    assert m % tm == 0 and n % tn == 0 and kdim % tk == 0
    nk = kdim // tk
    in_specs = [pl.BlockSpec((tm, tk), lambda i, j, k: (i, k)), pl.BlockSpec((tk, tn), lambda i, j, k: (k, j))]
    args = [a, w]
    if residual is not None:
        in_specs.append(pl.BlockSpec((tm, tn), lambda i, j, k: (i, j)))
        args.append(residual)
    return pl.pallas_call(
        functools.partial(_mm_body, nk=nk, relu2=relu2, residual=residual is not None),
        grid=(m // tm, n // tn, nk),
        in_specs=in_specs,
        out_specs=pl.BlockSpec((tm, tn), lambda i, j, k: (i, j)),
        out_shape=jax.ShapeDtypeStruct((m, n), out_dtype),
        scratch_shapes=[pltpu.VMEM((tm, tn), F32)] if nk > 1 else [],
        compiler_params=_params("parallel", "parallel", "arbitrary"),
        name="matmul",
    )(*args)


def _mm_res_norm_body(a_ref, w_ref, r_ref, g_ref, h_ref, u_ref, *scratch, nk):
    acc_ref = scratch[0] if nk > 1 else None
    n = h_ref.shape[1]
    sub = min(MM_SUB_COLS, n)
    blocks = [slice(c0, c0 + sub) for c0 in range(0, n, sub)]

    def k_step(first, last):
        for cols in blocks:
            part = jnp.dot(a_ref[...], w_ref[:, cols], preferred_element_type=F32)
            if not first:
                part = part + acc_ref[:, cols]
            if last:
                h_ref[:, cols] = part + r_ref[:, cols]
            else:
                acc_ref[:, cols] = part
        if last:
            h = h_ref[...]
            scale = lax.rsqrt(jnp.mean(h * h, axis=-1, keepdims=True) + EPS)
            u_ref[...] = (h * scale * g_ref[...]).astype(u_ref.dtype)

    if nk == 1:
        k_step(True, True)
        return
    k = pl.program_id(1)
    pl.when(k == 0)(functools.partial(k_step, True, False))
    pl.when((k > 0) & (k < nk - 1))(functools.partial(k_step, False, False))
    pl.when(k == nk - 1)(functools.partial(k_step, False, True))


def matmul_res_norm(a, w, residual, gain, u_dtype, *, tm=512, tk=2048):
    m, kdim = a.shape
    n = w.shape[1]
    tm, tk = min(tm, m), min(tk, kdim)
    assert m % tm == 0 and kdim % tk == 0
    nk = kdim // tk
    row = pl.BlockSpec((tm, n), lambda i, k: (i, 0))
    return pl.pallas_call(
        functools.partial(_mm_res_norm_body, nk=nk),
        grid=(m // tm, nk),
        in_specs=[pl.BlockSpec((tm, tk), lambda i, k: (i, k)), pl.BlockSpec((tk, n), lambda i, k: (k, 0)),
                  row, pl.BlockSpec((1, n), lambda i, k: (0, 0))],
        out_specs=[row, row],
        out_shape=[jax.ShapeDtypeStruct((m, n), F32), jax.ShapeDtypeStruct((m, n), u_dtype)],
        scratch_shapes=[pltpu.VMEM((tm, n), F32)] if nk > 1 else [],
        compiler_params=_params("parallel", "arbitrary"),
        name="matmul_res_norm",
    )(a, w, residual, gain.reshape(1, n))


def _dn_gates_body(x_ref, alog_ref, dt_ref, o_ref):
    x = x_ref[...]
    g = -jnp.exp(alog_ref[...]) * (jnp.maximum(x + dt_ref[...], 0.0)
                                   + jnp.log(1.0 + jnp.exp(-jnp.abs(x + dt_ref[...]))))
    pos = lax.broadcasted_iota(jnp.int32, x.shape, 0) % CHUNK
    s = 1
    while s < CHUNK:
        g = g + jnp.where(pos >= s, pltpu.roll(g, s, axis=0), 0.0)
        s *= 2
    lane = lax.broadcasted_iota(jnp.int32, x.shape, 1)
    o_ref[...] = jnp.where(lane < DN_V_HEADS, _sigmoid(x), g)


def dn_gates(pba, alog_pad, dt_pad, tm=512):
    n = pba.shape[0]
    return pl.pallas_call(
        _dn_gates_body,
        grid=(n // tm,),
        in_specs=[pl.BlockSpec((tm, LANES), lambda i: (i, 0)),
                  pl.BlockSpec((1, LANES), lambda i: (0, 0)),
                  pl.BlockSpec((1, LANES), lambda i: (0, 0))],
        out_specs=pl.BlockSpec((tm, LANES), lambda i: (i, 0)),
        out_shape=jax.ShapeDtypeStruct((n, LANES), F32),
        compiler_params=_params("parallel"),
        name="dn_gates",
    )(pba, alog_pad, dt_pad)


def _conv_silu(ref, cw_ref, t0, tt):
    cw = cw_ref[...]
    acc = ref[0, t0:t0 + tt, :] * cw[3:4, :]
    for s in (1, 2, 3):
        if t0 == 0:
            x = ref[0, 0:tt, :]
            rows = lax.broadcasted_iota(jnp.int32, x.shape, 0)
            xs = jnp.where(rows >= s, pltpu.roll(x, s, axis=0), 0.0)
        else:
            xs = ref[0, t0 - s:t0 - s + tt, :]
        acc = acc + xs * cw[3 - s:4 - s, :]
    return acc * _sigmoid(acc)


def _l2norm(t):
    return t * lax.rsqrt(jnp.sum(t * t, axis=-1, keepdims=True) + EPS)


def _bmm(a, b):
    return jnp.einsum("nij,njk->nik", a.astype(BF16), b.astype(BF16), preferred_element_type=F32)


def _bmm_nt(a, b):
    return jnp.einsum("nid,njd->nij", a.astype(BF16), b.astype(BF16), preferred_element_type=F32)


DN_GROUP = 16


def _dn_prep_body(q_ref, k_ref, v_ref, cwq_ref, cwk_ref, cwv_ref, gates_ref, gcr_ref,
                  u_ref, w_ref, qd_ref, kd_ref, a_ref, qs_ref, ks_ref, vs_ref, bb_ref, gb_ref, *, t_len):
    j = pl.program_id(1)
    tt = 256
    hd = DN_HEAD_DIM
    for ti in range(t_len // tt):
        t0 = ti * tt
        qs_ref[t0:t0 + tt, :] = _l2norm(_conv_silu(q_ref, cwq_ref, t0, tt)) * (hd ** -0.5)
        ks_ref[t0:t0 + tt, :] = _l2norm(_conv_silu(k_ref, cwk_ref, t0, tt))
        vs_ref[t0:t0 + tt, :] = _conv_silu(v_ref, cwv_ref, t0, tt)
        gt = gates_ref[0, t0:t0 + tt, :]
        lane = lax.broadcasted_iota(jnp.int32, gt.shape, 1)
        for hh in range(2):
            col = 2 * j + hh
            bsel = jnp.sum(jnp.where(lane == col, gt, 0.0), axis=1, keepdims=True)
            gsel = jnp.sum(jnp.where(lane == col + DN_V_HEADS, gt, 0.0), axis=1, keepdims=True)
            bb_ref[hh, t0:t0 + tt, :] = jnp.broadcast_to(bsel, (tt, LANES))
            gb_ref[hh, t0:t0 + tt, :] = jnp.broadcast_to(gsel, (tt, LANES))

    c = CHUNK
    g = min(DN_GROUP, t_len // c)
    rows = g * c
    ri = lax.broadcasted_iota(jnp.int32, (c, c), 0)
    ci = lax.broadcasted_iota(jnp.int32, (c, c), 1)
    lower = ri >= ci
    strict = ri > ci
    eye = jnp.where(ri == ci, 1.0, 0.0)
    pair_mask = strict & ((ri // 2) == (ci // 2))
    merge_masks = [((ri // (2 * s)) == (ci // (2 * s))) & ((ri // s) != (ci // s)) & strict
                   for s in (2, 4, 8, 16, 32)]

    def group_step(gi, carry):
        base = pl.multiple_of(gi * rows, rows)
        kc = ks_ref[pl.ds(base, rows), :].reshape(g, c, hd)
        qc = qs_ref[pl.ds(base, rows), :].reshape(g, c, hd)
        kk1 = _bmm_nt(kc, kc)
        qk1 = _bmm_nt(qc, kc)
        two = lambda x: jnp.concatenate([x, x], axis=0)
        kk, qk, kc2, qc2 = two(kk1), two(qk1), two(kc), two(qc)
        beta_b = jnp.concatenate([bb_ref[hh, pl.ds(base, rows), :].reshape(g, c, LANES) for hh in range(2)], axis=0)
        gcc_b = jnp.concatenate([gb_ref[hh, pl.ds(base, rows), :].reshape(g, c, LANES) for hh in range(2)], axis=0)
        grow = gcr_ref[0, 0, :, pl.ds(base, rows)]
        gcr = jnp.stack([grow[hh:hh + 1, n * c:(n + 1) * c] for hh in range(2) for n in range(g)], axis=0)
        vc = jnp.concatenate([vs_ref[pl.ds(base, rows), hh * hd:(hh + 1) * hd].reshape(g, c, hd)
                              for hh in range(2)], axis=0)
        gl_b = gcc_b[:, c - 1:c, :]
        dm = gcc_b[:, :, :c] - gcr
        decay = jnp.where(lower, jnp.exp(jnp.where(lower, dm, 0.0)), 0.0)
        lm = jnp.where(strict, kk * beta_b[:, :, :c] * decay, 0.0)
        tm_ = eye - jnp.where(pair_mask, lm, 0.0)
        for mk in merge_masks:
            tm_ = tm_ - _bmm(_bmm(tm_, jnp.where(mk, lm, 0.0)), tm_)
        eg = jnp.exp(gcc_b)
        uw = _bmm(tm_, jnp.concatenate([vc * beta_b, kc2 * (beta_b * eg)], axis=2))
        qd = qc2 * eg
        kd = kc2 * jnp.exp(gl_b - gcc_b)
        am = qk * decay
        for hh in range(2):
            cols = slice(hh * hd, (hh + 1) * hd)
            inst = slice(hh * g, (hh + 1) * g)
            u_ref[0, pl.ds(base, rows), cols] = uw[inst, :, :hd].reshape(rows, hd).astype(u_ref.dtype)
            w_ref[0, pl.ds(base, rows), cols] = uw[inst, :, hd:].reshape(rows, hd).astype(w_ref.dtype)
            qd_ref[0, pl.ds(base, rows), cols] = qd[inst].reshape(rows, hd).astype(qd_ref.dtype)
            kd_ref[0, pl.ds(base, rows), cols] = kd[inst].reshape(rows, hd).astype(kd_ref.dtype)
            a_ref[0, pl.ds(base, rows), hh * c:(hh + 1) * c] = am[inst].reshape(rows, c).astype(a_ref.dtype)
        return carry

    lax.fori_loop(0, t_len // rows, group_step, 0)


def dn_prep(proj, conv_w, gates, gcr):
    b, t, _ = proj.shape
    hd = DN_HEAD_DIM
    nq = DN_QK_HEADS
    v_blk0 = 2 * DN_KEY_DIM // (2 * hd)
    wide = pl.BlockSpec((1, t, 2 * hd), lambda bi, j: (bi, 0, j))
    big = jax.ShapeDtypeStruct((b, t, DN_VAL_DIM), BF16)
    return pl.pallas_call(
        functools.partial(_dn_prep_body, t_len=t),
        grid=(b, nq),
        in_specs=[
            pl.BlockSpec((1, t, hd), lambda bi, j: (bi, 0, j)),
            pl.BlockSpec((1, t, hd), lambda bi, j: (bi, 0, nq + j)),
            pl.BlockSpec((1, t, 2 * hd), lambda bi, j: (bi, 0, v_blk0 + j)),
            pl.BlockSpec((4, hd), lambda bi, j: (0, j)),
            pl.BlockSpec((4, hd), lambda bi, j: (0, nq + j)),
            pl.BlockSpec((4, 2 * hd), lambda bi, j: (0, v_blk0 + j)),
            pl.BlockSpec((1, t, LANES), lambda bi, j: (bi, 0, 0)),
            pl.BlockSpec((1, 1, 2, t), lambda bi, j: (bi, j, 0, 0)),
        ],
        out_specs=[wide, wide, wide, wide, pl.BlockSpec((1, t, 2 * CHUNK), lambda bi, j: (bi, 0, j))],
        out_shape=[big, big, big, big, jax.ShapeDtypeStruct((b, t, DN_V_HEADS * CHUNK), BF16)],
        scratch_shapes=[
            pltpu.VMEM((t, hd), F32), pltpu.VMEM((t, hd), F32), pltpu.VMEM((t, 2 * hd), F32),
            pltpu.VMEM((2, t, LANES), F32), pltpu.VMEM((2, t, LANES), F32),
        ],
        compiler_params=_params("parallel", "arbitrary"),
        name="dn_prep",
    )(proj, proj, proj, conv_w, conv_w, conv_w, gates, gcr)


DN_REC_HEADS = 16
DN_REC_ROWS = 512


def _dn_rec_body(u_ref, w_ref, qd_ref, kd_ref, a_ref, z_ref, gl_ref, onorm_ref, o_ref, s_ref, *, t_len):
    hd = DN_HEAD_DIM
    c = CHUNK

    @pl.when(pl.program_id(2) == 0)
    def _():
        s_ref[...] = jnp.zeros_like(s_ref)

    onorm = onorm_ref[...]

    def step(ci, carry):
        r0 = pl.multiple_of(ci * c, c)
        egl = jnp.exp(gl_ref[0, 0, ci])
        heads = range(DN_REC_HEADS)
        cols = [slice(h * hd, (h + 1) * hd) for h in heads]
        st = [s_ref[h] for h in heads]
        r = [jnp.dot(jnp.concatenate([w_ref[0, pl.ds(r0, c), cols[h]], qd_ref[0, pl.ds(r0, c), cols[h]]], axis=0),
                     st[h].astype(BF16), preferred_element_type=F32) for h in heads]
        vb = [(u_ref[0, pl.ds(r0, c), cols[h]].astype(F32) - r[h][:c]).astype(BF16) for h in heads]
        upd = [lax.dot_general(kd_ref[0, pl.ds(r0, c), cols[h]], vb[h], (((0,), (0,)), ((), ())),
                               preferred_element_type=F32) for h in heads]
        o = [r[h][c:] + jnp.dot(a_ref[0, pl.ds(r0, c), h * c:(h + 1) * c], vb[h], preferred_element_type=F32)
             for h in heads]
        for h in heads:
            s_ref[h] = st[h] * egl[:, h:h + 1] + upd[h]
        for h in heads:
            y = o[h] * lax.rsqrt(jnp.mean(o[h] * o[h], axis=-1, keepdims=True) + EPS) * onorm
            zz = z_ref[0, pl.ds(r0, c), cols[h]]
            o_ref[0, pl.ds(r0, c), cols[h]] = (y * (zz * _sigmoid(zz))).astype(o_ref.dtype)
        return carry

    lax.fori_loop(0, t_len // c, step, 0)


def dn_rec(u, w, qd, kd, a, proj, gl, out_norm):
    b, t, _ = u.shape
    hd = DN_HEAD_DIM
    hb = DN_REC_HEADS
    ng = DN_V_HEADS // hb
    tt = min(DN_REC_ROWS, t)
    z_blk0 = DN_CONV_DIM // (hb * hd)
    wide = pl.BlockSpec((1, tt, hb * hd), lambda bi, g, ti: (bi, ti, g))
    return pl.pallas_call(
        functools.partial(_dn_rec_body, t_len=tt),
        grid=(b, ng, t // tt),
        in_specs=[
            wide, wide, wide, wide,
            pl.BlockSpec((1, tt, hb * CHUNK), lambda bi, g, ti: (bi, ti, g)),
            pl.BlockSpec((1, tt, hb * hd), lambda bi, g, ti: (bi, ti, z_blk0 + g)),
            pl.BlockSpec((1, 1, tt // CHUNK, 1, hb), lambda bi, g, ti: (bi, g, ti, 0, 0)),
            pl.BlockSpec((1, hd), lambda bi, g, ti: (0, 0)),
        ],
        out_specs=wide,
        out_shape=jax.ShapeDtypeStruct((b, t, DN_VAL_DIM), BF16),
        scratch_shapes=[pltpu.VMEM((hb, hd, hd), F32)],
        compiler_params=_params("parallel", "parallel", "arbitrary"),
        name="dn_rec",
    )(u, w, qd, kd, a, proj, gl, out_norm.reshape(1, hd))


def gated_deltanet_layer(h2, u, b, t, next_gain, w_in, conv_w, a_log, dt_bias, out_norm, w_out):
    n = b * t
    n_qkvz = DN_CONV_DIM + DN_VAL_DIM
    proj = matmul(u, w_in[:, :n_qkvz].astype(BF16), F32, tn=2048).reshape(b, t, n_qkvz)
    w_ba = jnp.pad(w_in[:, n_qkvz:], ((0, 0), (0, LANES - 2 * DN_V_HEADS))).astype(BF16)
    pba = matmul(u, w_ba, F32)
    pad = lambda p: jnp.pad(p.astype(F32), (DN_V_HEADS, LANES - 2 * DN_V_HEADS)).reshape(1, LANES)
    gates = dn_gates(pba, pad(a_log), pad(dt_bias))
    gc = gates[:, DN_V_HEADS:2 * DN_V_HEADS].reshape(b, t, DN_V_HEADS)
    gcr = gc.reshape(b, t, DN_QK_HEADS, 2).transpose(0, 2, 3, 1)
    ng = DN_V_HEADS // DN_REC_HEADS
    gl = gc[:, CHUNK - 1::CHUNK, :].reshape(b, t // CHUNK, ng, 1, DN_REC_HEADS).transpose(0, 2, 1, 3, 4)
    uu, ww, qd, kd, am = dn_prep(proj, conv_w, gates.reshape(b, t, LANES), gcr)
    o = dn_rec(uu, ww, qd, kd, am, proj, gl, out_norm)
    return matmul_res_norm(o.reshape(n, DN_VAL_DIM), w_out.astype(BF16), h2, next_gain, BF16)


def _dsa_prep_body(p_ref, qn_ref, kvn_ref, kin_ref, q_out, kv_out, ki_out, w_out):
    def nrm(x, g):
        return x * lax.rsqrt(jnp.mean(x * x, axis=-1, keepdims=True) + EPS) * g

    a, b2, c2 = Q_LORA, Q_LORA + KV_LORA, Q_LORA + KV_LORA + IDX_DIM
    q_out[...] = nrm(p_ref[:, 0:a], qn_ref[...]).astype(q_out.dtype)
    kv_out[...] = nrm(p_ref[:, a:b2], kvn_ref[...]).astype(kv_out.dtype)
    ki_out[...] = nrm(p_ref[:, b2:c2], kin_ref[...]).astype(ki_out.dtype)
    w_out[...] = p_ref[:, c2:c2 + LANES] * (IDX_HEADS ** -0.5 * IDX_DIM ** -0.5)


def dsa_prep(proj, q_norm, kv_norm, kidx_norm, tm=512):
    n, width = proj.shape
    row = lambda i: (i, 0)
    fix = lambda i: (0, 0)
    return pl.pallas_call(
        _dsa_prep_body,
        grid=(n // tm,),
        in_specs=[pl.BlockSpec((tm, width), row), pl.BlockSpec((1, Q_LORA), fix),
                  pl.BlockSpec((1, KV_LORA), fix), pl.BlockSpec((1, IDX_DIM), fix)],
        out_specs=[pl.BlockSpec((tm, Q_LORA), row), pl.BlockSpec((tm, KV_LORA), row),
                   pl.BlockSpec((tm, IDX_DIM), row), pl.BlockSpec((tm, LANES), row)],
        out_shape=[jax.ShapeDtypeStruct((n, Q_LORA), BF16), jax.ShapeDtypeStruct((n, KV_LORA), BF16),
                   jax.ShapeDtypeStruct((n, IDX_DIM), BF16), jax.ShapeDtypeStruct((n, LANES), F32)],
        compiler_params=_params("parallel"),
        name="dsa_prep",
    )(proj, q_norm.reshape(1, -1), kv_norm.reshape(1, -1), kidx_norm.reshape(1, -1))


def _qabs_body(q_ref, w_ref, o_ref, *, r):
    for h in range(DSA_HEADS):
        res = jnp.dot(q_ref[:, h * DSA_HEAD_DIM:(h + 1) * DSA_HEAD_DIM], w_ref[h],
                      preferred_element_type=F32) * (DSA_HEAD_DIM ** -0.5)
        for rr in range(r):
            o_ref[rr, h] = res[rr * QBLOCK:(rr + 1) * QBLOCK].astype(o_ref.dtype)


def q_absorb(q_all, w_uk, tm=256):
    n = q_all.shape[0]
    r = tm // QBLOCK
    hw = DSA_HEADS * DSA_HEAD_DIM
    return pl.pallas_call(
        functools.partial(_qabs_body, r=r),
        grid=(n // tm,),
        in_specs=[pl.BlockSpec((tm, hw), lambda i: (i, 0)),
                  pl.BlockSpec((DSA_HEADS, DSA_HEAD_DIM, KV_LORA), lambda i: (0, 0, 0))],
        out_specs=pl.BlockSpec((r, DSA_HEADS, QBLOCK, KV_LORA), lambda i: (i, 0, 0, 0)),
        out_shape=jax.ShapeDtypeStruct((n // QBLOCK, DSA_HEADS, QBLOCK, KV_LORA), BF16),
        compiler_params=_params("parallel"),
        name="q_absorb",
    )(q_all, w_uk)


def _vup_body(o_ref, w_ref, out_ref, *, r):
    for h in range(DSA_HEADS):
        for rr in range(r):
            out_ref[rr * QBLOCK:(rr + 1) * QBLOCK, h * DSA_HEAD_DIM:(h + 1) * DSA_HEAD_DIM] = jnp.dot(
                o_ref[rr, h], w_ref[h], preferred_element_type=F32).astype(out_ref.dtype)


def v_up(o_lat, w_uv, tm=256):
    nb = o_lat.shape[0]
    r = tm // QBLOCK
    hw = DSA_HEADS * DSA_HEAD_DIM
    return pl.pallas_call(
        functools.partial(_vup_body, r=r),
        grid=(nb // r,),
        in_specs=[pl.BlockSpec((r, DSA_HEADS, QBLOCK, KV_LORA), lambda i: (i, 0, 0, 0)),
                  pl.BlockSpec((DSA_HEADS, KV_LORA, DSA_HEAD_DIM), lambda i: (0, 0, 0))],
        out_specs=pl.BlockSpec((tm, hw), lambda i: (i, 0)),
        out_shape=jax.ShapeDtypeStruct((nb * QBLOCK, hw), BF16),
        compiler_params=_params("parallel"),
        name="v_up",
    )(o_lat, w_uv)


def _dsa_core_body(qidx_ref, widx_ref, kidx_ref, ckv_ref, qabs_ref, o_ref,
                   keys_ref, tau_ref, bias_ref, s_ref, p_ref, m_ref, l_ref, acc_ref, *, top_k, kt):
    i = pl.program_id(1)
    nq = QBLOCK
    nkt = keys_ref.shape[0]
    n_act = ((i + 1) * nq + kt - 1) // kt
    wt = widx_ref[0].T
    krow = lax.broadcasted_iota(jnp.int32, (kt, nq), 0)
    qcol = lax.broadcasted_iota(jnp.int32, (kt, nq), 1)
    q_chunk = (i * nq + qcol) // CHUNK
    tn = (((1,), (1,)), ((), ()))
    sub = 8

    def index_tile(jt, carry):
        k0 = pl.multiple_of(jt * kt, kt)
        kblk = kidx_ref[0, pl.ds(k0, kt), :]
        sc = jnp.zeros((kt, nq), F32)
        for h in range(IDX_HEADS):
            d = lax.dot_general(kblk, qidx_ref[0, :, h * IDX_DIM:(h + 1) * IDX_DIM], tn,
                                preferred_element_type=F32)
            sc = sc + jnp.maximum(d, 0.0) * wt[h:h + 1, :]
        sc = jnp.where(sc == 0.0, 0.0, sc)
        bits = pltpu.bitcast(sc, jnp.int32)
        key = bits ^ ((bits >> 31) & 0x7FFFFFFF)
        adm = ((k0 + krow) // CHUNK) <= q_chunk
        keys_ref[jt] = jnp.where(adm, key, INT_MIN)
        return carry

    lax.fori_loop(0, n_act, index_tile, 0)

    for jt in range(nkt):
        @pl.when(jt >= n_act)
        def _():
            keys_ref[jt] = jnp.full((kt, nq), INT_MIN, jnp.int32)

    def bisect_over(ntiles):
        def bisect(it, prefix):
            cand = prefix | lax.shift_left(jnp.int32(1), 31 - it)
            cand_s = cand ^ INT_MIN
            cnt = jnp.zeros((sub, nq), F32)
            for jt in range(ntiles):
                hit = jnp.where(keys_ref[jt] >= cand_s, 1.0, 0.0)
                cnt = cnt + jnp.sum(hit.reshape(kt // sub, sub, nq), axis=0)
            return jnp.where(jnp.sum(cnt, axis=0, keepdims=True) >= top_k, cand, prefix)

        prefix = lax.fori_loop(0, 32, bisect, jnp.zeros((1, nq), jnp.int32))
        tau_ref[...] = jnp.broadcast_to(jnp.maximum(prefix ^ INT_MIN, INT_MIN + 1), tau_ref.shape)

    step = 2 if nkt % 2 == 0 else 1
    for ntiles in range(step, nkt + 1, step):
        @pl.when((n_act > ntiles - step) & (n_act <= ntiles))
        def _():
            bisect_over(ntiles)

    tau = tau_ref[0:1, :]

    def tie_counts(jt, carry):
        gt, eq = carry
        kj = keys_ref[jt]
        gt = gt + jnp.sum(jnp.where(kj > tau, 1.0, 0.0).reshape(kt // sub, sub, nq), axis=0)
        eq = eq + jnp.sum(jnp.where(kj == tau, 1.0, 0.0).reshape(kt // sub, sub, nq), axis=0)
        return gt, eq

    zero8 = jnp.zeros((sub, nq), F32)
    gt8, eq8 = lax.fori_loop(0, n_act, tie_counts, (zero8, zero8))
    need = top_k - jnp.sum(gt8, axis=0, keepdims=True)
    surplus = jnp.sum(eq8, axis=0, keepdims=True) > need

    @pl.when(jnp.max(jnp.where(surplus, 1.0, 0.0)) > 0.0)
    def _():
        before_row = (lax.broadcasted_iota(jnp.int32, (kt, kt), 1)
                      < lax.broadcasted_iota(jnp.int32, (kt, kt), 0))
        tri = jnp.where(before_row, 1.0, 0.0).astype(BF16)

        def drop_surplus(jt, seen):
            kj = keys_ref[jt]
            eq = kj == tau
            eqf = jnp.where(eq, 1.0, 0.0)
            earlier = jnp.dot(tri, eqf.astype(BF16), preferred_element_type=F32) + seen
            keys_ref[jt] = jnp.where(eq & (earlier >= need), INT_MIN, kj)
            return seen + jnp.sum(eqf, axis=0, keepdims=True)

        lax.fori_loop(0, n_act, drop_surplus, jnp.zeros((1, nq), F32))

    rows = DSA_HEADS * nq
    rep = kt // LANES
    m_ref[...] = jnp.full(m_ref.shape, M_INIT, F32)
    l_ref[...] = jnp.zeros_like(l_ref)
    acc_ref[...] = jnp.zeros_like(acc_ref)

    def attend_tile(jt, carry):
        k0 = pl.multiple_of(jt * kt, kt)
        ck = ckv_ref[0, pl.ds(k0, kt), :]
        bias_ref[...] = jnp.where(keys_ref[jt] >= tau, 0.0, MASKED).T
        s_ref[...] = lax.dot_general(qabs_ref[0].reshape(rows, KV_LORA), ck, tn, preferred_element_type=F32)

        for h in range(DSA_HEADS):
            hr = slice(h * nq, (h + 1) * nq)
            s = s_ref[hr, :] + bias_ref[...]
            m_old = m_ref[hr, :]
            m_new = jnp.maximum(m_old, jnp.max(s, axis=1, keepdims=True))
            alpha = jnp.exp(m_old - m_new)
            p = jnp.exp(s - jnp.concatenate([m_new] * rep, axis=1))
            l_ref[hr, :] = alpha * l_ref[hr, :] + jnp.sum(p, axis=1, keepdims=True)
            m_ref[hr, :] = m_new
            acc_ref[hr, :] = acc_ref[hr, :] * jnp.concatenate([alpha] * (KV_LORA // LANES), axis=1)
            p_ref[hr, :] = p.astype(BF16)
        acc_ref[...] += jnp.dot(p_ref[...], ck, preferred_element_type=F32)
        return carry

    lax.fori_loop(0, n_act, attend_tile, 0)
    inv_l = 1.0 / l_ref[...]
    out = acc_ref[...] * jnp.concatenate([inv_l] * (KV_LORA // LANES), axis=1)
    o_ref[0] = out.reshape(DSA_HEADS, nq, KV_LORA).astype(o_ref.dtype)


def dsa_core(q_all, widx, kidx, ckv, qabs, b, t, kt=256):
    nblk = t // QBLOCK
    top_k = min(IDX_TOPK, t // 4)
    hw = IDX_HEADS * IDX_DIM
    rows = DSA_HEADS * QBLOCK
    return pl.pallas_call(
        functools.partial(_dsa_core_body, top_k=top_k, kt=kt),
        grid=(b, nblk),
        in_specs=[
            pl.BlockSpec((1, QBLOCK, hw), lambda bi, i: (bi, i, 1)),
            pl.BlockSpec((1, QBLOCK, LANES), lambda bi, i: (bi, i, 0)),
            pl.BlockSpec((1, t, IDX_DIM), lambda bi, i: (bi, 0, 0)),
            pl.BlockSpec((1, t, KV_LORA), lambda bi, i: (bi, 0, 0)),
            pl.BlockSpec((1, DSA_HEADS, QBLOCK, KV_LORA), lambda bi, i: (bi * nblk + i, 0, 0, 0)),
        ],
        out_specs=pl.BlockSpec((1, DSA_HEADS, QBLOCK, KV_LORA), lambda bi, i: (bi * nblk + i, 0, 0, 0)),
        out_shape=jax.ShapeDtypeStruct((b * nblk, DSA_HEADS, QBLOCK, KV_LORA), BF16),
        scratch_shapes=[pltpu.VMEM((t // kt, kt, QBLOCK), jnp.int32), pltpu.VMEM((8, QBLOCK), jnp.int32),
                        pltpu.VMEM((QBLOCK, kt), F32),
                        pltpu.VMEM((rows, kt), F32), pltpu.VMEM((rows, kt), BF16),
                        pltpu.VMEM((rows, LANES), F32), pltpu.VMEM((rows, LANES), F32),
                        pltpu.VMEM((rows, KV_LORA), F32)],
        compiler_params=_params("parallel", "arbitrary"),
        name="dsa_core",
    )(q_all.reshape(b, t, -1), widx.reshape(b, t, LANES), kidx.reshape(b, t, IDX_DIM),
      ckv.reshape(b, t, KV_LORA), qabs)


def dsa_layer(h2, u, b, t, next_gain, w_in, q_norm, kv_norm, kidx_norm, w_uq, w_uk, w_uv, w_out):
    width = Q_LORA + KV_LORA + IDX_DIM + LANES
    w_in_p = jnp.pad(w_in, ((0, 0), (0, width - w_in.shape[1]))).astype(BF16)
    proj = matmul(u, w_in_p, F32)
    qlat, ckv, kidx, widx = dsa_prep(proj, q_norm, kv_norm, kidx_norm)
    q_all = matmul(qlat, w_uq.astype(BF16), BF16)
    qabs = q_absorb(q_all, w_uk.astype(BF16))
    o_lat = dsa_core(q_all, widx, kidx, ckv, qabs, b, t)
    o = v_up(o_lat, w_uv.astype(BF16))
    return matmul_res_norm(o, w_out.astype(BF16), h2, next_gain, BF16)


def mlp(h2, u, next_gain, u_dtype, w_up, w_down):
    a = matmul(u, w_up.astype(BF16), BF16, relu2=True, tn=2048)
    return matmul_res_norm(a, w_down.astype(BF16), h2, next_gain, u_dtype)


def kernel(x, norm_mix, norm_mlp, norm_final, dn_w_in, dn_conv_w, dn_a_log, dn_dt_bias, dn_out_norm, dn_w_out, dsa_w_in, dsa_q_norm, dsa_kv_norm, dsa_kidx_norm, dsa_w_uq, dsa_w_uk, dsa_w_uv, dsa_w_out, mlp_w_up, mlp_w_down):
    b, t, d = x.shape
    depth = norm_mix.shape[0]
    h2 = x.reshape(b * t, d)
    u = rms_norm(h2, norm_mix[0], BF16)
    for i in range(depth):
        j = i // 2
        if i % 2 == 0:
            h2, u = gated_deltanet_layer(h2, u, b, t, norm_mlp[i], dn_w_in[j], dn_conv_w[j], dn_a_log[j],
                                         dn_dt_bias[j], dn_out_norm[j], dn_w_out[j])
        else:
            h2, u = dsa_layer(h2, u, b, t, norm_mlp[i], dsa_w_in[j], dsa_q_norm[j], dsa_kv_norm[j],
                              dsa_kidx_norm[j], dsa_w_uq[j], dsa_w_uk[j], dsa_w_uv[j], dsa_w_out[j])
        last = i == depth - 1
        h2, u = mlp(h2, u, norm_final if last else norm_mix[i + 1], x.dtype if last else BF16,
                    mlp_w_up[i], mlp_w_down[i])
    return u.reshape(b, t, d)
```

```python
import functools

import jax
import jax.numpy as jnp
from jax import lax
from jax.experimental import pallas as pl
from jax.experimental.pallas import tpu as pltpu

F32 = jnp.float32
BF16 = jnp.bfloat16
EPS = 1e-6
CHUNK = 64
QBLOCK = 128
DN_QK_HEADS = 16
DN_V_HEADS = 32
DN_HEAD_DIM = 128
DN_KEY_DIM = DN_QK_HEADS * DN_HEAD_DIM
DN_VAL_DIM = DN_V_HEADS * DN_HEAD_DIM
DN_CONV_DIM = 2 * DN_KEY_DIM + DN_VAL_DIM
DSA_HEADS = 16
DSA_HEAD_DIM = 128
Q_LORA = 512
KV_LORA = 256
IDX_HEADS = 16
IDX_DIM = 128
IDX_TOPK = 256
LANES = 128
VMEM_LIMIT_BYTES = 56 * 1024 * 1024
INT_MIN = -(2 ** 31)
MASKED = -1e30
M_INIT = -1e20
LOG2_E = 1.4426950408889634


def _params(*sem):
    return pltpu.CompilerParams(dimension_semantics=sem, vmem_limit_bytes=VMEM_LIMIT_BYTES)


def _sigmoid(x):
    return 1.0 / (1.0 + jnp.exp(-x))


def _norm_body(x_ref, g_ref, o_ref):
    x = x_ref[...]
    y = x * lax.rsqrt(jnp.mean(x * x, axis=-1, keepdims=True) + EPS) * g_ref[...]
    o_ref[...] = y.astype(o_ref.dtype)


def rms_norm(x, g, out_dtype, tm=512):
    n, d = x.shape
    return pl.pallas_call(
        _norm_body,
        grid=(n // tm,),
        in_specs=[pl.BlockSpec((tm, d), lambda i: (i, 0)), pl.BlockSpec((1, d), lambda i: (0, 0))],
        out_specs=pl.BlockSpec((tm, d), lambda i: (i, 0)),
        out_shape=jax.ShapeDtypeStruct((n, d), out_dtype),
        compiler_params=_params("parallel"),
        name="rms_norm",
    )(x, g.reshape(1, d))


MM_SUB_COLS = 512


def _mm_body(a_ref, w_ref, *rest, nk, relu2, residual):
    if residual:
        r_ref, o_ref = rest[0], rest[1]
    else:
        r_ref, o_ref = None, rest[0]
    acc_ref = rest[-1] if nk > 1 else None

    def finish(acc, cols=slice(None)):
        if relu2:
            acc = jnp.square(jnp.maximum(acc, 0.0))
        if residual:
            acc = acc + r_ref[:, cols]
        o_ref[:, cols] = acc.astype(o_ref.dtype)

    sub = min(MM_SUB_COLS, o_ref.shape[1])
    blocks = [slice(c0, c0 + sub) for c0 in range(0, o_ref.shape[1], sub)]

    def k_step(first, last):
        for cols in blocks:
            part = jnp.dot(a_ref[...], w_ref[:, cols], preferred_element_type=F32)
            if not first:
                part = part + acc_ref[:, cols]
            if last:
                finish(part, cols)
            else:
                acc_ref[:, cols] = part

    if nk == 1:
        k_step(True, True)
        return
    k = pl.program_id(2)
    pl.when(k == 0)(functools.partial(k_step, True, False))
    pl.when((k > 0) & (k < nk - 1))(functools.partial(k_step, False, False))
    pl.when(k == nk - 1)(functools.partial(k_step, False, True))


def matmul(a, w, out_dtype, *, relu2=False, residual=None, tm=1024, tn=1024, tk=2048):
    m, kdim = a.shape
    n = w.shape[1]
    tm, tn, tk = min(tm, m), min(tn, n), min(tk, kdim)
    assert m % tm == 0 and n % tn == 0 and kdim % tk == 0
    nk = kdim // tk
    in_specs = [pl.BlockSpec((tm, tk), lambda i, j, k: (i, k)), pl.BlockSpec((tk, tn), lambda i, j, k: (k, j))]
    args = [a, w]
    if residual is not None:
        in_specs.append(pl.BlockSpec((tm, tn), lambda i, j, k: (i, j)))
        args.append(residual)
    return pl.pallas_call(
        functools.partial(_mm_body, nk=nk, relu2=relu2, residual=residual is not None),
        grid=(m // tm, n // tn, nk),
        in_specs=in_specs,
        out_specs=pl.BlockSpec((tm, tn), lambda i, j, k: (i, j)),
        out_shape=jax.ShapeDtypeStruct((m, n), out_dtype),
        scratch_shapes=[pltpu.VMEM((tm, tn), F32)] if nk > 1 else [],
        compiler_params=_params("parallel", "parallel", "arbitrary"),
        name="matmul",
    )(*args)


def _mm_res_norm_body(a_ref, w_ref, r_ref, g_ref, h_ref, u_ref, *scratch, nk):
    acc_ref = scratch[0] if nk > 1 else None
    n = h_ref.shape[1]
    sub = min(MM_SUB_COLS, n)
    blocks = [slice(c0, c0 + sub) for c0 in range(0, n, sub)]

    def k_step(first, last):
        for cols in blocks:
            part = jnp.dot(a_ref[...], w_ref[:, cols], preferred_element_type=F32)
            if not first:
                part = part + acc_ref[:, cols]
            if last:
                h_ref[:, cols] = part + r_ref[:, cols]
            else:
                acc_ref[:, cols] = part
        if last:
            h = h_ref[...]
            scale = lax.rsqrt(jnp.mean(h * h, axis=-1, keepdims=True) + EPS)
            u_ref[...] = (h * scale * g_ref[...]).astype(u_ref.dtype)

    if nk == 1:
        k_step(True, True)
        return
    k = pl.program_id(1)
    pl.when(k == 0)(functools.partial(k_step, True, False))
    pl.when((k > 0) & (k < nk - 1))(functools.partial(k_step, False, False))
    pl.when(k == nk - 1)(functools.partial(k_step, False, True))


def matmul_res_norm(a, w, residual, gain, u_dtype, *, tm=512, tk=2048):
    m, kdim = a.shape
    n = w.shape[1]
    tm, tk = min(tm, m), min(tk, kdim)
    assert m % tm == 0 and kdim % tk == 0
    nk = kdim // tk
    row = pl.BlockSpec((tm, n), lambda i, k: (i, 0))
    return pl.pallas_call(
        functools.partial(_mm_res_norm_body, nk=nk),
        grid=(m // tm, nk),
        in_specs=[pl.BlockSpec((tm, tk), lambda i, k: (i, k)), pl.BlockSpec((tk, n), lambda i, k: (k, 0)),
                  row, pl.BlockSpec((1, n), lambda i, k: (0, 0))],
        out_specs=[row, row],
        out_shape=[jax.ShapeDtypeStruct((m, n), F32), jax.ShapeDtypeStruct((m, n), u_dtype)],
        scratch_shapes=[pltpu.VMEM((tm, n), F32)] if nk > 1 else [],
        compiler_params=_params("parallel", "arbitrary"),
        name="matmul_res_norm",
    )(a, w, residual, gain.reshape(1, n))


def _dn_gates_body(x_ref, alog_ref, dt_ref, o_ref):
    x = x_ref[...]
    g = -jnp.exp(alog_ref[...]) * (jnp.maximum(x + dt_ref[...], 0.0)
                                   + jnp.log(1.0 + jnp.exp(-jnp.abs(x + dt_ref[...]))))
    pos = lax.broadcasted_iota(jnp.int32, x.shape, 0) % CHUNK
    s = 1
    while s < CHUNK:
        g = g + jnp.where(pos >= s, pltpu.roll(g, s, axis=0), 0.0)
        s *= 2
    lane = lax.broadcasted_iota(jnp.int32, x.shape, 1)
    o_ref[...] = jnp.where(lane < DN_V_HEADS, _sigmoid(x), g)


def dn_gates(pba, alog_pad, dt_pad, tm=512):
    n = pba.shape[0]
    return pl.pallas_call(
        _dn_gates_body,
        grid=(n // tm,),
        in_specs=[pl.BlockSpec((tm, LANES), lambda i: (i, 0)),
                  pl.BlockSpec((1, LANES), lambda i: (0, 0)),
                  pl.BlockSpec((1, LANES), lambda i: (0, 0))],
        out_specs=pl.BlockSpec((tm, LANES), lambda i: (i, 0)),
        out_shape=jax.ShapeDtypeStruct((n, LANES), F32),
        compiler_params=_params("parallel"),
        name="dn_gates",
    )(pba, alog_pad, dt_pad)


def _conv_silu(ref, cw_ref, t0, tt):
    cw = cw_ref[...]
    acc = ref[0, t0:t0 + tt, :] * cw[3:4, :]
    for s in (1, 2, 3):
        if t0 == 0:
            x = ref[0, 0:tt, :]
            rows = lax.broadcasted_iota(jnp.int32, x.shape, 0)
            xs = jnp.where(rows >= s, pltpu.roll(x, s, axis=0), 0.0)
        else:
            xs = ref[0, t0 - s:t0 - s + tt, :]
        acc = acc + xs * cw[3 - s:4 - s, :]
    return acc * _sigmoid(acc)


def _l2norm(t):
    return t * lax.rsqrt(jnp.sum(t * t, axis=-1, keepdims=True) + EPS)


def _bmm(a, b):
    return jnp.einsum("nij,njk->nik", a.astype(BF16), b.astype(BF16), preferred_element_type=F32)


def _bmm_nt(a, b):
    return jnp.einsum("nid,njd->nij", a.astype(BF16), b.astype(BF16), preferred_element_type=F32)


DN_GROUP = 32


def _dn_prep_body(q_ref, k_ref, v_ref, cwq_ref, cwk_ref, cwv_ref, gates_ref, gcr_ref,
                  u_ref, w_ref, qd_ref, kd_ref, a_ref, qs_ref, ks_ref, vs_ref, bb_ref, gb_ref, *, t_len):
    j = pl.program_id(1)
    tt = 256
    hd = DN_HEAD_DIM
    for ti in range(t_len // tt):
        t0 = ti * tt
        qs_ref[t0:t0 + tt, :] = _l2norm(_conv_silu(q_ref, cwq_ref, t0, tt)) * (hd ** -0.5)
        ks_ref[t0:t0 + tt, :] = _l2norm(_conv_silu(k_ref, cwk_ref, t0, tt))
        vs_ref[t0:t0 + tt, :] = _conv_silu(v_ref, cwv_ref, t0, tt)
        gt = gates_ref[0, t0:t0 + tt, :]
        lane = lax.broadcasted_iota(jnp.int32, gt.shape, 1)
        for hh in range(2):
            col = 2 * j + hh
            bsel = jnp.sum(jnp.where(lane == col, gt, 0.0), axis=1, keepdims=True)
            gsel = jnp.sum(jnp.where(lane == col + DN_V_HEADS, gt, 0.0), axis=1, keepdims=True)
            bb_ref[hh, t0:t0 + tt, :] = jnp.broadcast_to(bsel, (tt, LANES))
            gb_ref[hh, t0:t0 + tt, :] = jnp.broadcast_to(gsel, (tt, LANES))

    c = CHUNK
    g = min(DN_GROUP, t_len // c)
    rows = g * c
    ri = lax.broadcasted_iota(jnp.int32, (c, c), 0)
    ci = lax.broadcasted_iota(jnp.int32, (c, c), 1)
    lower = ri >= ci
    strict = ri > ci
    eye = jnp.where(ri == ci, 1.0, 0.0)
    pair_mask = strict & ((ri // 2) == (ci // 2))
    merge_masks = [((ri // (2 * s)) == (ci // (2 * s))) & ((ri // s) != (ci // s)) & strict
                   for s in (2, 4, 8, 16, 32)]

    def group_step(gi, carry):
        base = pl.multiple_of(gi * rows, rows)
        kc = ks_ref[pl.ds(base, rows), :].reshape(g, c, hd)
        qc = qs_ref[pl.ds(base, rows), :].reshape(g, c, hd)
        kk1 = _bmm_nt(kc, kc)
        qk1 = _bmm_nt(qc, kc)
        two = lambda x: jnp.concatenate([x, x], axis=0)
        kk, qk, kc2, qc2 = two(kk1), two(qk1), two(kc), two(qc)
        beta_b = jnp.concatenate([bb_ref[hh, pl.ds(base, rows), :].reshape(g, c, LANES) for hh in range(2)], axis=0)
        gcc_b = jnp.concatenate([gb_ref[hh, pl.ds(base, rows), :].reshape(g, c, LANES) for hh in range(2)], axis=0)
        grow = gcr_ref[0, 0, :, pl.ds(base, rows)]
        gcr = jnp.stack([grow[hh:hh + 1, n * c:(n + 1) * c] for hh in range(2) for n in range(g)], axis=0)
        vc = jnp.concatenate([vs_ref[pl.ds(base, rows), hh * hd:(hh + 1) * hd].reshape(g, c, hd)
                              for hh in range(2)], axis=0)
        gl_b = gcc_b[:, c - 1:c, :]
        dm = gcc_b[:, :, :c] - gcr
        decay = jnp.where(lower, jnp.exp(jnp.where(lower, dm, 0.0)), 0.0)
        lm = jnp.where(strict, kk * beta_b[:, :, :c] * decay, 0.0)
        tm_ = eye - jnp.where(pair_mask, lm, 0.0)
        lmb = lm.astype(BF16)
        zero_b = jnp.zeros_like(lmb)
        for mk in merge_masks:
            tb = tm_.astype(BF16)
            tm_ = tm_ - _bmm(_bmm(tb, jnp.where(mk, lmb, zero_b)), tb)
        eg = jnp.exp(gcc_b)
        uw = _bmm(tm_, jnp.concatenate([vc * beta_b, kc2 * (beta_b * eg)], axis=2))
        qd = qc2 * eg
        kd = kc2 * jnp.exp(gl_b - gcc_b)
        am = qk * decay
        for hh in range(2):
            cols = slice(hh * hd, (hh + 1) * hd)
            inst = slice(hh * g, (hh + 1) * g)
            u_ref[0, pl.ds(base, rows), cols] = uw[inst, :, :hd].reshape(rows, hd).astype(u_ref.dtype)
            w_ref[0, pl.ds(base, rows), cols] = uw[inst, :, hd:].reshape(rows, hd).astype(w_ref.dtype)
            qd_ref[0, pl.ds(base, rows), cols] = qd[inst].reshape(rows, hd).astype(qd_ref.dtype)
            kd_ref[0, pl.ds(base, rows), cols] = kd[inst].reshape(rows, hd).astype(kd_ref.dtype)
            a_ref[0, pl.ds(base, rows), hh * c:(hh + 1) * c] = am[inst].reshape(rows, c).astype(a_ref.dtype)
        return carry

    lax.fori_loop(0, t_len // rows, group_step, 0)


def dn_prep(proj, conv_w, gates, gcr):
    b, t, _ = proj.shape
    hd = DN_HEAD_DIM
    nq = DN_QK_HEADS
    v_blk0 = 2 * DN_KEY_DIM // (2 * hd)
    wide = pl.BlockSpec((1, t, 2 * hd), lambda bi, j: (bi, 0, j))
    big = jax.ShapeDtypeStruct((b, t, DN_VAL_DIM), BF16)
    return pl.pallas_call(
        functools.partial(_dn_prep_body, t_len=t),
        grid=(b, nq),
        in_specs=[
            pl.BlockSpec((1, t, hd), lambda bi, j: (bi, 0, j)),
            pl.BlockSpec((1, t, hd), lambda bi, j: (bi, 0, nq + j)),
            pl.BlockSpec((1, t, 2 * hd), lambda bi, j: (bi, 0, v_blk0 + j)),
            pl.BlockSpec((4, hd), lambda bi, j: (0, j)),
            pl.BlockSpec((4, hd), lambda bi, j: (0, nq + j)),
            pl.BlockSpec((4, 2 * hd), lambda bi, j: (0, v_blk0 + j)),
            pl.BlockSpec((1, t, LANES), lambda bi, j: (bi, 0, 0)),
            pl.BlockSpec((1, 1, 2, t), lambda bi, j: (bi, j, 0, 0)),
        ],
        out_specs=[wide, wide, wide, wide, pl.BlockSpec((1, t, 2 * CHUNK), lambda bi, j: (bi, 0, j))],
        out_shape=[big, big, big, big, jax.ShapeDtypeStruct((b, t, DN_V_HEADS * CHUNK), BF16)],
        scratch_shapes=[
            pltpu.VMEM((t, hd), F32), pltpu.VMEM((t, hd), F32), pltpu.VMEM((t, 2 * hd), F32),
            pltpu.VMEM((2, t, LANES), F32), pltpu.VMEM((2, t, LANES), F32),
        ],
        compiler_params=_params("parallel", "arbitrary"),
        name="dn_prep",
    )(proj, proj, proj, conv_w, conv_w, conv_w, gates, gcr)


DN_REC_HEADS = 16
DN_REC_ROWS = 512


def _dn_rec_body(u_ref, w_ref, qd_ref, kd_ref, a_ref, z_ref, gl_ref, onorm_ref, o_ref, s_ref, *, t_len):
    hd = DN_HEAD_DIM
    c = CHUNK

    @pl.when(pl.program_id(2) == 0)
    def _():
        s_ref[...] = jnp.zeros_like(s_ref)

    onorm = onorm_ref[...]

    def step(ci, carry):
        r0 = pl.multiple_of(ci * c, c)
        egl = jnp.exp(gl_ref[0, 0, ci])
        heads = range(DN_REC_HEADS)
        cols = [slice(h * hd, (h + 1) * hd) for h in heads]
        st = [s_ref[h] for h in heads]
        r = [jnp.dot(jnp.concatenate([w_ref[0, pl.ds(r0, c), cols[h]], qd_ref[0, pl.ds(r0, c), cols[h]]], axis=0),
                     st[h].astype(BF16), preferred_element_type=F32) for h in heads]
        vb = [(u_ref[0, pl.ds(r0, c), cols[h]].astype(F32) - r[h][:c]).astype(BF16) for h in heads]
        upd = [lax.dot_general(kd_ref[0, pl.ds(r0, c), cols[h]], vb[h], (((0,), (0,)), ((), ())),
                               preferred_element_type=F32) for h in heads]
        o = [r[h][c:] + jnp.dot(a_ref[0, pl.ds(r0, c), h * c:(h + 1) * c], vb[h], preferred_element_type=F32)
             for h in heads]
        for h in heads:
            s_ref[h] = st[h] * egl[:, h:h + 1] + upd[h]
        for h in heads:
            y = o[h] * lax.rsqrt(jnp.mean(o[h] * o[h], axis=-1, keepdims=True) + EPS) * onorm
            zz = z_ref[0, pl.ds(r0, c), cols[h]]
            o_ref[0, pl.ds(r0, c), cols[h]] = (y * (zz * _sigmoid(zz))).astype(o_ref.dtype)
        return carry

    lax.fori_loop(0, t_len // c, step, 0)


def dn_rec(u, w, qd, kd, a, proj, gl, out_norm):
    b, t, _ = u.shape
    hd = DN_HEAD_DIM
    hb = DN_REC_HEADS
    ng = DN_V_HEADS // hb
    tt = min(DN_REC_ROWS, t)
    z_blk0 = DN_CONV_DIM // (hb * hd)
    wide = pl.BlockSpec((1, tt, hb * hd), lambda bi, g, ti: (bi, ti, g))
    return pl.pallas_call(
        functools.partial(_dn_rec_body, t_len=tt),
        grid=(b, ng, t // tt),
        in_specs=[
            wide, wide, wide, wide,
            pl.BlockSpec((1, tt, hb * CHUNK), lambda bi, g, ti: (bi, ti, g)),
            pl.BlockSpec((1, tt, hb * hd), lambda bi, g, ti: (bi, ti, z_blk0 + g)),
            pl.BlockSpec((1, 1, tt // CHUNK, 1, hb), lambda bi, g, ti: (bi, g, ti, 0, 0)),
            pl.BlockSpec((1, hd), lambda bi, g, ti: (0, 0)),
        ],
        out_specs=wide,
        out_shape=jax.ShapeDtypeStruct((b, t, DN_VAL_DIM), BF16),
        scratch_shapes=[pltpu.VMEM((hb, hd, hd), F32)],
        compiler_params=_params("parallel", "parallel", "arbitrary"),
        name="dn_rec",
    )(u, w, qd, kd, a, proj, gl, out_norm.reshape(1, hd))


def gated_deltanet_layer(h2, u, b, t, next_gain, w_in, conv_w, a_log, dt_bias, out_norm, w_out):
    n = b * t
    n_qkvz = DN_CONV_DIM + DN_VAL_DIM
    proj = matmul(u, w_in[:, :n_qkvz].astype(BF16), F32, tn=2048).reshape(b, t, n_qkvz)
    w_ba = jnp.pad(w_in[:, n_qkvz:], ((0, 0), (0, LANES - 2 * DN_V_HEADS))).astype(BF16)
    pba = matmul(u, w_ba, F32)
    pad = lambda p: jnp.pad(p.astype(F32), (DN_V_HEADS, LANES - 2 * DN_V_HEADS)).reshape(1, LANES)
    gates = dn_gates(pba, pad(a_log), pad(dt_bias))
    gc = gates[:, DN_V_HEADS:2 * DN_V_HEADS].reshape(b, t, DN_V_HEADS)
    gcr = gc.reshape(b, t, DN_QK_HEADS, 2).transpose(0, 2, 3, 1)
    ng = DN_V_HEADS // DN_REC_HEADS
    gl = gc[:, CHUNK - 1::CHUNK, :].reshape(b, t // CHUNK, ng, 1, DN_REC_HEADS).transpose(0, 2, 1, 3, 4)
    uu, ww, qd, kd, am = dn_prep(proj, conv_w, gates.reshape(b, t, LANES), gcr)
    o = dn_rec(uu, ww, qd, kd, am, proj, gl, out_norm)
    return matmul_res_norm(o.reshape(n, DN_VAL_DIM), w_out.astype(BF16), h2, next_gain, BF16)


def _dsa_prep_body(p_ref, qn_ref, kvn_ref, kin_ref, q_out, kv_out, ki_out, w_out):
    def nrm(x, g):
        return x * lax.rsqrt(jnp.mean(x * x, axis=-1, keepdims=True) + EPS) * g

    a, b2, c2 = Q_LORA, Q_LORA + KV_LORA, Q_LORA + KV_LORA + IDX_DIM
    q_out[...] = nrm(p_ref[:, 0:a], qn_ref[...]).astype(q_out.dtype)
    kv_out[...] = nrm(p_ref[:, a:b2], kvn_ref[...]).astype(kv_out.dtype)
    ki_out[...] = nrm(p_ref[:, b2:c2], kin_ref[...]).astype(ki_out.dtype)
    w_out[...] = p_ref[:, c2:c2 + LANES] * (IDX_HEADS ** -0.5 * IDX_DIM ** -0.5)


def dsa_prep(proj, q_norm, kv_norm, kidx_norm, tm=512):
    n, width = proj.shape
    row = lambda i: (i, 0)
    fix = lambda i: (0, 0)
    return pl.pallas_call(
        _dsa_prep_body,
        grid=(n // tm,),
        in_specs=[pl.BlockSpec((tm, width), row), pl.BlockSpec((1, Q_LORA), fix),
                  pl.BlockSpec((1, KV_LORA), fix), pl.BlockSpec((1, IDX_DIM), fix)],
        out_specs=[pl.BlockSpec((tm, Q_LORA), row), pl.BlockSpec((tm, KV_LORA), row),
                   pl.BlockSpec((tm, IDX_DIM), row), pl.BlockSpec((tm, LANES), row)],
        out_shape=[jax.ShapeDtypeStruct((n, Q_LORA), BF16), jax.ShapeDtypeStruct((n, KV_LORA), BF16),
                   jax.ShapeDtypeStruct((n, IDX_DIM), BF16), jax.ShapeDtypeStruct((n, LANES), F32)],
        compiler_params=_params("parallel"),
        name="dsa_prep",
    )(proj, q_norm.reshape(1, -1), kv_norm.reshape(1, -1), kidx_norm.reshape(1, -1))


def _qabs_body(q_ref, w_ref, o_ref, *, r):
    for h in range(DSA_HEADS):
        res = jnp.dot(q_ref[:, h * DSA_HEAD_DIM:(h + 1) * DSA_HEAD_DIM], w_ref[h],
                      preferred_element_type=F32) * (DSA_HEAD_DIM ** -0.5 * LOG2_E)
        for rr in range(r):
            o_ref[rr, h] = res[rr * QBLOCK:(rr + 1) * QBLOCK].astype(o_ref.dtype)


def q_absorb(q_all, w_uk, tm=256):
    n = q_all.shape[0]
    r = tm // QBLOCK
    hw = DSA_HEADS * DSA_HEAD_DIM
    return pl.pallas_call(
        functools.partial(_qabs_body, r=r),
        grid=(n // tm,),
        in_specs=[pl.BlockSpec((tm, hw), lambda i: (i, 0)),
                  pl.BlockSpec((DSA_HEADS, DSA_HEAD_DIM, KV_LORA), lambda i: (0, 0, 0))],
        out_specs=pl.BlockSpec((r, DSA_HEADS, QBLOCK, KV_LORA), lambda i: (i, 0, 0, 0)),
        out_shape=jax.ShapeDtypeStruct((n // QBLOCK, DSA_HEADS, QBLOCK, KV_LORA), BF16),
        compiler_params=_params("parallel"),
        name="q_absorb",
    )(q_all, w_uk)


def _vup_body(o_ref, w_ref, out_ref, *, r):
    for h in range(DSA_HEADS):
        for rr in range(r):
            out_ref[rr * QBLOCK:(rr + 1) * QBLOCK, h * DSA_HEAD_DIM:(h + 1) * DSA_HEAD_DIM] = jnp.dot(
                o_ref[rr, h], w_ref[h], preferred_element_type=F32).astype(out_ref.dtype)


def v_up(o_lat, w_uv, tm=256):
    nb = o_lat.shape[0]
    r = tm // QBLOCK
    hw = DSA_HEADS * DSA_HEAD_DIM
    return pl.pallas_call(
        functools.partial(_vup_body, r=r),
        grid=(nb // r,),
        in_specs=[pl.BlockSpec((r, DSA_HEADS, QBLOCK, KV_LORA), lambda i: (i, 0, 0, 0)),
                  pl.BlockSpec((DSA_HEADS, KV_LORA, DSA_HEAD_DIM), lambda i: (0, 0, 0))],
        out_specs=pl.BlockSpec((tm, hw), lambda i: (i, 0)),
        out_shape=jax.ShapeDtypeStruct((nb * QBLOCK, hw), BF16),
        compiler_params=_params("parallel"),
        name="v_up",
    )(o_lat, w_uv)


def _dsa_core_body(qidx_ref, widx_ref, kidx_ref, ckv_ref, qabs_ref, o_ref,
                   keys_ref, tau_ref, bias_ref, s_ref, p_ref, m_ref, l_ref, acc_ref, *, top_k, kt):
    i = pl.program_id(1)
    nq = QBLOCK
    nkt = keys_ref.shape[0]
    n_act = ((i + 1) * nq + kt - 1) // kt
    wt = widx_ref[0].T
    krow = lax.broadcasted_iota(jnp.int32, (kt, nq), 0)
    qcol = lax.broadcasted_iota(jnp.int32, (kt, nq), 1)
    q_chunk = (i * nq + qcol) // CHUNK
    tn = (((1,), (1,)), ((), ()))
    sub = 8

    def index_tile(jt, carry):
        k0 = pl.multiple_of(jt * kt, kt)
        kblk = kidx_ref[0, pl.ds(k0, kt), :]
        sc = jnp.zeros((kt, nq), F32)
        for h in range(IDX_HEADS):
            d = lax.dot_general(kblk, qidx_ref[0, :, h * IDX_DIM:(h + 1) * IDX_DIM], tn,
                                preferred_element_type=F32)
            sc = sc + jnp.maximum(d, 0.0) * wt[h:h + 1, :]
        sc = jnp.where(sc == 0.0, 0.0, sc)
        bits = pltpu.bitcast(sc, jnp.int32)
        key = bits ^ ((bits >> 31) & 0x7FFFFFFF)
        adm = ((k0 + krow) // CHUNK) <= q_chunk
        keys_ref[jt] = jnp.where(adm, key, INT_MIN)
        return carry

    lax.fori_loop(0, n_act, index_tile, 0)

    for jt in range(nkt):
        @pl.when(jt >= n_act)
        def _():
            keys_ref[jt] = jnp.full((kt, nq), INT_MIN, jnp.int32)

    def bisect_over(ntiles):
        def bisect(it, prefix):
            cand = prefix | lax.shift_left(jnp.int32(1), 31 - it)
            cand_s = cand ^ INT_MIN
            cnt = jnp.zeros((sub, nq), F32)
            for jt in range(ntiles):
                hit = jnp.where(keys_ref[jt] >= cand_s, 1.0, 0.0)
                cnt = cnt + jnp.sum(hit.reshape(kt // sub, sub, nq), axis=0)
            return jnp.where(jnp.sum(cnt, axis=0, keepdims=True) >= top_k, cand, prefix)

        prefix = lax.fori_loop(0, 32, bisect, jnp.zeros((1, nq), jnp.int32))
        tau_ref[...] = jnp.broadcast_to(jnp.maximum(prefix ^ INT_MIN, INT_MIN + 1), tau_ref.shape)

    step = 2 if nkt % 2 == 0 else 1
    for ntiles in range(step, nkt + 1, step):
        @pl.when((n_act > ntiles - step) & (n_act <= ntiles))
        def _():
            bisect_over(ntiles)

    tau = tau_ref[0:1, :]

    def tie_counts(jt, carry):
        gt, eq = carry
        kj = keys_ref[jt]
        gt = gt + jnp.sum(jnp.where(kj > tau, 1.0, 0.0).reshape(kt // sub, sub, nq), axis=0)
        eq = eq + jnp.sum(jnp.where(kj == tau, 1.0, 0.0).reshape(kt // sub, sub, nq), axis=0)
        return gt, eq

    zero8 = jnp.zeros((sub, nq), F32)
    gt8, eq8 = lax.fori_loop(0, n_act, tie_counts, (zero8, zero8))
    need = top_k - jnp.sum(gt8, axis=0, keepdims=True)
    surplus = jnp.sum(eq8, axis=0, keepdims=True) > need

    @pl.when(jnp.max(jnp.where(surplus, 1.0, 0.0)) > 0.0)
    def _():
        before_row = (lax.broadcasted_iota(jnp.int32, (kt, kt), 1)
                      < lax.broadcasted_iota(jnp.int32, (kt, kt), 0))
        tri = jnp.where(before_row, 1.0, 0.0).astype(BF16)

        def drop_surplus(jt, seen):
            kj = keys_ref[jt]
            eq = kj == tau
            eqf = jnp.where(eq, 1.0, 0.0)
            earlier = jnp.dot(tri, eqf.astype(BF16), preferred_element_type=F32) + seen
            keys_ref[jt] = jnp.where(eq & (earlier >= need), INT_MIN, kj)
            return seen + jnp.sum(eqf, axis=0, keepdims=True)

        lax.fori_loop(0, n_act, drop_surplus, jnp.zeros((1, nq), F32))

    rows = DSA_HEADS * nq
    rep = kt // LANES
    m_ref[...] = jnp.full(m_ref.shape, M_INIT, F32)
    l_ref[...] = jnp.zeros_like(l_ref)
    acc_ref[...] = jnp.zeros_like(acc_ref)

    def attend_tile(jt, carry):
        k0 = pl.multiple_of(jt * kt, kt)
        ck = ckv_ref[0, pl.ds(k0, kt), :]
        bias_ref[...] = jnp.where(keys_ref[jt] >= tau, 0.0, MASKED).T
        s_ref[...] = lax.dot_general(qabs_ref[0].reshape(rows, KV_LORA), ck, tn, preferred_element_type=F32)

        for h in range(DSA_HEADS):
            hr = slice(h * nq, (h + 1) * nq)
            s = s_ref[hr, :] + bias_ref[...]
            m_old = m_ref[hr, :]
            m_new = jnp.maximum(m_old, jnp.max(s, axis=1, keepdims=True))
            alpha = jnp.exp2(m_old - m_new)
            p = jnp.exp2(s - jnp.concatenate([m_new] * rep, axis=1))
            l_ref[hr, :] = alpha * l_ref[hr, :] + jnp.sum(p, axis=1, keepdims=True)
            m_ref[hr, :] = m_new
            acc_ref[hr, :] = acc_ref[hr, :] * jnp.concatenate([alpha] * (KV_LORA // LANES), axis=1)
            p_ref[hr, :] = p.astype(BF16)
        acc_ref[...] += jnp.dot(p_ref[...], ck, preferred_element_type=F32)
        return carry

    lax.fori_loop(0, n_act, attend_tile, 0)
    inv_l = 1.0 / l_ref[...]
    out = acc_ref[...] * jnp.concatenate([inv_l] * (KV_LORA // LANES), axis=1)
    o_ref[0] = out.reshape(DSA_HEADS, nq, KV_LORA).astype(o_ref.dtype)


def dsa_core(q_all, widx, kidx, ckv, qabs, b, t, kt=256):
    nblk = t // QBLOCK
    top_k = min(IDX_TOPK, t // 4)
    hw = IDX_HEADS * IDX_DIM
    rows = DSA_HEADS * QBLOCK
    return pl.pallas_call(
        functools.partial(_dsa_core_body, top_k=top_k, kt=kt),
        grid=(b, nblk),
        in_specs=[
            pl.BlockSpec((1, QBLOCK, hw), lambda bi, i: (bi, i, 1)),
            pl.BlockSpec((1, QBLOCK, LANES), lambda bi, i: (bi, i, 0)),
            pl.BlockSpec((1, t, IDX_DIM), lambda bi, i: (bi, 0, 0)),
            pl.BlockSpec((1, t, KV_LORA), lambda bi, i: (bi, 0, 0)),
            pl.BlockSpec((1, DSA_HEADS, QBLOCK, KV_LORA), lambda bi, i: (bi * nblk + i, 0, 0, 0)),
        ],
        out_specs=pl.BlockSpec((1, DSA_HEADS, QBLOCK, KV_LORA), lambda bi, i: (bi * nblk + i, 0, 0, 0)),
        out_shape=jax.ShapeDtypeStruct((b * nblk, DSA_HEADS, QBLOCK, KV_LORA), BF16),
        scratch_shapes=[pltpu.VMEM((t // kt, kt, QBLOCK), jnp.int32), pltpu.VMEM((8, QBLOCK), jnp.int32),
                        pltpu.VMEM((QBLOCK, kt), F32),
                        pltpu.VMEM((rows, kt), F32), pltpu.VMEM((rows, kt), BF16),
                        pltpu.VMEM((rows, LANES), F32), pltpu.VMEM((rows, LANES), F32),
                        pltpu.VMEM((rows, KV_LORA), F32)],
        compiler_params=_params("parallel", "arbitrary"),
        name="dsa_core",
    )(q_all.reshape(b, t, -1), widx.reshape(b, t, LANES), kidx.reshape(b, t, IDX_DIM),
      ckv.reshape(b, t, KV_LORA), qabs)


def dsa_layer(h2, u, b, t, next_gain, w_in, q_norm, kv_norm, kidx_norm, w_uq, w_uk, w_uv, w_out):
    width = Q_LORA + KV_LORA + IDX_DIM + LANES
    w_in_p = jnp.pad(w_in, ((0, 0), (0, width - w_in.shape[1]))).astype(BF16)
    proj = matmul(u, w_in_p, F32)
    qlat, ckv, kidx, widx = dsa_prep(proj, q_norm, kv_norm, kidx_norm)
    q_all = matmul(qlat, w_uq.astype(BF16), BF16)
    qabs = q_absorb(q_all, w_uk.astype(BF16))
    o_lat = dsa_core(q_all, widx, kidx, ckv, qabs, b, t)
    o = v_up(o_lat, w_uv.astype(BF16))
    return matmul_res_norm(o, w_out.astype(BF16), h2, next_gain, BF16)


def mlp(h2, u, next_gain, u_dtype, w_up, w_down):
    a = matmul(u, w_up.astype(BF16), BF16, relu2=True, tn=2048)
    return matmul_res_norm(a, w_down.astype(BF16), h2, next_gain, u_dtype)


def kernel(x, norm_mix, norm_mlp, norm_final, dn_w_in, dn_conv_w, dn_a_log, dn_dt_bias, dn_out_norm, dn_w_out, dsa_w_in, dsa_q_norm, dsa_kv_norm, dsa_kidx_norm, dsa_w_uq, dsa_w_uk, dsa_w_uv, dsa_w_out, mlp_w_up, mlp_w_down):
    b, t, d = x.shape
    depth = norm_mix.shape[0]
    h2 = x.reshape(b * t, d)
    u = rms_norm(h2, norm_mix[0], BF16)
    for i in range(depth):
        j = i // 2
        if i % 2 == 0:
            h2, u = gated_deltanet_layer(h2, u, b, t, norm_mlp[i], dn_w_in[j], dn_conv_w[j], dn_a_log[j],
                                         dn_dt_bias[j], dn_out_norm[j], dn_w_out[j])
        else:
            h2, u = dsa_layer(h2, u, b, t, norm_mlp[i], dsa_w_in[j], dsa_q_norm[j], dsa_kv_norm[j],
                              dsa_kidx_norm[j], dsa_w_uq[j], dsa_w_uk[j], dsa_w_uv[j], dsa_w_out[j])
        last = i == depth - 1
        h2, u = mlp(h2, u, norm_final if last else norm_mix[i + 1], x.dtype if last else BF16,
                    mlp_w_up[i], mlp_w_down[i])
    return u.reshape(b, t, d)
```

```python
import functools

import jax
import jax.numpy as jnp
from jax import lax
from jax.experimental import pallas as pl
from jax.experimental.pallas import tpu as pltpu

F32 = jnp.float32
BF16 = jnp.bfloat16
EPS = 1e-6
CHUNK = 64
QBLOCK = 128
DN_QK_HEADS = 16
DN_V_HEADS = 32
DN_HEAD_DIM = 128
DN_KEY_DIM = DN_QK_HEADS * DN_HEAD_DIM
DN_VAL_DIM = DN_V_HEADS * DN_HEAD_DIM
DN_CONV_DIM = 2 * DN_KEY_DIM + DN_VAL_DIM
DSA_HEADS = 16
DSA_HEAD_DIM = 128
Q_LORA = 512
KV_LORA = 256
IDX_HEADS = 16
IDX_DIM = 128
IDX_TOPK = 256
LANES = 128
VMEM_LIMIT_BYTES = 56 * 1024 * 1024
INT_MIN = -(2 ** 31)
HALF_MIN = -(2 ** 15)
MASKED = -1e30
M_INIT = -1e20
LOG2_E = 1.4426950408889634


def _params(*sem):
    return pltpu.CompilerParams(dimension_semantics=sem, vmem_limit_bytes=VMEM_LIMIT_BYTES)


def _sigmoid(x):
    return 1.0 / (1.0 + jnp.exp(-x))


def _norm_body(x_ref, g_ref, o_ref):
    x = x_ref[...]
    y = x * lax.rsqrt(jnp.mean(x * x, axis=-1, keepdims=True) + EPS) * g_ref[...]
    o_ref[...] = y.astype(o_ref.dtype)


def rms_norm(x, g, out_dtype, tm=512):
    n, d = x.shape
    return pl.pallas_call(
        _norm_body,
        grid=(n // tm,),
        in_specs=[pl.BlockSpec((tm, d), lambda i: (i, 0)), pl.BlockSpec((1, d), lambda i: (0, 0))],
        out_specs=pl.BlockSpec((tm, d), lambda i: (i, 0)),
        out_shape=jax.ShapeDtypeStruct((n, d), out_dtype),
        compiler_params=_params("parallel"),
        name="rms_norm",
    )(x, g.reshape(1, d))


MM_SUB_COLS = 512


def _mm_body(a_ref, w_ref, *rest, nk, relu2, residual):
    if residual:
        r_ref, o_ref = rest[0], rest[1]
    else:
        r_ref, o_ref = None, rest[0]
    acc_ref = rest[-1] if nk > 1 else None

    def finish(acc, cols=slice(None)):
        if relu2:
            acc = jnp.square(jnp.maximum(acc, 0.0))
        if residual:
            acc = acc + r_ref[:, cols]
        o_ref[:, cols] = acc.astype(o_ref.dtype)

    sub = min(MM_SUB_COLS, o_ref.shape[1])
    blocks = [slice(c0, c0 + sub) for c0 in range(0, o_ref.shape[1], sub)]

    def k_step(first, last):
        for cols in blocks:
            part = jnp.dot(a_ref[...], w_ref[:, cols], preferred_element_type=F32)
            if not first:
                part = part + acc_ref[:, cols]
            if last:
                finish(part, cols)
            else:
                acc_ref[:, cols] = part

    if nk == 1:
        k_step(True, True)
        return
    k = pl.program_id(2)
    pl.when(k == 0)(functools.partial(k_step, True, False))
    pl.when((k > 0) & (k < nk - 1))(functools.partial(k_step, False, False))
    pl.when(k == nk - 1)(functools.partial(k_step, False, True))


def matmul(a, w, out_dtype, *, relu2=False, residual=None, tm=1024, tn=1024, tk=2048):
    m, kdim = a.shape
    n = w.shape[1]
    tm, tn, tk = min(tm, m), min(tn, n), min(tk, kdim)
    assert m % tm == 0 and n % tn == 0 and kdim % tk == 0
    nk = kdim // tk
    in_specs = [pl.BlockSpec((tm, tk), lambda i, j, k: (i, k)), pl.BlockSpec((tk, tn), lambda i, j, k: (k, j))]
    args = [a, w]
    if residual is not None:
        in_specs.append(pl.BlockSpec((tm, tn), lambda i, j, k: (i, j)))
        args.append(residual)
    return pl.pallas_call(
        functools.partial(_mm_body, nk=nk, relu2=relu2, residual=residual is not None),
        grid=(m // tm, n // tn, nk),
        in_specs=in_specs,
        out_specs=pl.BlockSpec((tm, tn), lambda i, j, k: (i, j)),
        out_shape=jax.ShapeDtypeStruct((m, n), out_dtype),
        scratch_shapes=[pltpu.VMEM((tm, tn), F32)] if nk > 1 else [],
        compiler_params=_params("parallel", "parallel", "arbitrary"),
        name="matmul",
    )(*args)


def _mm_res_norm_body(a_ref, w_ref, r_ref, g_ref, h_ref, u_ref, *scratch, nk):
    acc_ref = scratch[0] if nk > 1 else None
    n = h_ref.shape[1]
    sub = min(MM_SUB_COLS, n)
    blocks = [slice(c0, c0 + sub) for c0 in range(0, n, sub)]

    def k_step(first, last):
        for cols in blocks:
            part = jnp.dot(a_ref[...], w_ref[:, cols], preferred_element_type=F32)
            if not first:
                part = part + acc_ref[:, cols]
            if last:
                h_ref[:, cols] = part + r_ref[:, cols]
            else:
                acc_ref[:, cols] = part
        if last:
            h = h_ref[...]
            scale = lax.rsqrt(jnp.mean(h * h, axis=-1, keepdims=True) + EPS)
            u_ref[...] = (h * scale * g_ref[...]).astype(u_ref.dtype)

    if nk == 1:
        k_step(True, True)
        return
    k = pl.program_id(1)
    pl.when(k == 0)(functools.partial(k_step, True, False))
    pl.when((k > 0) & (k < nk - 1))(functools.partial(k_step, False, False))
    pl.when(k == nk - 1)(functools.partial(k_step, False, True))


def matmul_res_norm(a, w, residual, gain, u_dtype, *, tm=512, tk=2048):
    m, kdim = a.shape
    n = w.shape[1]
    tm, tk = min(tm, m), min(tk, kdim)
    assert m % tm == 0 and kdim % tk == 0
    nk = kdim // tk
    row = pl.BlockSpec((tm, n), lambda i, k: (i, 0))
    return pl.pallas_call(
        functools.partial(_mm_res_norm_body, nk=nk),
        grid=(m // tm, nk),
        in_specs=[pl.BlockSpec((tm, tk), lambda i, k: (i, k)), pl.BlockSpec((tk, n), lambda i, k: (k, 0)),
                  row, pl.BlockSpec((1, n), lambda i, k: (0, 0))],
        out_specs=[row, row],
        out_shape=[jax.ShapeDtypeStruct((m, n), F32), jax.ShapeDtypeStruct((m, n), u_dtype)],
        scratch_shapes=[pltpu.VMEM((tm, n), F32)] if nk > 1 else [],
        compiler_params=_params("parallel", "arbitrary"),
        name="matmul_res_norm",
    )(a, w, residual, gain.reshape(1, n))


def _dn_gates_body(x_ref, alog_ref, dt_ref, o_ref):
    x = x_ref[...]
    g = -jnp.exp(alog_ref[...]) * (jnp.maximum(x + dt_ref[...], 0.0)
                                   + jnp.log(1.0 + jnp.exp(-jnp.abs(x + dt_ref[...]))))
    pos = lax.broadcasted_iota(jnp.int32, x.shape, 0) % CHUNK
    s = 1
    while s < CHUNK:
        g = g + jnp.where(pos >= s, pltpu.roll(g, s, axis=0), 0.0)
        s *= 2
    lane = lax.broadcasted_iota(jnp.int32, x.shape, 1)
    o_ref[...] = jnp.where(lane < DN_V_HEADS, _sigmoid(x), g)


def dn_gates(pba, alog_pad, dt_pad, tm=512):
    n = pba.shape[0]
    return pl.pallas_call(
        _dn_gates_body,
        grid=(n // tm,),
        in_specs=[pl.BlockSpec((tm, LANES), lambda i: (i, 0)),
                  pl.BlockSpec((1, LANES), lambda i: (0, 0)),
                  pl.BlockSpec((1, LANES), lambda i: (0, 0))],
        out_specs=pl.BlockSpec((tm, LANES), lambda i: (i, 0)),
        out_shape=jax.ShapeDtypeStruct((n, LANES), F32),
        compiler_params=_params("parallel"),
        name="dn_gates",
    )(pba, alog_pad, dt_pad)


def _conv_silu(ref, cw_ref, t0, tt):
    cw = cw_ref[...]
    acc = ref[0, t0:t0 + tt, :] * cw[3:4, :]
    for s in (1, 2, 3):
        if t0 == 0:
            x = ref[0, 0:tt, :]
            rows = lax.broadcasted_iota(jnp.int32, x.shape, 0)
            xs = jnp.where(rows >= s, pltpu.roll(x, s, axis=0), 0.0)
        else:
            xs = ref[0, t0 - s:t0 - s + tt, :]
        acc = acc + xs * cw[3 - s:4 - s, :]
    return acc * _sigmoid(acc)


def _l2norm(t):
    return t * lax.rsqrt(jnp.sum(t * t, axis=-1, keepdims=True) + EPS)


def _bmm(a, b):
    return jnp.einsum("nij,njk->nik", a.astype(BF16), b.astype(BF16), preferred_element_type=F32)


def _bmm_nt(a, b):
    return jnp.einsum("nid,njd->nij", a.astype(BF16), b.astype(BF16), preferred_element_type=F32)


DN_GROUP = 32


def _dn_prep_body(q_ref, k_ref, v_ref, cwq_ref, cwk_ref, cwv_ref, gates_ref, gcr_ref,
                  u_ref, w_ref, qd_ref, kd_ref, a_ref, qs_ref, ks_ref, vs_ref, bb_ref, gb_ref, *, t_len):
    j = pl.program_id(1)
    tt = 256
    hd = DN_HEAD_DIM
    for ti in range(t_len // tt):
        t0 = ti * tt
        qs_ref[t0:t0 + tt, :] = _l2norm(_conv_silu(q_ref, cwq_ref, t0, tt)) * (hd ** -0.5)
        ks_ref[t0:t0 + tt, :] = _l2norm(_conv_silu(k_ref, cwk_ref, t0, tt))
        vs_ref[t0:t0 + tt, :] = _conv_silu(v_ref, cwv_ref, t0, tt)
        gt = gates_ref[0, t0:t0 + tt, :]
        lane = lax.broadcasted_iota(jnp.int32, gt.shape, 1)
        for hh in range(2):
            col = 2 * j + hh
            bsel = jnp.sum(jnp.where(lane == col, gt, 0.0), axis=1, keepdims=True)
            gsel = jnp.sum(jnp.where(lane == col + DN_V_HEADS, gt, 0.0), axis=1, keepdims=True)
            bb_ref[hh, t0:t0 + tt, :] = jnp.broadcast_to(bsel, (tt, LANES))
            gb_ref[hh, t0:t0 + tt, :] = jnp.broadcast_to(gsel, (tt, LANES))

    c = CHUNK
    g = min(DN_GROUP, t_len // c)
    rows = g * c
    ri = lax.broadcasted_iota(jnp.int32, (c, c), 0)
    ci = lax.broadcasted_iota(jnp.int32, (c, c), 1)
    lower = ri >= ci
    strict = ri > ci
    eye = jnp.where(ri == ci, 1.0, 0.0)
    pair_mask = strict & ((ri // 2) == (ci // 2))
    merge_masks = [((ri // (2 * s)) == (ci // (2 * s))) & ((ri // s) != (ci // s)) & strict
                   for s in (2, 4, 8, 16, 32)]

    def group_step(gi, carry):
        base = pl.multiple_of(gi * rows, rows)
        kc = ks_ref[pl.ds(base, rows), :].reshape(g, c, hd)
        qc = qs_ref[pl.ds(base, rows), :].reshape(g, c, hd)
        kk1 = _bmm_nt(kc, kc)
        qk1 = _bmm_nt(qc, kc)
        two = lambda x: jnp.concatenate([x, x], axis=0)
        kk, qk, kc2, qc2 = two(kk1), two(qk1), two(kc), two(qc)
        beta_b = jnp.concatenate([bb_ref[hh, pl.ds(base, rows), :].reshape(g, c, LANES) for hh in range(2)], axis=0)
        gcc_b = jnp.concatenate([gb_ref[hh, pl.ds(base, rows), :].reshape(g, c, LANES) for hh in range(2)], axis=0)
        grow = gcr_ref[0, 0, :, pl.ds(base, rows)]
        gcr = jnp.stack([grow[hh:hh + 1, n * c:(n + 1) * c] for hh in range(2) for n in range(g)], axis=0)
        vc = jnp.concatenate([vs_ref[pl.ds(base, rows), hh * hd:(hh + 1) * hd].reshape(g, c, hd)
                              for hh in range(2)], axis=0)
        gl_b = gcc_b[:, c - 1:c, :]
        dm = gcc_b[:, :, :c] - gcr
        decay = jnp.where(lower, jnp.exp(jnp.where(lower, dm, 0.0)), 0.0)
        lm = jnp.where(strict, kk * beta_b[:, :, :c] * decay, 0.0)
        tm_ = eye - jnp.where(pair_mask, lm, 0.0)
        lmb = lm.astype(BF16)
        zero_b = jnp.zeros_like(lmb)
        for mk in merge_masks:
            tb = tm_.astype(BF16)
            tm_ = tm_ - _bmm(_bmm(tb, jnp.where(mk, lmb, zero_b)), tb)
        eg = jnp.exp(gcc_b)
        uw = _bmm(tm_, jnp.concatenate([vc * beta_b, kc2 * (beta_b * eg)], axis=2))
        qd = qc2 * eg
        kd = kc2 * jnp.exp(gl_b - gcc_b)
        am = qk * decay
        for hh in range(2):
            cols = slice(hh * hd, (hh + 1) * hd)
            inst = slice(hh * g, (hh + 1) * g)
            u_ref[0, pl.ds(base, rows), cols] = uw[inst, :, :hd].reshape(rows, hd).astype(u_ref.dtype)
            w_ref[0, pl.ds(base, rows), cols] = uw[inst, :, hd:].reshape(rows, hd).astype(w_ref.dtype)
            qd_ref[0, pl.ds(base, rows), cols] = qd[inst].reshape(rows, hd).astype(qd_ref.dtype)
            kd_ref[0, pl.ds(base, rows), cols] = kd[inst].reshape(rows, hd).astype(kd_ref.dtype)
            a_ref[0, pl.ds(base, rows), hh * c:(hh + 1) * c] = am[inst].reshape(rows, c).astype(a_ref.dtype)
        return carry

    lax.fori_loop(0, t_len // rows, group_step, 0)


def dn_prep(proj, conv_w, gates, gcr):
    b, t, _ = proj.shape
    hd = DN_HEAD_DIM
    nq = DN_QK_HEADS
    v_blk0 = 2 * DN_KEY_DIM // (2 * hd)
    wide = pl.BlockSpec((1, t, 2 * hd), lambda bi, j: (bi, 0, j))
    big = jax.ShapeDtypeStruct((b, t, DN_VAL_DIM), BF16)
    return pl.pallas_call(
        functools.partial(_dn_prep_body, t_len=t),
        grid=(b, nq),
        in_specs=[
            pl.BlockSpec((1, t, hd), lambda bi, j: (bi, 0, j)),
            pl.BlockSpec((1, t, hd), lambda bi, j: (bi, 0, nq + j)),
            pl.BlockSpec((1, t, 2 * hd), lambda bi, j: (bi, 0, v_blk0 + j)),
            pl.BlockSpec((4, hd), lambda bi, j: (0, j)),
            pl.BlockSpec((4, hd), lambda bi, j: (0, nq + j)),
            pl.BlockSpec((4, 2 * hd), lambda bi, j: (0, v_blk0 + j)),
            pl.BlockSpec((1, t, LANES), lambda bi, j: (bi, 0, 0)),
            pl.BlockSpec((1, 1, 2, t), lambda bi, j: (bi, j, 0, 0)),
        ],
        out_specs=[wide, wide, wide, wide, pl.BlockSpec((1, t, 2 * CHUNK), lambda bi, j: (bi, 0, j))],
        out_shape=[big, big, big, big, jax.ShapeDtypeStruct((b, t, DN_V_HEADS * CHUNK), BF16)],
        scratch_shapes=[
            pltpu.VMEM((t, hd), F32), pltpu.VMEM((t, hd), F32), pltpu.VMEM((t, 2 * hd), F32),
            pltpu.VMEM((2, t, LANES), F32), pltpu.VMEM((2, t, LANES), F32),
        ],
        compiler_params=_params("parallel", "arbitrary"),
        name="dn_prep",
    )(proj, proj, proj, conv_w, conv_w, conv_w, gates, gcr)


DN_REC_HEADS = 16
DN_REC_ROWS = 512


def _dn_rec_body(u_ref, w_ref, qd_ref, kd_ref, a_ref, z_ref, gl_ref, onorm_ref, o_ref, s_ref, *, t_len):
    hd = DN_HEAD_DIM
    c = CHUNK

    @pl.when(pl.program_id(2) == 0)
    def _():
        s_ref[...] = jnp.zeros_like(s_ref)

    onorm = onorm_ref[...]

    def step(ci, carry):
        r0 = pl.multiple_of(ci * c, c)
        egl = jnp.exp(gl_ref[0, 0, ci])
        heads = range(DN_REC_HEADS)
        cols = [slice(h * hd, (h + 1) * hd) for h in heads]
        st = [s_ref[h] for h in heads]
        r = [jnp.dot(jnp.concatenate([w_ref[0, pl.ds(r0, c), cols[h]], qd_ref[0, pl.ds(r0, c), cols[h]]], axis=0),
                     st[h].astype(BF16), preferred_element_type=F32) for h in heads]
        vb = [(u_ref[0, pl.ds(r0, c), cols[h]].astype(F32) - r[h][:c]).astype(BF16) for h in heads]
        upd = [lax.dot_general(kd_ref[0, pl.ds(r0, c), cols[h]], vb[h], (((0,), (0,)), ((), ())),
                               preferred_element_type=F32) for h in heads]
        o = [r[h][c:] + jnp.dot(a_ref[0, pl.ds(r0, c), h * c:(h + 1) * c], vb[h], preferred_element_type=F32)
             for h in heads]
        for h in heads:
            s_ref[h] = st[h] * egl[:, h:h + 1] + upd[h]
        for h in heads:
            y = o[h] * lax.rsqrt(jnp.mean(o[h] * o[h], axis=-1, keepdims=True) + EPS) * onorm
            zz = z_ref[0, pl.ds(r0, c), cols[h]]
            o_ref[0, pl.ds(r0, c), cols[h]] = (y * (zz * _sigmoid(zz))).astype(o_ref.dtype)
        return carry

    lax.fori_loop(0, t_len // c, step, 0)


def dn_rec(u, w, qd, kd, a, proj, gl, out_norm):
    b, t, _ = u.shape
    hd = DN_HEAD_DIM
    hb = DN_REC_HEADS
    ng = DN_V_HEADS // hb
    tt = min(DN_REC_ROWS, t)
    z_blk0 = DN_CONV_DIM // (hb * hd)
    wide = pl.BlockSpec((1, tt, hb * hd), lambda bi, g, ti: (bi, ti, g))
    return pl.pallas_call(
        functools.partial(_dn_rec_body, t_len=tt),
        grid=(b, ng, t // tt),
        in_specs=[
            wide, wide, wide, wide,
            pl.BlockSpec((1, tt, hb * CHUNK), lambda bi, g, ti: (bi, ti, g)),
            pl.BlockSpec((1, tt, hb * hd), lambda bi, g, ti: (bi, ti, z_blk0 + g)),
            pl.BlockSpec((1, 1, tt // CHUNK, 1, hb), lambda bi, g, ti: (bi, g, ti, 0, 0)),
            pl.BlockSpec((1, hd), lambda bi, g, ti: (0, 0)),
        ],
        out_specs=wide,
        out_shape=jax.ShapeDtypeStruct((b, t, DN_VAL_DIM), BF16),
        scratch_shapes=[pltpu.VMEM((hb, hd, hd), F32)],
        compiler_params=_params("parallel", "parallel", "arbitrary"),
        name="dn_rec",
    )(u, w, qd, kd, a, proj, gl, out_norm.reshape(1, hd))


def gated_deltanet_layer(h2, u, b, t, next_gain, w_in, conv_w, a_log, dt_bias, out_norm, w_out):
    n = b * t
    n_qkvz = DN_CONV_DIM + DN_VAL_DIM
    proj = matmul(u, w_in[:, :n_qkvz].astype(BF16), F32, tn=2048).reshape(b, t, n_qkvz)
    w_ba = jnp.pad(w_in[:, n_qkvz:], ((0, 0), (0, LANES - 2 * DN_V_HEADS))).astype(BF16)
    pba = matmul(u, w_ba, F32)
    pad = lambda p: jnp.pad(p.astype(F32), (DN_V_HEADS, LANES - 2 * DN_V_HEADS)).reshape(1, LANES)
    gates = dn_gates(pba, pad(a_log), pad(dt_bias))
    gc = gates[:, DN_V_HEADS:2 * DN_V_HEADS].reshape(b, t, DN_V_HEADS)
    gcr = gc.reshape(b, t, DN_QK_HEADS, 2).transpose(0, 2, 3, 1)
    ng = DN_V_HEADS // DN_REC_HEADS
    gl = gc[:, CHUNK - 1::CHUNK, :].reshape(b, t // CHUNK, ng, 1, DN_REC_HEADS).transpose(0, 2, 1, 3, 4)
    uu, ww, qd, kd, am = dn_prep(proj, conv_w, gates.reshape(b, t, LANES), gcr)
    o = dn_rec(uu, ww, qd, kd, am, proj, gl, out_norm)
    return matmul_res_norm(o.reshape(n, DN_VAL_DIM), w_out.astype(BF16), h2, next_gain, BF16)


def _dsa_prep_body(p_ref, qn_ref, kvn_ref, kin_ref, q_out, kv_out, ki_out, w_out):
    def nrm(x, g):
        return x * lax.rsqrt(jnp.mean(x * x, axis=-1, keepdims=True) + EPS) * g

    a, b2, c2 = Q_LORA, Q_LORA + KV_LORA, Q_LORA + KV_LORA + IDX_DIM
    q_out[...] = nrm(p_ref[:, 0:a], qn_ref[...]).astype(q_out.dtype)
    kv_out[...] = nrm(p_ref[:, a:b2], kvn_ref[...]).astype(kv_out.dtype)
    ki_out[...] = nrm(p_ref[:, b2:c2], kin_ref[...]).astype(ki_out.dtype)
    w_out[...] = p_ref[:, c2:c2 + LANES] * (IDX_HEADS ** -0.5 * IDX_DIM ** -0.5)


def dsa_prep(proj, q_norm, kv_norm, kidx_norm, tm=512):
    n, width = proj.shape
    row = lambda i: (i, 0)
    fix = lambda i: (0, 0)
    return pl.pallas_call(
        _dsa_prep_body,
        grid=(n // tm,),
        in_specs=[pl.BlockSpec((tm, width), row), pl.BlockSpec((1, Q_LORA), fix),
                  pl.BlockSpec((1, KV_LORA), fix), pl.BlockSpec((1, IDX_DIM), fix)],
        out_specs=[pl.BlockSpec((tm, Q_LORA), row), pl.BlockSpec((tm, KV_LORA), row),
                   pl.BlockSpec((tm, IDX_DIM), row), pl.BlockSpec((tm, LANES), row)],
        out_shape=[jax.ShapeDtypeStruct((n, Q_LORA), BF16), jax.ShapeDtypeStruct((n, KV_LORA), BF16),
                   jax.ShapeDtypeStruct((n, IDX_DIM), BF16), jax.ShapeDtypeStruct((n, LANES), F32)],
        compiler_params=_params("parallel"),
        name="dsa_prep",
    )(proj, q_norm.reshape(1, -1), kv_norm.reshape(1, -1), kidx_norm.reshape(1, -1))


def _qabs_body(q_ref, w_ref, o_ref, *, r):
    for h in range(DSA_HEADS):
        res = jnp.dot(q_ref[:, h * DSA_HEAD_DIM:(h + 1) * DSA_HEAD_DIM], w_ref[h],
                      preferred_element_type=F32) * (DSA_HEAD_DIM ** -0.5 * LOG2_E)
        for rr in range(r):
            o_ref[rr, h] = res[rr * QBLOCK:(rr + 1) * QBLOCK].astype(o_ref.dtype)


def q_absorb(q_all, w_uk, tm=256):
    n = q_all.shape[0]
    r = tm // QBLOCK
    hw = DSA_HEADS * DSA_HEAD_DIM
    return pl.pallas_call(
        functools.partial(_qabs_body, r=r),
        grid=(n // tm,),
        in_specs=[pl.BlockSpec((tm, hw), lambda i: (i, 0)),
                  pl.BlockSpec((DSA_HEADS, DSA_HEAD_DIM, KV_LORA), lambda i: (0, 0, 0))],
        out_specs=pl.BlockSpec((r, DSA_HEADS, QBLOCK, KV_LORA), lambda i: (i, 0, 0, 0)),
        out_shape=jax.ShapeDtypeStruct((n // QBLOCK, DSA_HEADS, QBLOCK, KV_LORA), BF16),
        compiler_params=_params("parallel"),
        name="q_absorb",
    )(q_all, w_uk)


def _vup_body(o_ref, w_ref, out_ref, *, r):
    for h in range(DSA_HEADS):
        for rr in range(r):
            out_ref[rr * QBLOCK:(rr + 1) * QBLOCK, h * DSA_HEAD_DIM:(h + 1) * DSA_HEAD_DIM] = jnp.dot(
                o_ref[rr, h], w_ref[h], preferred_element_type=F32).astype(out_ref.dtype)


def v_up(o_lat, w_uv, tm=256):
    nb = o_lat.shape[0]
    r = tm // QBLOCK
    hw = DSA_HEADS * DSA_HEAD_DIM
    return pl.pallas_call(
        functools.partial(_vup_body, r=r),
        grid=(nb // r,),
        in_specs=[pl.BlockSpec((r, DSA_HEADS, QBLOCK, KV_LORA), lambda i: (i, 0, 0, 0)),
                  pl.BlockSpec((DSA_HEADS, KV_LORA, DSA_HEAD_DIM), lambda i: (0, 0, 0))],
        out_specs=pl.BlockSpec((tm, hw), lambda i: (i, 0)),
        out_shape=jax.ShapeDtypeStruct((nb * QBLOCK, hw), BF16),
        compiler_params=_params("parallel"),
        name="v_up",
    )(o_lat, w_uv)


def _dsa_core_body(qidx_ref, widx_ref, kidx_ref, ckv_ref, qabs_ref, o_ref,
                   keys_ref, hi_ref, lo_ref, lom_ref, tau_ref, bias_ref, s_ref, p_ref, m_ref, l_ref, acc_ref,
                   *, top_k, kt):
    i = pl.program_id(1)
    nq = QBLOCK
    nkt = keys_ref.shape[0]
    n_act = ((i + 1) * nq + kt - 1) // kt
    wt = widx_ref[0].T
    krow = lax.broadcasted_iota(jnp.int32, (kt, nq), 0)
    qcol = lax.broadcasted_iota(jnp.int32, (kt, nq), 1)
    q_chunk = (i * nq + qcol) // CHUNK
    tn = (((1,), (1,)), ((), ()))
    sub = 8

    def index_tile(jt, carry):
        k0 = pl.multiple_of(jt * kt, kt)
        kblk = kidx_ref[0, pl.ds(k0, kt), :]
        sc = jnp.zeros((kt, nq), F32)
        for h in range(IDX_HEADS):
            d = lax.dot_general(kblk, qidx_ref[0, :, h * IDX_DIM:(h + 1) * IDX_DIM], tn,
                                preferred_element_type=F32)
            sc = sc + jnp.maximum(d, 0.0) * wt[h:h + 1, :]
        sc = jnp.where(sc == 0.0, 0.0, sc)
        bits = pltpu.bitcast(sc, jnp.int32)
        key = bits ^ ((bits >> 31) & 0x7FFFFFFF)
        adm = ((k0 + krow) // CHUNK) <= q_chunk
        key = jnp.where(adm, key, INT_MIN)
        keys_ref[jt] = key
        hi_ref[jt] = (key >> 16).astype(jnp.int16)
        lo_ref[jt] = ((key & 0xFFFF) + HALF_MIN).astype(jnp.int16)
        return carry

    lax.fori_loop(0, n_act, index_tile, 0)

    for jt in range(nkt):
        @pl.when(jt >= n_act)
        def _():
            keys_ref[jt] = jnp.full((kt, nq), INT_MIN, jnp.int32)
            hi_ref[jt] = jnp.full((kt, nq), HALF_MIN, jnp.int16)
            lo_ref[jt] = jnp.full((kt, nq), HALF_MIN, jnp.int16)

    one_b = jnp.ones((kt, nq), BF16)
    zero_b = jnp.zeros((kt, nq), BF16)
    pack = 16

    def count_hits(ref, ntiles, thr16, strict):
        parts = []
        for jt in range(ntiles):
            hit = (ref[jt] > thr16) if strict else (ref[jt] >= thr16)
            h3 = jnp.where(hit, one_b, zero_b).reshape(kt // pack, pack, nq)
            parts.extend(h3[q] for q in range(kt // pack))
        while len(parts) > 1:
            odd = parts[-1:] if len(parts) % 2 else []
            parts = [a + b for a, b in zip(parts[0::2], parts[1::2])] + odd
        return jnp.sum(parts[0].astype(F32), axis=0, keepdims=True)

    def bisect16(ref, ntiles, need):
        def step(it, prefix):
            cand = prefix | lax.shift_left(jnp.int32(1), 15 - it)
            cnt = count_hits(ref, ntiles, (cand + HALF_MIN).astype(jnp.int16), False)
            return jnp.where(cnt >= need, cand, prefix)

        return lax.fori_loop(0, 16, step, jnp.zeros((1, nq), jnp.int32))

    def select_over(ntiles):
        t_hi = bisect16(hi_ref, ntiles, jnp.full((1, nq), top_k, F32)) + HALF_MIN
        t_hi16 = t_hi.astype(jnp.int16)
        above = count_hits(hi_ref, ntiles, t_hi16, True)
        for jt in range(ntiles):
            lom_ref[jt] = jnp.where(hi_ref[jt] == t_hi16, lo_ref[jt], jnp.full((kt, nq), HALF_MIN, jnp.int16))
        p_lo = bisect16(lom_ref, ntiles, top_k - above)
        thr = lax.shift_left(t_hi, 16) | p_lo
        tau_ref[...] = jnp.broadcast_to(jnp.maximum(thr, INT_MIN + 1), tau_ref.shape)

    step = 2 if nkt % 2 == 0 else 1
    for ntiles in range(step, nkt + 1, step):
        @pl.when((n_act > ntiles - step) & (n_act <= ntiles))
        def _():
            select_over(ntiles)

    tau = tau_ref[0:1, :]

    def tie_counts(jt, carry):
        gt, eq = carry
        kj = keys_ref[jt]
        gt = gt + jnp.sum(jnp.where(kj > tau, 1.0, 0.0).reshape(kt // sub, sub, nq), axis=0)
        eq = eq + jnp.sum(jnp.where(kj == tau, 1.0, 0.0).reshape(kt // sub, sub, nq), axis=0)
        return gt, eq

    zero8 = jnp.zeros((sub, nq), F32)
    gt8, eq8 = lax.fori_loop(0, n_act, tie_counts, (zero8, zero8))
    need = top_k - jnp.sum(gt8, axis=0, keepdims=True)
    surplus = jnp.sum(eq8, axis=0, keepdims=True) > need

    @pl.when(jnp.max(jnp.where(surplus, 1.0, 0.0)) > 0.0)
    def _():
        before_row = (lax.broadcasted_iota(jnp.int32, (kt, kt), 1)
                      < lax.broadcasted_iota(jnp.int32, (kt, kt), 0))
        tri = jnp.where(before_row, 1.0, 0.0).astype(BF16)

        def drop_surplus(jt, seen):
            kj = keys_ref[jt]
            eq = kj == tau
            eqf = jnp.where(eq, 1.0, 0.0)
            earlier = jnp.dot(tri, eqf.astype(BF16), preferred_element_type=F32) + seen
            keys_ref[jt] = jnp.where(eq & (earlier >= need), INT_MIN, kj)
            return seen + jnp.sum(eqf, axis=0, keepdims=True)

        lax.fori_loop(0, n_act, drop_surplus, jnp.zeros((1, nq), F32))

    rows = DSA_HEADS * nq
    rep = kt // LANES
    m_ref[...] = jnp.full(m_ref.shape, M_INIT, F32)
    l_ref[...] = jnp.zeros_like(l_ref)
    acc_ref[...] = jnp.zeros_like(acc_ref)

    def attend_tile(jt, carry):
        k0 = pl.multiple_of(jt * kt, kt)
        ck = ckv_ref[0, pl.ds(k0, kt), :]
        bias_ref[...] = jnp.where(keys_ref[jt] >= tau, 0.0, MASKED).T
        s_ref[...] = lax.dot_general(qabs_ref[0].reshape(rows, KV_LORA), ck, tn, preferred_element_type=F32)

        for h in range(DSA_HEADS):
            hr = slice(h * nq, (h + 1) * nq)
            s = s_ref[hr, :] + bias_ref[...]
            m_old = m_ref[hr, :]
            m_new = jnp.maximum(m_old, jnp.max(s, axis=1, keepdims=True))
            alpha = jnp.exp2(m_old - m_new)
            p = jnp.exp2(s - jnp.concatenate([m_new] * rep, axis=1))
            l_ref[hr, :] = alpha * l_ref[hr, :] + jnp.sum(p, axis=1, keepdims=True)
            m_ref[hr, :] = m_new
            acc_ref[hr, :] = acc_ref[hr, :] * jnp.concatenate([alpha] * (KV_LORA // LANES), axis=1)
            p_ref[hr, :] = p.astype(BF16)
        acc_ref[...] += jnp.dot(p_ref[...], ck, preferred_element_type=F32)
        return carry

    lax.fori_loop(0, n_act, attend_tile, 0)
    inv_l = 1.0 / l_ref[...]
    out = acc_ref[...] * jnp.concatenate([inv_l] * (KV_LORA // LANES), axis=1)
    o_ref[0] = out.reshape(DSA_HEADS, nq, KV_LORA).astype(o_ref.dtype)


def dsa_core(q_all, widx, kidx, ckv, qabs, b, t, kt=256):
    nblk = t // QBLOCK
    top_k = min(IDX_TOPK, t // 4)
    hw = IDX_HEADS * IDX_DIM
    rows = DSA_HEADS * QBLOCK
    return pl.pallas_call(
        functools.partial(_dsa_core_body, top_k=top_k, kt=kt),
        grid=(b, nblk),
        in_specs=[
            pl.BlockSpec((1, QBLOCK, hw), lambda bi, i: (bi, i, 1)),
            pl.BlockSpec((1, QBLOCK, LANES), lambda bi, i: (bi, i, 0)),
            pl.BlockSpec((1, t, IDX_DIM), lambda bi, i: (bi, 0, 0)),
            pl.BlockSpec((1, t, KV_LORA), lambda bi, i: (bi, 0, 0)),
            pl.BlockSpec((1, DSA_HEADS, QBLOCK, KV_LORA), lambda bi, i: (bi * nblk + i, 0, 0, 0)),
        ],
        out_specs=pl.BlockSpec((1, DSA_HEADS, QBLOCK, KV_LORA), lambda bi, i: (bi * nblk + i, 0, 0, 0)),
        out_shape=jax.ShapeDtypeStruct((b * nblk, DSA_HEADS, QBLOCK, KV_LORA), BF16),
        scratch_shapes=[pltpu.VMEM((t // kt, kt, QBLOCK), jnp.int32)]
        + [pltpu.VMEM((t // kt, kt, QBLOCK), jnp.int16)] * 3
        + [pltpu.VMEM((8, QBLOCK), jnp.int32),
                        pltpu.VMEM((QBLOCK, kt), F32),
                        pltpu.VMEM((rows, kt), F32), pltpu.VMEM((rows, kt), BF16),
                        pltpu.VMEM((rows, LANES), F32), pltpu.VMEM((rows, LANES), F32),
                        pltpu.VMEM((rows, KV_LORA), F32)],
        compiler_params=_params("parallel", "arbitrary"),
        name="dsa_core",
    )(q_all.reshape(b, t, -1), widx.reshape(b, t, LANES), kidx.reshape(b, t, IDX_DIM),
      ckv.reshape(b, t, KV_LORA), qabs)


def dsa_layer(h2, u, b, t, next_gain, w_in, q_norm, kv_norm, kidx_norm, w_uq, w_uk, w_uv, w_out):
    width = Q_LORA + KV_LORA + IDX_DIM + LANES
    w_in_p = jnp.pad(w_in, ((0, 0), (0, width - w_in.shape[1]))).astype(BF16)
    proj = matmul(u, w_in_p, F32)
    qlat, ckv, kidx, widx = dsa_prep(proj, q_norm, kv_norm, kidx_norm)
    q_all = matmul(qlat, w_uq.astype(BF16), BF16)
    qabs = q_absorb(q_all, w_uk.astype(BF16))
    o_lat = dsa_core(q_all, widx, kidx, ckv, qabs, b, t)
    o = v_up(o_lat, w_uv.astype(BF16))
    return matmul_res_norm(o, w_out.astype(BF16), h2, next_gain, BF16)


def mlp(h2, u, next_gain, u_dtype, w_up, w_down):
    a = matmul(u, w_up.astype(BF16), BF16, relu2=True, tn=2048)
    return matmul_res_norm(a, w_down.astype(BF16), h2, next_gain, u_dtype)


def kernel(x, norm_mix, norm_mlp, norm_final, dn_w_in, dn_conv_w, dn_a_log, dn_dt_bias, dn_out_norm, dn_w_out, dsa_w_in, dsa_q_norm, dsa_kv_norm, dsa_kidx_norm, dsa_w_uq, dsa_w_uk, dsa_w_uv, dsa_w_out, mlp_w_up, mlp_w_down):
    b, t, d = x.shape
    depth = norm_mix.shape[0]
    h2 = x.reshape(b * t, d)
    u = rms_norm(h2, norm_mix[0], BF16)
    for i in range(depth):
        j = i // 2
        if i % 2 == 0:
            h2, u = gated_deltanet_layer(h2, u, b, t, norm_mlp[i], dn_w_in[j], dn_conv_w[j], dn_a_log[j],
                                         dn_dt_bias[j], dn_out_norm[j], dn_w_out[j])
        else:
            h2, u = dsa_layer(h2, u, b, t, norm_mlp[i], dsa_w_in[j], dsa_q_norm[j], dsa_kv_norm[j],
                              dsa_kidx_norm[j], dsa_w_uq[j], dsa_w_uk[j], dsa_w_uv[j], dsa_w_out[j])
        last = i == depth - 1
        h2, u = mlp(h2, u, norm_final if last else norm_mix[i + 1], x.dtype if last else BF16,
                    mlp_w_up[i], mlp_w_down[i])
    return u.reshape(b, t, d)
```

```python
import functools

import jax
import jax.numpy as jnp
from jax import lax
from jax.experimental import pallas as pl
from jax.experimental.pallas import tpu as pltpu

F32 = jnp.float32
BF16 = jnp.bfloat16
EPS = 1e-6
CHUNK = 64
QBLOCK = 128
DN_QK_HEADS = 16
DN_V_HEADS = 32
DN_HEAD_DIM = 128
DN_KEY_DIM = DN_QK_HEADS * DN_HEAD_DIM
DN_VAL_DIM = DN_V_HEADS * DN_HEAD_DIM
DN_CONV_DIM = 2 * DN_KEY_DIM + DN_VAL_DIM
DSA_HEADS = 16
DSA_HEAD_DIM = 128
Q_LORA = 512
KV_LORA = 256
IDX_HEADS = 16
IDX_DIM = 128
IDX_TOPK = 256
LANES = 128
VMEM_LIMIT_BYTES = 56 * 1024 * 1024
INT_MIN = -(2 ** 31)
MASKED = -1e30
M_INIT = -1e20
LOG2_E = 1.4426950408889634


def _params(*sem):
    return pltpu.CompilerParams(dimension_semantics=sem, vmem_limit_bytes=VMEM_LIMIT_BYTES)


def _sigmoid(x):
    return 1.0 / (1.0 + jnp.exp(-x))


def _norm_body(x_ref, g_ref, o_ref):
    x = x_ref[...]
    y = x * lax.rsqrt(jnp.mean(x * x, axis=-1, keepdims=True) + EPS) * g_ref[...]
    o_ref[...] = y.astype(o_ref.dtype)


def rms_norm(x, g, out_dtype, tm=512):
    n, d = x.shape
    return pl.pallas_call(
        _norm_body,
        grid=(n // tm,),
        in_specs=[pl.BlockSpec((tm, d), lambda i: (i, 0)), pl.BlockSpec((1, d), lambda i: (0, 0))],
        out_specs=pl.BlockSpec((tm, d), lambda i: (i, 0)),
        out_shape=jax.ShapeDtypeStruct((n, d), out_dtype),
        compiler_params=_params("parallel"),
        name="rms_norm",
    )(x, g.reshape(1, d))


MM_SUB_COLS = 512


def _mm_body(a_ref, w_ref, *rest, nk, relu2, residual):
    if residual:
        r_ref, o_ref = rest[0], rest[1]
    else:
        r_ref, o_ref = None, rest[0]
    acc_ref = rest[-1] if nk > 1 else None

    def finish(acc, cols=slice(None)):
        if relu2:
            acc = jnp.square(jnp.maximum(acc, 0.0))
        if residual:
            acc = acc + r_ref[:, cols]
        o_ref[:, cols] = acc.astype(o_ref.dtype)

    sub = min(MM_SUB_COLS, o_ref.shape[1])
    blocks = [slice(c0, c0 + sub) for c0 in range(0, o_ref.shape[1], sub)]

    def k_step(first, last):
        for cols in blocks:
            part = jnp.dot(a_ref[...], w_ref[:, cols], preferred_element_type=F32)
            if not first:
                part = part + acc_ref[:, cols]
            if last:
                finish(part, cols)
            else:
                acc_ref[:, cols] = part

    if nk == 1:
        k_step(True, True)
        return
    k = pl.program_id(2)
    pl.when(k == 0)(functools.partial(k_step, True, False))
    pl.when((k > 0) & (k < nk - 1))(functools.partial(k_step, False, False))
    pl.when(k == nk - 1)(functools.partial(k_step, False, True))


def matmul(a, w, out_dtype, *, relu2=False, residual=None, tm=1024, tn=1024, tk=2048):
    m, kdim = a.shape
    n = w.shape[1]
    tm, tn, tk = min(tm, m), min(tn, n), min(tk, kdim)
    assert m % tm == 0 and n % tn == 0 and kdim % tk == 0
    nk = kdim // tk
    in_specs = [pl.BlockSpec((tm, tk), lambda i, j, k: (i, k)), pl.BlockSpec((tk, tn), lambda i, j, k: (k, j))]
    args = [a, w]
    if residual is not None:
        in_specs.append(pl.BlockSpec((tm, tn), lambda i, j, k: (i, j)))
        args.append(residual)
    return pl.pallas_call(
        functools.partial(_mm_body, nk=nk, relu2=relu2, residual=residual is not None),
        grid=(m // tm, n // tn, nk),
        in_specs=in_specs,
        out_specs=pl.BlockSpec((tm, tn), lambda i, j, k: (i, j)),
        out_shape=jax.ShapeDtypeStruct((m, n), out_dtype),
        scratch_shapes=[pltpu.VMEM((tm, tn), F32)] if nk > 1 else [],
        compiler_params=_params("parallel", "parallel", "arbitrary"),
        name="matmul",
    )(*args)


def _mm_res_norm_body(a_ref, w_ref, r_ref, g_ref, h_ref, u_ref, *, nk):
    n = h_ref.shape[1]
    sub = min(MM_SUB_COLS, n)
    blocks = [slice(c0, c0 + sub) for c0 in range(0, n, sub)]

    def k_step(first, last):
        for cols in blocks:
            part = jnp.dot(a_ref[...], w_ref[:, cols], preferred_element_type=F32)
            h_ref[:, cols] = part + (r_ref[:, cols] if first else h_ref[:, cols])
        if last:
            h = h_ref[...]
            scale = lax.rsqrt(jnp.mean(h * h, axis=-1, keepdims=True) + EPS)
            u_ref[...] = (h * scale * g_ref[...]).astype(u_ref.dtype)

    if nk == 1:
        k_step(True, True)
        return
    k = pl.program_id(1)
    pl.when(k == 0)(functools.partial(k_step, True, False))
    pl.when((k > 0) & (k < nk - 1))(functools.partial(k_step, False, False))
    pl.when(k == nk - 1)(functools.partial(k_step, False, True))


def matmul_res_norm(a, w, residual, gain, u_dtype, *, tm=1024, tk=1024):
    m, kdim = a.shape
    n = w.shape[1]
    if jnp.dtype(u_dtype).itemsize > 2:
        tm //= 2
    tm, tk = min(tm, m), min(tk, kdim)
    assert m % tm == 0 and kdim % tk == 0
    nk = kdim // tk
    row = pl.BlockSpec((tm, n), lambda i, k: (i, 0))
    return pl.pallas_call(
        functools.partial(_mm_res_norm_body, nk=nk),
        grid=(m // tm, nk),
        in_specs=[pl.BlockSpec((tm, tk), lambda i, k: (i, k)), pl.BlockSpec((tk, n), lambda i, k: (k, 0)),
                  row, pl.BlockSpec((1, n), lambda i, k: (0, 0))],
        out_specs=[row, row],
        out_shape=[jax.ShapeDtypeStruct((m, n), F32), jax.ShapeDtypeStruct((m, n), u_dtype)],
        compiler_params=_params("parallel", "arbitrary"),
        name="matmul_res_norm",
    )(a, w, residual, gain.reshape(1, n))


def _dn_gates_body(x_ref, alog_ref, dt_ref, o_ref):
    x = x_ref[...]
    g = -jnp.exp(alog_ref[...]) * (jnp.maximum(x + dt_ref[...], 0.0)
                                   + jnp.log(1.0 + jnp.exp(-jnp.abs(x + dt_ref[...]))))
    pos = lax.broadcasted_iota(jnp.int32, x.shape, 0) % CHUNK
    s = 1
    while s < CHUNK:
        g = g + jnp.where(pos >= s, pltpu.roll(g, s, axis=0), 0.0)
        s *= 2
    lane = lax.broadcasted_iota(jnp.int32, x.shape, 1)
    o_ref[...] = jnp.where(lane < DN_V_HEADS, _sigmoid(x), g)


def dn_gates(pba, alog_pad, dt_pad, tm=512):
    n = pba.shape[0]
    return pl.pallas_call(
        _dn_gates_body,
        grid=(n // tm,),
        in_specs=[pl.BlockSpec((tm, LANES), lambda i: (i, 0)),
                  pl.BlockSpec((1, LANES), lambda i: (0, 0)),
                  pl.BlockSpec((1, LANES), lambda i: (0, 0))],
        out_specs=pl.BlockSpec((tm, LANES), lambda i: (i, 0)),
        out_shape=jax.ShapeDtypeStruct((n, LANES), F32),
        compiler_params=_params("parallel"),
        name="dn_gates",
    )(pba, alog_pad, dt_pad)


def _conv_silu(ref, cw_ref, t0, tt):
    cw = cw_ref[...]
    acc = ref[0, t0:t0 + tt, :] * cw[3:4, :]
    for s in (1, 2, 3):
        if t0 == 0:
            x = ref[0, 0:tt, :]
            rows = lax.broadcasted_iota(jnp.int32, x.shape, 0)
            xs = jnp.where(rows >= s, pltpu.roll(x, s, axis=0), 0.0)
        else:
            xs = ref[0, t0 - s:t0 - s + tt, :]
        acc = acc + xs * cw[3 - s:4 - s, :]
    return acc * _sigmoid(acc)


def _l2norm(t):
    return t * lax.rsqrt(jnp.sum(t * t, axis=-1, keepdims=True) + EPS)


def _bmm(a, b):
    return jnp.einsum("nij,njk->nik", a.astype(BF16), b.astype(BF16), preferred_element_type=F32)


def _bmm_nt(a, b):
    return jnp.einsum("nid,njd->nij", a.astype(BF16), b.astype(BF16), preferred_element_type=F32)


DN_GROUP = 32


def _dn_prep_body(q_ref, k_ref, v_ref, cwq_ref, cwk_ref, cwv_ref, gates_ref, gcr_ref,
                  u_ref, w_ref, qd_ref, kd_ref, a_ref, qs_ref, ks_ref, vs_ref, bb_ref, gb_ref, *, t_len):
    j = pl.program_id(1)
    tt = 256
    hd = DN_HEAD_DIM
    for ti in range(t_len // tt):
        t0 = ti * tt
        qs_ref[t0:t0 + tt, :] = _l2norm(_conv_silu(q_ref, cwq_ref, t0, tt)) * (hd ** -0.5)
        ks_ref[t0:t0 + tt, :] = _l2norm(_conv_silu(k_ref, cwk_ref, t0, tt))
        vs_ref[t0:t0 + tt, :] = _conv_silu(v_ref, cwv_ref, t0, tt)
        gt = gates_ref[0, t0:t0 + tt, :]
        lane = lax.broadcasted_iota(jnp.int32, gt.shape, 1)
        for hh in range(2):
            col = 2 * j + hh
            bsel = jnp.sum(jnp.where(lane == col, gt, 0.0), axis=1, keepdims=True)
            gsel = jnp.sum(jnp.where(lane == col + DN_V_HEADS, gt, 0.0), axis=1, keepdims=True)
            bb_ref[hh, t0:t0 + tt, :] = jnp.broadcast_to(bsel, (tt, LANES))
            gb_ref[hh, t0:t0 + tt, :] = jnp.broadcast_to(gsel, (tt, LANES))

    c = CHUNK
    g = min(DN_GROUP, t_len // c)
    rows = g * c
    ri = lax.broadcasted_iota(jnp.int32, (c, c), 0)
    ci = lax.broadcasted_iota(jnp.int32, (c, c), 1)
    lower = ri >= ci
    strict = ri > ci
    eye = jnp.where(ri == ci, 1.0, 0.0)
    pair_mask = strict & ((ri // 2) == (ci // 2))
    merge_masks = [((ri // (2 * s)) == (ci // (2 * s))) & ((ri // s) != (ci // s)) & strict
                   for s in (2, 4, 8, 16, 32)]

    def group_step(gi, carry):
        base = pl.multiple_of(gi * rows, rows)
        kc = ks_ref[pl.ds(base, rows), :].reshape(g, c, hd)
        qc = qs_ref[pl.ds(base, rows), :].reshape(g, c, hd)
        kk1 = _bmm_nt(kc, kc)
        qk1 = _bmm_nt(qc, kc)
        two = lambda x: jnp.concatenate([x, x], axis=0)
        kk, qk, kc2, qc2 = two(kk1), two(qk1), two(kc), two(qc)
        beta_b = jnp.concatenate([bb_ref[hh, pl.ds(base, rows), :].reshape(g, c, LANES) for hh in range(2)], axis=0)
        gcc_b = jnp.concatenate([gb_ref[hh, pl.ds(base, rows), :].reshape(g, c, LANES) for hh in range(2)], axis=0)
        grow = gcr_ref[0, 0, :, pl.ds(base, rows)]
        gcr = jnp.stack([grow[hh:hh + 1, n * c:(n + 1) * c] for hh in range(2) for n in range(g)], axis=0)
        vc = jnp.concatenate([vs_ref[pl.ds(base, rows), hh * hd:(hh + 1) * hd].reshape(g, c, hd)
                              for hh in range(2)], axis=0)
        gl_b = gcc_b[:, c - 1:c, :]
        dm = gcc_b[:, :, :c] - gcr
        decay = jnp.where(lower, jnp.exp(jnp.where(lower, dm, 0.0)), 0.0)
        lm = jnp.where(strict, kk * beta_b[:, :, :c] * decay, 0.0)
        tm_ = eye - jnp.where(pair_mask, lm, 0.0)
        lmb = lm.astype(BF16)
        zero_b = jnp.zeros_like(lmb)
        for mk in merge_masks:
            tb = tm_.astype(BF16)
            tm_ = tm_ - _bmm(_bmm(tb, jnp.where(mk, lmb, zero_b)), tb)
        eg = jnp.exp(gcc_b)
        uw = _bmm(tm_, jnp.concatenate([vc * beta_b, kc2 * (beta_b * eg)], axis=2))
        qd = qc2 * eg
        kd = kc2 * jnp.exp(gl_b - gcc_b)
        am = qk * decay
        for hh in range(2):
            cols = slice(hh * hd, (hh + 1) * hd)
            inst = slice(hh * g, (hh + 1) * g)
            u_ref[0, pl.ds(base, rows), cols] = uw[inst, :, :hd].reshape(rows, hd).astype(u_ref.dtype)
            w_ref[0, pl.ds(base, rows), cols] = uw[inst, :, hd:].reshape(rows, hd).astype(w_ref.dtype)
            qd_ref[0, pl.ds(base, rows), cols] = qd[inst].reshape(rows, hd).astype(qd_ref.dtype)
            kd_ref[0, pl.ds(base, rows), cols] = kd[inst].reshape(rows, hd).astype(kd_ref.dtype)
            a_ref[0, pl.ds(base, rows), hh * c:(hh + 1) * c] = am[inst].reshape(rows, c).astype(a_ref.dtype)
        return carry

    lax.fori_loop(0, t_len // rows, group_step, 0)


def dn_prep(proj, conv_w, gates, gcr):
    b, t, _ = proj.shape
    hd = DN_HEAD_DIM
    nq = DN_QK_HEADS
    v_blk0 = 2 * DN_KEY_DIM // (2 * hd)
    wide = pl.BlockSpec((1, t, 2 * hd), lambda bi, j: (bi, 0, j))
    big = jax.ShapeDtypeStruct((b, t, DN_VAL_DIM), BF16)
    return pl.pallas_call(
        functools.partial(_dn_prep_body, t_len=t),
        grid=(b, nq),
        in_specs=[
            pl.BlockSpec((1, t, hd), lambda bi, j: (bi, 0, j)),
            pl.BlockSpec((1, t, hd), lambda bi, j: (bi, 0, nq + j)),
            pl.BlockSpec((1, t, 2 * hd), lambda bi, j: (bi, 0, v_blk0 + j)),
            pl.BlockSpec((4, hd), lambda bi, j: (0, j)),
            pl.BlockSpec((4, hd), lambda bi, j: (0, nq + j)),
            pl.BlockSpec((4, 2 * hd), lambda bi, j: (0, v_blk0 + j)),
            pl.BlockSpec((1, t, LANES), lambda bi, j: (bi, 0, 0)),
            pl.BlockSpec((1, 1, 2, t), lambda bi, j: (bi, j, 0, 0)),
        ],
        out_specs=[wide, wide, wide, wide, pl.BlockSpec((1, t, 2 * CHUNK), lambda bi, j: (bi, 0, j))],
        out_shape=[big, big, big, big, jax.ShapeDtypeStruct((b, t, DN_V_HEADS * CHUNK), BF16)],
        scratch_shapes=[
            pltpu.VMEM((t, hd), F32), pltpu.VMEM((t, hd), F32), pltpu.VMEM((t, 2 * hd), F32),
            pltpu.VMEM((2, t, LANES), F32), pltpu.VMEM((2, t, LANES), F32),
        ],
        compiler_params=_params("parallel", "arbitrary"),
        name="dn_prep",
    )(proj, proj, proj, conv_w, conv_w, conv_w, gates, gcr)


DN_REC_HEADS = 16
DN_REC_ROWS = 512


def _dn_rec_body(u_ref, w_ref, qd_ref, kd_ref, a_ref, z_ref, gl_ref, onorm_ref, o_ref, s_ref, *, t_len):
    hd = DN_HEAD_DIM
    c = CHUNK

    @pl.when(pl.program_id(2) == 0)
    def _():
        s_ref[...] = jnp.zeros_like(s_ref)

    onorm = onorm_ref[...]

    def step(ci, carry):
        r0 = pl.multiple_of(ci * c, c)
        egl = jnp.exp(gl_ref[0, 0, ci])
        heads = range(DN_REC_HEADS)
        cols = [slice(h * hd, (h + 1) * hd) for h in heads]
        st = [s_ref[h] for h in heads]
        r = [jnp.dot(jnp.concatenate([w_ref[0, pl.ds(r0, c), cols[h]], qd_ref[0, pl.ds(r0, c), cols[h]]], axis=0),
                     st[h].astype(BF16), preferred_element_type=F32) for h in heads]
        vb = [(u_ref[0, pl.ds(r0, c), cols[h]].astype(F32) - r[h][:c]).astype(BF16) for h in heads]
        upd = [lax.dot_general(kd_ref[0, pl.ds(r0, c), cols[h]], vb[h], (((0,), (0,)), ((), ())),
                               preferred_element_type=F32) for h in heads]
        o = [r[h][c:] + jnp.dot(a_ref[0, pl.ds(r0, c), h * c:(h + 1) * c], vb[h], preferred_element_type=F32)
             for h in heads]
        for h in heads:
            s_ref[h] = st[h] * egl[:, h:h + 1] + upd[h]
        for h in heads:
            y = o[h] * lax.rsqrt(jnp.mean(o[h] * o[h], axis=-1, keepdims=True) + EPS) * onorm
            zz = z_ref[0, pl.ds(r0, c), cols[h]]
            o_ref[0, pl.ds(r0, c), cols[h]] = (y * (zz * _sigmoid(zz))).astype(o_ref.dtype)
        return carry

    lax.fori_loop(0, t_len // c, step, 0)


def dn_rec(u, w, qd, kd, a, proj, gl, out_norm):
    b, t, _ = u.shape
    hd = DN_HEAD_DIM
    hb = DN_REC_HEADS
    ng = DN_V_HEADS // hb
    tt = min(DN_REC_ROWS, t)
    z_blk0 = DN_CONV_DIM // (hb * hd)
    wide = pl.BlockSpec((1, tt, hb * hd), lambda bi, g, ti: (bi, ti, g))
    return pl.pallas_call(
        functools.partial(_dn_rec_body, t_len=tt),
        grid=(b, ng, t // tt),
        in_specs=[
            wide, wide, wide, wide,
            pl.BlockSpec((1, tt, hb * CHUNK), lambda bi, g, ti: (bi, ti, g)),
            pl.BlockSpec((1, tt, hb * hd), lambda bi, g, ti: (bi, ti, z_blk0 + g)),
            pl.BlockSpec((1, 1, tt // CHUNK, 1, hb), lambda bi, g, ti: (bi, g, ti, 0, 0)),
            pl.BlockSpec((1, hd), lambda bi, g, ti: (0, 0)),
        ],
        out_specs=wide,
        out_shape=jax.ShapeDtypeStruct((b, t, DN_VAL_DIM), BF16),
        scratch_shapes=[pltpu.VMEM((hb, hd, hd), F32)],
        compiler_params=_params("parallel", "parallel", "arbitrary"),
        name="dn_rec",
    )(u, w, qd, kd, a, proj, gl, out_norm.reshape(1, hd))


def gated_deltanet_layer(h2, u, b, t, next_gain, w_in, conv_w, a_log, dt_bias, out_norm, w_out):
    n = b * t
    n_qkvz = DN_CONV_DIM + DN_VAL_DIM
    proj = matmul(u, w_in[:, :n_qkvz].astype(BF16), F32, tm=2048, tn=1024).reshape(b, t, n_qkvz)
    w_ba = jnp.pad(w_in[:, n_qkvz:], ((0, 0), (0, LANES - 2 * DN_V_HEADS))).astype(BF16)
    pba = matmul(u, w_ba, F32)
    pad = lambda p: jnp.pad(p.astype(F32), (DN_V_HEADS, LANES - 2 * DN_V_HEADS)).reshape(1, LANES)
    gates = dn_gates(pba, pad(a_log), pad(dt_bias))
    gc = gates[:, DN_V_HEADS:2 * DN_V_HEADS].reshape(b, t, DN_V_HEADS)
    gcr = gc.reshape(b, t, DN_QK_HEADS, 2).transpose(0, 2, 3, 1)
    ng = DN_V_HEADS // DN_REC_HEADS
    gl = gc[:, CHUNK - 1::CHUNK, :].reshape(b, t // CHUNK, ng, 1, DN_REC_HEADS).transpose(0, 2, 1, 3, 4)
    uu, ww, qd, kd, am = dn_prep(proj, conv_w, gates.reshape(b, t, LANES), gcr)
    o = dn_rec(uu, ww, qd, kd, am, proj, gl, out_norm)
    return matmul_res_norm(o.reshape(n, DN_VAL_DIM), w_out.astype(BF16), h2, next_gain, BF16)


def _dsa_prep_body(p_ref, qn_ref, kvn_ref, kin_ref, q_out, kv_out, ki_out, w_out):
    def nrm(x, g):
        return x * lax.rsqrt(jnp.mean(x * x, axis=-1, keepdims=True) + EPS) * g

    a, b2, c2 = Q_LORA, Q_LORA + KV_LORA, Q_LORA + KV_LORA + IDX_DIM
    q_out[...] = nrm(p_ref[:, 0:a], qn_ref[...]).astype(q_out.dtype)
    kv_out[...] = nrm(p_ref[:, a:b2], kvn_ref[...]).astype(kv_out.dtype)
    ki_out[...] = nrm(p_ref[:, b2:c2], kin_ref[...]).astype(ki_out.dtype)
    w_out[...] = p_ref[:, c2:c2 + LANES] * (IDX_HEADS ** -0.5 * IDX_DIM ** -0.5)


def dsa_prep(proj, q_norm, kv_norm, kidx_norm, tm=512):
    n, width = proj.shape
    row = lambda i: (i, 0)
    fix = lambda i: (0, 0)
    return pl.pallas_call(
        _dsa_prep_body,
        grid=(n // tm,),
        in_specs=[pl.BlockSpec((tm, width), row), pl.BlockSpec((1, Q_LORA), fix),
                  pl.BlockSpec((1, KV_LORA), fix), pl.BlockSpec((1, IDX_DIM), fix)],
        out_specs=[pl.BlockSpec((tm, Q_LORA), row), pl.BlockSpec((tm, KV_LORA), row),
                   pl.BlockSpec((tm, IDX_DIM), row), pl.BlockSpec((tm, LANES), row)],
        out_shape=[jax.ShapeDtypeStruct((n, Q_LORA), BF16), jax.ShapeDtypeStruct((n, KV_LORA), BF16),
                   jax.ShapeDtypeStruct((n, IDX_DIM), BF16), jax.ShapeDtypeStruct((n, LANES), F32)],
        compiler_params=_params("parallel"),
        name="dsa_prep",
    )(proj, q_norm.reshape(1, -1), kv_norm.reshape(1, -1), kidx_norm.reshape(1, -1))


def _qabs_body(q_ref, w_ref, o_ref, *, r):
    for h in range(DSA_HEADS):
        res = jnp.dot(q_ref[:, h * DSA_HEAD_DIM:(h + 1) * DSA_HEAD_DIM], w_ref[h],
                      preferred_element_type=F32) * (DSA_HEAD_DIM ** -0.5 * LOG2_E)
        for rr in range(r):
            o_ref[rr, h] = res[rr * QBLOCK:(rr + 1) * QBLOCK].astype(o_ref.dtype)


def q_absorb(q_all, w_uk, tm=256):
    n = q_all.shape[0]
    r = tm // QBLOCK
    hw = DSA_HEADS * DSA_HEAD_DIM
    return pl.pallas_call(
        functools.partial(_qabs_body, r=r),
        grid=(n // tm,),
        in_specs=[pl.BlockSpec((tm, hw), lambda i: (i, 0)),
                  pl.BlockSpec((DSA_HEADS, DSA_HEAD_DIM, KV_LORA), lambda i: (0, 0, 0))],
        out_specs=pl.BlockSpec((r, DSA_HEADS, QBLOCK, KV_LORA), lambda i: (i, 0, 0, 0)),
        out_shape=jax.ShapeDtypeStruct((n // QBLOCK, DSA_HEADS, QBLOCK, KV_LORA), BF16),
        compiler_params=_params("parallel"),
        name="q_absorb",
    )(q_all, w_uk)


def _vup_body(o_ref, w_ref, out_ref, *, r):
    for h in range(DSA_HEADS):
        for rr in range(r):
            out_ref[rr * QBLOCK:(rr + 1) * QBLOCK, h * DSA_HEAD_DIM:(h + 1) * DSA_HEAD_DIM] = jnp.dot(
                o_ref[rr, h], w_ref[h], preferred_element_type=F32).astype(out_ref.dtype)


def v_up(o_lat, w_uv, tm=256):
    nb = o_lat.shape[0]
    r = tm // QBLOCK
    hw = DSA_HEADS * DSA_HEAD_DIM
    return pl.pallas_call(
        functools.partial(_vup_body, r=r),
        grid=(nb // r,),
        in_specs=[pl.BlockSpec((r, DSA_HEADS, QBLOCK, KV_LORA), lambda i: (i, 0, 0, 0)),
                  pl.BlockSpec((DSA_HEADS, KV_LORA, DSA_HEAD_DIM), lambda i: (0, 0, 0))],
        out_specs=pl.BlockSpec((tm, hw), lambda i: (i, 0)),
        out_shape=jax.ShapeDtypeStruct((nb * QBLOCK, hw), BF16),
        compiler_params=_params("parallel"),
        name="v_up",
    )(o_lat, w_uv)


def _dsa_core_body(qidx_ref, widx_ref, kidx_ref, ckv_ref, qabs_ref, o_ref,
                   keys_ref, tau_ref, bias_ref, s_ref, p_ref, m_ref, l_ref, acc_ref, *, top_k, kt):
    i = pl.program_id(1)
    nq = QBLOCK
    nkt = keys_ref.shape[0]
    n_act = ((i + 1) * nq + kt - 1) // kt
    wt = widx_ref[0].T
    krow = lax.broadcasted_iota(jnp.int32, (kt, nq), 0)
    qcol = lax.broadcasted_iota(jnp.int32, (kt, nq), 1)
    q_chunk = (i * nq + qcol) // CHUNK
    tn = (((1,), (1,)), ((), ()))
    sub = 8

    def index_tile(jt, carry):
        k0 = pl.multiple_of(jt * kt, kt)
        kblk = kidx_ref[0, pl.ds(k0, kt), :]
        sc = jnp.zeros((kt, nq), F32)
        for h in range(IDX_HEADS):
            d = lax.dot_general(kblk, qidx_ref[0, :, h * IDX_DIM:(h + 1) * IDX_DIM], tn,
                                preferred_element_type=F32)
            sc = sc + jnp.maximum(d, 0.0) * wt[h:h + 1, :]
        sc = jnp.where(sc == 0.0, 0.0, sc)
        bits = pltpu.bitcast(sc, jnp.int32)
        key = bits ^ ((bits >> 31) & 0x7FFFFFFF)
        adm = ((k0 + krow) // CHUNK) <= q_chunk
        keys_ref[jt] = jnp.where(adm, key, INT_MIN)
        return carry

    lax.fori_loop(0, n_act, index_tile, 0)

    for jt in range(nkt):
        @pl.when(jt >= n_act)
        def _():
            keys_ref[jt] = jnp.full((kt, nq), INT_MIN, jnp.int32)

    def bisect_over(ntiles):
        def bisect(it, prefix):
            cand = prefix | lax.shift_left(jnp.int32(1), 31 - it)
            cand_s = cand ^ INT_MIN
            cnt = jnp.zeros((sub, nq), F32)
            for jt in range(ntiles):
                hit = jnp.where(keys_ref[jt] >= cand_s, 1.0, 0.0)
                cnt = cnt + jnp.sum(hit.reshape(kt // sub, sub, nq), axis=0)
            return jnp.where(jnp.sum(cnt, axis=0, keepdims=True) >= top_k, cand, prefix)

        prefix = lax.fori_loop(0, 32, bisect, jnp.zeros((1, nq), jnp.int32))
        tau_ref[...] = jnp.broadcast_to(jnp.maximum(prefix ^ INT_MIN, INT_MIN + 1), tau_ref.shape)

    step = 2 if nkt % 2 == 0 else 1
    for ntiles in range(step, nkt + 1, step):
        @pl.when((n_act > ntiles - step) & (n_act <= ntiles))
        def _():
            bisect_over(ntiles)

    tau = tau_ref[0:1, :]

    def tie_counts(jt, carry):
        gt, eq = carry
        kj = keys_ref[jt]
        gt = gt + jnp.sum(jnp.where(kj > tau, 1.0, 0.0).reshape(kt // sub, sub, nq), axis=0)
        eq = eq + jnp.sum(jnp.where(kj == tau, 1.0, 0.0).reshape(kt // sub, sub, nq), axis=0)
        return gt, eq

    zero8 = jnp.zeros((sub, nq), F32)
    gt8, eq8 = lax.fori_loop(0, n_act, tie_counts, (zero8, zero8))
    need = top_k - jnp.sum(gt8, axis=0, keepdims=True)
    surplus = jnp.sum(eq8, axis=0, keepdims=True) > need

    @pl.when(jnp.max(jnp.where(surplus, 1.0, 0.0)) > 0.0)
    def _():
        before_row = (lax.broadcasted_iota(jnp.int32, (kt, kt), 1)
                      < lax.broadcasted_iota(jnp.int32, (kt, kt), 0))
        tri = jnp.where(before_row, 1.0, 0.0).astype(BF16)

        def drop_surplus(jt, seen):
            kj = keys_ref[jt]
            eq = kj == tau
            eqf = jnp.where(eq, 1.0, 0.0)
            earlier = jnp.dot(tri, eqf.astype(BF16), preferred_element_type=F32) + seen
            keys_ref[jt] = jnp.where(eq & (earlier >= need), INT_MIN, kj)
            return seen + jnp.sum(eqf, axis=0, keepdims=True)

        lax.fori_loop(0, n_act, drop_surplus, jnp.zeros((1, nq), F32))

    rows = DSA_HEADS * nq
    rep = kt // LANES
    m_ref[...] = jnp.full(m_ref.shape, M_INIT, F32)
    l_ref[...] = jnp.zeros_like(l_ref)
    acc_ref[...] = jnp.zeros_like(acc_ref)

    def attend_tile(jt, carry):
        k0 = pl.multiple_of(jt * kt, kt)
        ck = ckv_ref[0, pl.ds(k0, kt), :]
        bias_ref[...] = jnp.where(keys_ref[jt] >= tau, 0.0, MASKED).T
        s_ref[...] = lax.dot_general(qabs_ref[0].reshape(rows, KV_LORA), ck, tn, preferred_element_type=F32)

        for h in range(DSA_HEADS):
            hr = slice(h * nq, (h + 1) * nq)
            s = s_ref[hr, :] + bias_ref[...]
            m_old = m_ref[hr, :]
            m_new = jnp.maximum(m_old, jnp.max(s, axis=1, keepdims=True))
            alpha = jnp.exp2(m_old - m_new)
            p = jnp.exp2(s - jnp.concatenate([m_new] * rep, axis=1))
            l_ref[hr, :] = alpha * l_ref[hr, :] + jnp.sum(p, axis=1, keepdims=True)
            m_ref[hr, :] = m_new
            acc_ref[hr, :] = acc_ref[hr, :] * jnp.concatenate([alpha] * (KV_LORA // LANES), axis=1)
            p_ref[hr, :] = p.astype(BF16)
        acc_ref[...] += jnp.dot(p_ref[...], ck, preferred_element_type=F32)
        return carry

    lax.fori_loop(0, n_act, attend_tile, 0)
    inv_l = 1.0 / l_ref[...]
    out = acc_ref[...] * jnp.concatenate([inv_l] * (KV_LORA // LANES), axis=1)
    o_ref[0] = out.reshape(DSA_HEADS, nq, KV_LORA).astype(o_ref.dtype)


def dsa_core(q_all, widx, kidx, ckv, qabs, b, t, kt=256):
    nblk = t // QBLOCK
    top_k = min(IDX_TOPK, t // 4)
    hw = IDX_HEADS * IDX_DIM
    rows = DSA_HEADS * QBLOCK
    return pl.pallas_call(
        functools.partial(_dsa_core_body, top_k=top_k, kt=kt),
        grid=(b, nblk),
        in_specs=[
            pl.BlockSpec((1, QBLOCK, hw), lambda bi, i: (bi, i, 1)),
            pl.BlockSpec((1, QBLOCK, LANES), lambda bi, i: (bi, i, 0)),
            pl.BlockSpec((1, t, IDX_DIM), lambda bi, i: (bi, 0, 0)),
            pl.BlockSpec((1, t, KV_LORA), lambda bi, i: (bi, 0, 0)),
            pl.BlockSpec((1, DSA_HEADS, QBLOCK, KV_LORA), lambda bi, i: (bi * nblk + i, 0, 0, 0)),
        ],
        out_specs=pl.BlockSpec((1, DSA_HEADS, QBLOCK, KV_LORA), lambda bi, i: (bi * nblk + i, 0, 0, 0)),
        out_shape=jax.ShapeDtypeStruct((b * nblk, DSA_HEADS, QBLOCK, KV_LORA), BF16),
        scratch_shapes=[pltpu.VMEM((t // kt, kt, QBLOCK), jnp.int32), pltpu.VMEM((8, QBLOCK), jnp.int32),
                        pltpu.VMEM((QBLOCK, kt), F32),
                        pltpu.VMEM((rows, kt), F32), pltpu.VMEM((rows, kt), BF16),
                        pltpu.VMEM((rows, LANES), F32), pltpu.VMEM((rows, LANES), F32),
                        pltpu.VMEM((rows, KV_LORA), F32)],
        compiler_params=_params("parallel", "arbitrary"),
        name="dsa_core",
    )(q_all.reshape(b, t, -1), widx.reshape(b, t, LANES), kidx.reshape(b, t, IDX_DIM),
      ckv.reshape(b, t, KV_LORA), qabs)


def dsa_layer(h2, u, b, t, next_gain, w_in, q_norm, kv_norm, kidx_norm, w_uq, w_uk, w_uv, w_out):
    width = Q_LORA + KV_LORA + IDX_DIM + LANES
    w_in_p = jnp.pad(w_in, ((0, 0), (0, width - w_in.shape[1]))).astype(BF16)
    proj = matmul(u, w_in_p, F32)
    qlat, ckv, kidx, widx = dsa_prep(proj, q_norm, kv_norm, kidx_norm)
    q_all = matmul(qlat, w_uq.astype(BF16), BF16)
    qabs = q_absorb(q_all, w_uk.astype(BF16))
    o_lat = dsa_core(q_all, widx, kidx, ckv, qabs, b, t)
    o = v_up(o_lat, w_uv.astype(BF16))
    return matmul_res_norm(o, w_out.astype(BF16), h2, next_gain, BF16)


def mlp(h2, u, next_gain, u_dtype, w_up, w_down):
    a = matmul(u, w_up.astype(BF16), BF16, relu2=True, tm=2048, tn=2048)
    return matmul_res_norm(a, w_down.astype(BF16), h2, next_gain, u_dtype)


def kernel(x, norm_mix, norm_mlp, norm_final, dn_w_in, dn_conv_w, dn_a_log, dn_dt_bias, dn_out_norm, dn_w_out, dsa_w_in, dsa_q_norm, dsa_kv_norm, dsa_kidx_norm, dsa_w_uq, dsa_w_uk, dsa_w_uv, dsa_w_out, mlp_w_up, mlp_w_down):
    b, t, d = x.shape
    depth = norm_mix.shape[0]
    h2 = x.reshape(b * t, d)
    u = rms_norm(h2, norm_mix[0], BF16)
    for i in range(depth):
        j = i // 2
        if i % 2 == 0:
            h2, u = gated_deltanet_layer(h2, u, b, t, norm_mlp[i], dn_w_in[j], dn_conv_w[j], dn_a_log[j],
                                         dn_dt_bias[j], dn_out_norm[j], dn_w_out[j])
        else:
            h2, u = dsa_layer(h2, u, b, t, norm_mlp[i], dsa_w_in[j], dsa_q_norm[j], dsa_kv_norm[j],
                              dsa_kidx_norm[j], dsa_w_uq[j], dsa_w_uk[j], dsa_w_uv[j], dsa_w_out[j])
        last = i == depth - 1
        h2, u = mlp(h2, u, norm_final if last else norm_mix[i + 1], x.dtype if last else BF16,
                    mlp_w_up[i], mlp_w_down[i])
    return u.reshape(b, t, d)
```

```python
import functools

import jax
import jax.numpy as jnp
from jax import lax
from jax.experimental import pallas as pl
from jax.experimental.pallas import tpu as pltpu

F32 = jnp.float32
BF16 = jnp.bfloat16
EPS = 1e-6
CHUNK = 64
QBLOCK = 128
DN_QK_HEADS = 16
DN_V_HEADS = 32
DN_HEAD_DIM = 128
DN_KEY_DIM = DN_QK_HEADS * DN_HEAD_DIM
DN_VAL_DIM = DN_V_HEADS * DN_HEAD_DIM
DN_CONV_DIM = 2 * DN_KEY_DIM + DN_VAL_DIM
DSA_HEADS = 16
DSA_HEAD_DIM = 128
Q_LORA = 512
KV_LORA = 256
IDX_HEADS = 16
IDX_DIM = 128
IDX_TOPK = 256
LANES = 128
VMEM_LIMIT_BYTES = 56 * 1024 * 1024
INT_MIN = -(2 ** 31)
MASKED = -1e30
M_INIT = -1e20
LOG2_E = 1.4426950408889634


def _params(*sem):
    return pltpu.CompilerParams(dimension_semantics=sem, vmem_limit_bytes=VMEM_LIMIT_BYTES)


def _sigmoid(x):
    return 1.0 / (1.0 + jnp.exp(-x))


def _norm_body(x_ref, g_ref, o_ref):
    x = x_ref[...]
    y = x * lax.rsqrt(jnp.mean(x * x, axis=-1, keepdims=True) + EPS) * g_ref[...]
    o_ref[...] = y.astype(o_ref.dtype)


def rms_norm(x, g, out_dtype, tm=512):
    n, d = x.shape
    return pl.pallas_call(
        _norm_body,
        grid=(n // tm,),
        in_specs=[pl.BlockSpec((tm, d), lambda i: (i, 0)), pl.BlockSpec((1, d), lambda i: (0, 0))],
        out_specs=pl.BlockSpec((tm, d), lambda i: (i, 0)),
        out_shape=jax.ShapeDtypeStruct((n, d), out_dtype),
        compiler_params=_params("parallel"),
        name="rms_norm",
    )(x, g.reshape(1, d))


MM_SUB_COLS = 512


def _mm_body(a_ref, w_ref, *rest, nk, relu2, residual):
    if residual:
        r_ref, o_ref = rest[0], rest[1]
    else:
        r_ref, o_ref = None, rest[0]
    acc_ref = rest[-1] if nk > 1 else None

    def finish(acc, cols=slice(None)):
        if relu2:
            acc = jnp.square(jnp.maximum(acc, 0.0))
        if residual:
            acc = acc + r_ref[:, cols]
        o_ref[:, cols] = acc.astype(o_ref.dtype)

    sub = min(MM_SUB_COLS, o_ref.shape[1])
    blocks = [slice(c0, c0 + sub) for c0 in range(0, o_ref.shape[1], sub)]

    def k_step(first, last):
        for cols in blocks:
            part = jnp.dot(a_ref[...], w_ref[:, cols], preferred_element_type=F32)
            if not first:
                part = part + acc_ref[:, cols]
            if last:
                finish(part, cols)
            else:
                acc_ref[:, cols] = part

    if nk == 1:
        k_step(True, True)
        return
    k = pl.program_id(2)
    pl.when(k == 0)(functools.partial(k_step, True, False))
    pl.when((k > 0) & (k < nk - 1))(functools.partial(k_step, False, False))
    pl.when(k == nk - 1)(functools.partial(k_step, False, True))


def matmul(a, w, out_dtype, *, relu2=False, residual=None, tm=1024, tn=1024, tk=2048):
    m, kdim = a.shape
    n = w.shape[1]
    tm, tn, tk = min(tm, m), min(tn, n), min(tk, kdim)
    assert m % tm == 0 and n % tn == 0 and kdim % tk == 0
    nk = kdim // tk
    in_specs = [pl.BlockSpec((tm, tk), lambda i, j, k: (i, k)), pl.BlockSpec((tk, tn), lambda i, j, k: (k, j))]
    args = [a, w]
    if residual is not None:
        in_specs.append(pl.BlockSpec((tm, tn), lambda i, j, k: (i, j)))
        args.append(residual)
    return pl.pallas_call(
        functools.partial(_mm_body, nk=nk, relu2=relu2, residual=residual is not None),
        grid=(m // tm, n // tn, nk),
        in_specs=in_specs,
        out_specs=pl.BlockSpec((tm, tn), lambda i, j, k: (i, j)),
        out_shape=jax.ShapeDtypeStruct((m, n), out_dtype),
        scratch_shapes=[pltpu.VMEM((tm, tn), F32)] if nk > 1 else [],
        compiler_params=_params("parallel", "parallel", "arbitrary"),
        name="matmul",
    )(*args)


def _mm_res_norm_body(a_ref, w_ref, r_ref, g_ref, h_ref, u_ref, *scratch, nk):
    acc_ref = scratch[0] if nk > 1 else None
    n = h_ref.shape[1]
    sub = min(MM_SUB_COLS, n)
    blocks = [slice(c0, c0 + sub) for c0 in range(0, n, sub)]

    def k_step(first, last):
        for cols in blocks:
            part = jnp.dot(a_ref[...], w_ref[:, cols], preferred_element_type=F32)
            if not first:
                part = part + acc_ref[:, cols]
            if last:
                h_ref[:, cols] = part + r_ref[:, cols]
            else:
                acc_ref[:, cols] = part
        if last:
            h = h_ref[...]
            scale = lax.rsqrt(jnp.mean(h * h, axis=-1, keepdims=True) + EPS)
            u_ref[...] = (h * scale * g_ref[...]).astype(u_ref.dtype)

    if nk == 1:
        k_step(True, True)
        return
    k = pl.program_id(1)
    pl.when(k == 0)(functools.partial(k_step, True, False))
    pl.when((k > 0) & (k < nk - 1))(functools.partial(k_step, False, False))
    pl.when(k == nk - 1)(functools.partial(k_step, False, True))


def matmul_res_norm(a, w, residual, gain, u_dtype, *, tm=512, tk=2048):
    m, kdim = a.shape
    n = w.shape[1]
    tm, tk = min(tm, m), min(tk, kdim)
    assert m % tm == 0 and kdim % tk == 0
    nk = kdim // tk
    row = pl.BlockSpec((tm, n), lambda i, k: (i, 0))
    return pl.pallas_call(
        functools.partial(_mm_res_norm_body, nk=nk),
        grid=(m // tm, nk),
        in_specs=[pl.BlockSpec((tm, tk), lambda i, k: (i, k)), pl.BlockSpec((tk, n), lambda i, k: (k, 0)),
                  row, pl.BlockSpec((1, n), lambda i, k: (0, 0))],
        out_specs=[row, row],
        out_shape=[jax.ShapeDtypeStruct((m, n), F32), jax.ShapeDtypeStruct((m, n), u_dtype)],
        scratch_shapes=[pltpu.VMEM((tm, n), F32)] if nk > 1 else [],
        compiler_params=_params("parallel", "arbitrary"),
        name="matmul_res_norm",
    )(a, w, residual, gain.reshape(1, n))


def _dn_gates_body(x_ref, alog_ref, dt_ref, o_ref):
    x = x_ref[...]
    g = -jnp.exp(alog_ref[...]) * (jnp.maximum(x + dt_ref[...], 0.0)
                                   + jnp.log(1.0 + jnp.exp(-jnp.abs(x + dt_ref[...]))))
    pos = lax.broadcasted_iota(jnp.int32, x.shape, 0) % CHUNK
    s = 1
    while s < CHUNK:
        g = g + jnp.where(pos >= s, pltpu.roll(g, s, axis=0), 0.0)
        s *= 2
    lane = lax.broadcasted_iota(jnp.int32, x.shape, 1)
    o_ref[...] = jnp.where(lane < DN_V_HEADS, _sigmoid(x), g)


def dn_gates(pba, alog_pad, dt_pad, tm=512):
    n = pba.shape[0]
    return pl.pallas_call(
        _dn_gates_body,
        grid=(n // tm,),
        in_specs=[pl.BlockSpec((tm, LANES), lambda i: (i, 0)),
                  pl.BlockSpec((1, LANES), lambda i: (0, 0)),
                  pl.BlockSpec((1, LANES), lambda i: (0, 0))],
        out_specs=pl.BlockSpec((tm, LANES), lambda i: (i, 0)),
        out_shape=jax.ShapeDtypeStruct((n, LANES), F32),
        compiler_params=_params("parallel"),
        name="dn_gates",
    )(pba, alog_pad, dt_pad)


def _conv_silu(ref, cw_ref, t0, tt):
    cw = cw_ref[...]
    acc = ref[0, t0:t0 + tt, :] * cw[3:4, :]
    for s in (1, 2, 3):
        if t0 == 0:
            x = ref[0, 0:tt, :]
            rows = lax.broadcasted_iota(jnp.int32, x.shape, 0)
            xs = jnp.where(rows >= s, pltpu.roll(x, s, axis=0), 0.0)
        else:
            xs = ref[0, t0 - s:t0 - s + tt, :]
        acc = acc + xs * cw[3 - s:4 - s, :]
    return acc * _sigmoid(acc)


def _l2norm(t):
    return t * lax.rsqrt(jnp.sum(t * t, axis=-1, keepdims=True) + EPS)


def _bmm(a, b):
    return jnp.einsum("nij,njk->nik", a.astype(BF16), b.astype(BF16), preferred_element_type=F32)


def _bmm_nt(a, b):
    return jnp.einsum("nid,njd->nij", a.astype(BF16), b.astype(BF16), preferred_element_type=F32)


DN_GROUP = 32


def _dn_prep_body(q_ref, k_ref, v_ref, cwq_ref, cwk_ref, cwv_ref, gates_ref, gcr_ref,
                  u_ref, w_ref, qd_ref, kd_ref, a_ref, qs_ref, ks_ref, vs_ref, bb_ref, gb_ref, *, t_len):
    j = pl.program_id(1)
    tt = 256
    hd = DN_HEAD_DIM
    for ti in range(t_len // tt):
        t0 = ti * tt
        qs_ref[t0:t0 + tt, :] = _l2norm(_conv_silu(q_ref, cwq_ref, t0, tt)) * (hd ** -0.5)
        ks_ref[t0:t0 + tt, :] = _l2norm(_conv_silu(k_ref, cwk_ref, t0, tt))
        vs_ref[t0:t0 + tt, :] = _conv_silu(v_ref, cwv_ref, t0, tt)
        gt = gates_ref[0, t0:t0 + tt, :]
        lane = lax.broadcasted_iota(jnp.int32, gt.shape, 1)
        for hh in range(2):
            col = 2 * j + hh
            bsel = jnp.sum(jnp.where(lane == col, gt, 0.0), axis=1, keepdims=True)
            gsel = jnp.sum(jnp.where(lane == col + DN_V_HEADS, gt, 0.0), axis=1, keepdims=True)
            bb_ref[hh, t0:t0 + tt, :] = jnp.broadcast_to(bsel, (tt, LANES))
            gb_ref[hh, t0:t0 + tt, :] = jnp.broadcast_to(gsel, (tt, LANES))

    c = CHUNK
    g = min(DN_GROUP, t_len // c)
    rows = g * c
    ri = lax.broadcasted_iota(jnp.int32, (c, c), 0)
    ci = lax.broadcasted_iota(jnp.int32, (c, c), 1)
    lower = ri >= ci
    strict = ri > ci
    eye = jnp.where(ri == ci, 1.0, 0.0)
    pair_mask = strict & ((ri // 2) == (ci // 2))
    merge_masks = [((ri // (2 * s)) == (ci // (2 * s))) & ((ri // s) != (ci // s)) & strict
                   for s in (2, 4, 8, 16, 32)]

    def group_step(gi, carry):
        base = pl.multiple_of(gi * rows, rows)
        kc = ks_ref[pl.ds(base, rows), :].reshape(g, c, hd)
        qc = qs_ref[pl.ds(base, rows), :].reshape(g, c, hd)
        kk1 = _bmm_nt(kc, kc)
        qk1 = _bmm_nt(qc, kc)
        two = lambda x: jnp.concatenate([x, x], axis=0)
        kk, qk, kc2, qc2 = two(kk1), two(qk1), two(kc), two(qc)
        beta_b = jnp.concatenate([bb_ref[hh, pl.ds(base, rows), :].reshape(g, c, LANES) for hh in range(2)], axis=0)
        gcc_b = jnp.concatenate([gb_ref[hh, pl.ds(base, rows), :].reshape(g, c, LANES) for hh in range(2)], axis=0)
        grow = gcr_ref[0, 0, :, pl.ds(base, rows)]
        gcr = jnp.stack([grow[hh:hh + 1, n * c:(n + 1) * c] for hh in range(2) for n in range(g)], axis=0)
        vc = jnp.concatenate([vs_ref[pl.ds(base, rows), hh * hd:(hh + 1) * hd].reshape(g, c, hd)
                              for hh in range(2)], axis=0)
        gl_b = gcc_b[:, c - 1:c, :]
        dm = gcc_b[:, :, :c] - gcr
        decay = jnp.where(lower, jnp.exp(dm), 0.0)
        lm = jnp.where(strict, kk * beta_b[:, :, :c] * decay, 0.0)
        tb = (eye - jnp.where(pair_mask, lm, 0.0)).astype(BF16)
        lmb = lm.astype(BF16)
        zero_b = jnp.zeros_like(lmb)
        for mk in merge_masks:
            tb = tb - _bmm(_bmm(tb, jnp.where(mk, lmb, zero_b)), tb).astype(BF16)
        eg = jnp.exp(gcc_b)
        uw = _bmm(tb, jnp.concatenate([vc * beta_b, kc2 * (beta_b * eg)], axis=2))
        qd = qc2 * eg
        kd = kc2 * jnp.exp(gl_b - gcc_b)
        am = qk * decay
        for hh in range(2):
            cols = slice(hh * hd, (hh + 1) * hd)
            inst = slice(hh * g, (hh + 1) * g)
            u_ref[0, pl.ds(base, rows), cols] = uw[inst, :, :hd].reshape(rows, hd).astype(u_ref.dtype)
            w_ref[0, pl.ds(base, rows), cols] = uw[inst, :, hd:].reshape(rows, hd).astype(w_ref.dtype)
            qd_ref[0, pl.ds(base, rows), cols] = qd[inst].reshape(rows, hd).astype(qd_ref.dtype)
            kd_ref[0, pl.ds(base, rows), cols] = kd[inst].reshape(rows, hd).astype(kd_ref.dtype)
            a_ref[0, pl.ds(base, rows), hh * c:(hh + 1) * c] = am[inst].reshape(rows, c).astype(a_ref.dtype)
        return carry

    lax.fori_loop(0, t_len // rows, group_step, 0)


def dn_prep(proj, conv_w, gates, gcr):
    b, t, _ = proj.shape
    hd = DN_HEAD_DIM
    nq = DN_QK_HEADS
    v_blk0 = 2 * DN_KEY_DIM // (2 * hd)
    wide = pl.BlockSpec((1, t, 2 * hd), lambda bi, j: (bi, 0, j))
    big = jax.ShapeDtypeStruct((b, t, DN_VAL_DIM), BF16)
    return pl.pallas_call(
        functools.partial(_dn_prep_body, t_len=t),
        grid=(b, nq),
        in_specs=[
            pl.BlockSpec((1, t, hd), lambda bi, j: (bi, 0, j)),
            pl.BlockSpec((1, t, hd), lambda bi, j: (bi, 0, nq + j)),
            pl.BlockSpec((1, t, 2 * hd), lambda bi, j: (bi, 0, v_blk0 + j)),
            pl.BlockSpec((4, hd), lambda bi, j: (0, j)),
            pl.BlockSpec((4, hd), lambda bi, j: (0, nq + j)),
            pl.BlockSpec((4, 2 * hd), lambda bi, j: (0, v_blk0 + j)),
            pl.BlockSpec((1, t, LANES), lambda bi, j: (bi, 0, 0)),
            pl.BlockSpec((1, 1, 2, t), lambda bi, j: (bi, j, 0, 0)),
        ],
        out_specs=[wide, wide, wide, wide, pl.BlockSpec((1, t, 2 * CHUNK), lambda bi, j: (bi, 0, j))],
        out_shape=[big, big, big, big, jax.ShapeDtypeStruct((b, t, DN_V_HEADS * CHUNK), BF16)],
        scratch_shapes=[
            pltpu.VMEM((t, hd), F32), pltpu.VMEM((t, hd), F32), pltpu.VMEM((t, 2 * hd), F32),
            pltpu.VMEM((2, t, LANES), F32), pltpu.VMEM((2, t, LANES), F32),
        ],
        compiler_params=_params("parallel", "arbitrary"),
        name="dn_prep",
    )(proj, proj, proj, conv_w, conv_w, conv_w, gates, gcr)


DN_REC_HEADS = 32
DN_REC_ROWS = 256


def _dn_rec_body(u_ref, w_ref, qd_ref, kd_ref, a_ref, z_ref, gl_ref, onorm_ref, o_ref, s_ref, *, t_len):
    hd = DN_HEAD_DIM
    c = CHUNK

    @pl.when(pl.program_id(2) == 0)
    def _():
        s_ref[...] = jnp.zeros_like(s_ref)

    onorm = onorm_ref[...]

    def step(ci, carry):
        r0 = pl.multiple_of(ci * c, c)
        egl = jnp.exp(gl_ref[0, 0, ci])
        heads = range(DN_REC_HEADS)
        cols = [slice(h * hd, (h + 1) * hd) for h in heads]
        st = [s_ref[h] for h in heads]
        r = [jnp.dot(jnp.concatenate([w_ref[0, pl.ds(r0, c), cols[h]], qd_ref[0, pl.ds(r0, c), cols[h]]], axis=0),
                     st[h].astype(BF16), preferred_element_type=F32) for h in heads]
        vb = [(u_ref[0, pl.ds(r0, c), cols[h]].astype(F32) - r[h][:c]).astype(BF16) for h in heads]
        upd = [lax.dot_general(kd_ref[0, pl.ds(r0, c), cols[h]], vb[h], (((0,), (0,)), ((), ())),
                               preferred_element_type=F32) for h in heads]
        o = [r[h][c:] + jnp.dot(a_ref[0, pl.ds(r0, c), h * c:(h + 1) * c], vb[h], preferred_element_type=F32)
             for h in heads]
        for h in heads:
            s_ref[h] = st[h] * egl[:, h:h + 1] + upd[h]
        for h in heads:
            y = o[h] * lax.rsqrt(jnp.mean(o[h] * o[h], axis=-1, keepdims=True) + EPS) * onorm
            zz = z_ref[0, pl.ds(r0, c), cols[h]]
            o_ref[0, pl.ds(r0, c), cols[h]] = (y * (zz * _sigmoid(zz))).astype(o_ref.dtype)
        return carry

    lax.fori_loop(0, t_len // c, step, 0)


def dn_rec(u, w, qd, kd, a, proj, gl, out_norm):
    b, t, _ = u.shape
    hd = DN_HEAD_DIM
    hb = DN_REC_HEADS
    ng = DN_V_HEADS // hb
    tt = min(DN_REC_ROWS, t)
    z_blk0 = DN_CONV_DIM // (hb * hd)
    wide = pl.BlockSpec((1, tt, hb * hd), lambda bi, g, ti: (bi, ti, g))
    return pl.pallas_call(
        functools.partial(_dn_rec_body, t_len=tt),
        grid=(b, ng, t // tt),
        in_specs=[
            wide, wide, wide, wide,
            pl.BlockSpec((1, tt, hb * CHUNK), lambda bi, g, ti: (bi, ti, g)),
            pl.BlockSpec((1, tt, hb * hd), lambda bi, g, ti: (bi, ti, z_blk0 + g)),
            pl.BlockSpec((1, 1, tt // CHUNK, 1, hb), lambda bi, g, ti: (bi, g, ti, 0, 0)),
            pl.BlockSpec((1, hd), lambda bi, g, ti: (0, 0)),
        ],
        out_specs=wide,
        out_shape=jax.ShapeDtypeStruct((b, t, DN_VAL_DIM), BF16),
        scratch_shapes=[pltpu.VMEM((hb, hd, hd), F32)],
        compiler_params=_params("parallel", "parallel", "arbitrary"),
        name="dn_rec",
    )(u, w, qd, kd, a, proj, gl, out_norm.reshape(1, hd))


def gated_deltanet_layer(h2, u, b, t, next_gain, w_in, conv_w, a_log, dt_bias, out_norm, w_out):
    n = b * t
    n_qkvz = DN_CONV_DIM + DN_VAL_DIM
    proj = matmul(u, w_in[:, :n_qkvz].astype(BF16), F32, tn=2048).reshape(b, t, n_qkvz)
    w_ba = jnp.pad(w_in[:, n_qkvz:], ((0, 0), (0, LANES - 2 * DN_V_HEADS))).astype(BF16)
    pba = matmul(u, w_ba, F32)
    pad = lambda p: jnp.pad(p.astype(F32), (DN_V_HEADS, LANES - 2 * DN_V_HEADS)).reshape(1, LANES)
    gates = dn_gates(pba, pad(a_log), pad(dt_bias))
    gc = gates[:, DN_V_HEADS:2 * DN_V_HEADS].reshape(b, t, DN_V_HEADS)
    gcr = gc.reshape(b, t, DN_QK_HEADS, 2).transpose(0, 2, 3, 1)
    ng = DN_V_HEADS // DN_REC_HEADS
    gl = gc[:, CHUNK - 1::CHUNK, :].reshape(b, t // CHUNK, ng, 1, DN_REC_HEADS).transpose(0, 2, 1, 3, 4)
    uu, ww, qd, kd, am = dn_prep(proj, conv_w, gates.reshape(b, t, LANES), gcr)
    o = dn_rec(uu, ww, qd, kd, am, proj, gl, out_norm)
    return matmul_res_norm(o.reshape(n, DN_VAL_DIM), w_out.astype(BF16), h2, next_gain, BF16)


def _dsa_prep_body(p_ref, qn_ref, kvn_ref, kin_ref, q_out, kv_out, ki_out, w_out):
    def nrm(x, g):
        return x * lax.rsqrt(jnp.mean(x * x, axis=-1, keepdims=True) + EPS) * g

    a, b2, c2 = Q_LORA, Q_LORA + KV_LORA, Q_LORA + KV_LORA + IDX_DIM
    q_out[...] = nrm(p_ref[:, 0:a], qn_ref[...]).astype(q_out.dtype)
    kv_out[...] = nrm(p_ref[:, a:b2], kvn_ref[...]).astype(kv_out.dtype)
    ki_out[...] = nrm(p_ref[:, b2:c2], kin_ref[...]).astype(ki_out.dtype)
    w_out[...] = p_ref[:, c2:c2 + LANES] * (IDX_HEADS ** -0.5 * IDX_DIM ** -0.5)


def dsa_prep(proj, q_norm, kv_norm, kidx_norm, tm=512):
    n, width = proj.shape
    row = lambda i: (i, 0)
    fix = lambda i: (0, 0)
    return pl.pallas_call(
        _dsa_prep_body,
        grid=(n // tm,),
        in_specs=[pl.BlockSpec((tm, width), row), pl.BlockSpec((1, Q_LORA), fix),
                  pl.BlockSpec((1, KV_LORA), fix), pl.BlockSpec((1, IDX_DIM), fix)],
        out_specs=[pl.BlockSpec((tm, Q_LORA), row), pl.BlockSpec((tm, KV_LORA), row),
                   pl.BlockSpec((tm, IDX_DIM), row), pl.BlockSpec((tm, LANES), row)],
        out_shape=[jax.ShapeDtypeStruct((n, Q_LORA), BF16), jax.ShapeDtypeStruct((n, KV_LORA), BF16),
                   jax.ShapeDtypeStruct((n, IDX_DIM), BF16), jax.ShapeDtypeStruct((n, LANES), F32)],
        compiler_params=_params("parallel"),
        name="dsa_prep",
    )(proj, q_norm.reshape(1, -1), kv_norm.reshape(1, -1), kidx_norm.reshape(1, -1))


def _qabs_body(q_ref, w_ref, o_ref, *, r):
    for h in range(DSA_HEADS):
        res = jnp.dot(q_ref[:, h * DSA_HEAD_DIM:(h + 1) * DSA_HEAD_DIM], w_ref[h],
                      preferred_element_type=F32) * (DSA_HEAD_DIM ** -0.5 * LOG2_E)
        for rr in range(r):
            o_ref[rr, h] = res[rr * QBLOCK:(rr + 1) * QBLOCK].astype(o_ref.dtype)


def q_absorb(q_all, w_uk, tm=1024):
    n = q_all.shape[0]
    r = tm // QBLOCK
    hw = DSA_HEADS * DSA_HEAD_DIM
    return pl.pallas_call(
        functools.partial(_qabs_body, r=r),
        grid=(n // tm,),
        in_specs=[pl.BlockSpec((tm, hw), lambda i: (i, 0)),
                  pl.BlockSpec((DSA_HEADS, DSA_HEAD_DIM, KV_LORA), lambda i: (0, 0, 0))],
        out_specs=pl.BlockSpec((r, DSA_HEADS, QBLOCK, KV_LORA), lambda i: (i, 0, 0, 0)),
        out_shape=jax.ShapeDtypeStruct((n // QBLOCK, DSA_HEADS, QBLOCK, KV_LORA), BF16),
        compiler_params=_params("parallel"),
        name="q_absorb",
    )(q_all, w_uk)


def _vup_body(o_ref, w_ref, out_ref, *, r):
    for h in range(DSA_HEADS):
        x = o_ref[:, h].reshape(r * QBLOCK, KV_LORA)
        out_ref[:, h * DSA_HEAD_DIM:(h + 1) * DSA_HEAD_DIM] = jnp.dot(
            x, w_ref[h], preferred_element_type=F32).astype(out_ref.dtype)


def v_up(o_lat, w_uv, tm=1024):
    nb = o_lat.shape[0]
    r = tm // QBLOCK
    hw = DSA_HEADS * DSA_HEAD_DIM
    return pl.pallas_call(
        functools.partial(_vup_body, r=r),
        grid=(nb // r,),
        in_specs=[pl.BlockSpec((r, DSA_HEADS, QBLOCK, KV_LORA), lambda i: (i, 0, 0, 0)),
                  pl.BlockSpec((DSA_HEADS, KV_LORA, DSA_HEAD_DIM), lambda i: (0, 0, 0))],
        out_specs=pl.BlockSpec((tm, hw), lambda i: (i, 0)),
        out_shape=jax.ShapeDtypeStruct((nb * QBLOCK, hw), BF16),
        compiler_params=_params("parallel"),
        name="v_up",
    )(o_lat, w_uv)


def _dsa_core_body(qidx_ref, widx_ref, kidx_ref, ckv_ref, qabs_ref, o_ref,
                   keys_ref, tau_ref, bias_ref, s_ref, p_ref, m_ref, l_ref, acc_ref, *, top_k, kt):
    i = pl.program_id(1)
    nq = QBLOCK
    nkt = keys_ref.shape[0]
    n_act = ((i + 1) * nq + kt - 1) // kt
    wt = widx_ref[0].T
    krow = lax.broadcasted_iota(jnp.int32, (kt, nq), 0)
    qcol = lax.broadcasted_iota(jnp.int32, (kt, nq), 1)
    q_chunk = (i * nq + qcol) // CHUNK
    tn = (((1,), (1,)), ((), ()))
    sub = 8

    def index_tile(jt, carry):
        k0 = pl.multiple_of(jt * kt, kt)
        kblk = kidx_ref[0, pl.ds(k0, kt), :]
        sc = jnp.zeros((kt, nq), F32)
        for h in range(IDX_HEADS):
            d = lax.dot_general(kblk, qidx_ref[0, :, h * IDX_DIM:(h + 1) * IDX_DIM], tn,
                                preferred_element_type=F32)
            sc = sc + jnp.maximum(d, 0.0) * wt[h:h + 1, :]
        sc = jnp.where(sc == 0.0, 0.0, sc)
        bits = pltpu.bitcast(sc, jnp.int32)
        key = bits ^ ((bits >> 31) & 0x7FFFFFFF)
        adm = ((k0 + krow) // CHUNK) <= q_chunk
        keys_ref[jt] = jnp.where(adm, key, INT_MIN)
        return carry

    lax.fori_loop(0, n_act, index_tile, 0)

    for jt in range(nkt):
        @pl.when(jt >= n_act)
        def _():
            keys_ref[jt] = jnp.full((kt, nq), INT_MIN, jnp.int32)

    def bisect_over(ntiles):
        def bisect(it, prefix):
            cand = prefix | lax.shift_left(jnp.int32(1), 31 - it)
            cand_s = cand ^ INT_MIN
            cnt = jnp.zeros((sub, nq), F32)
            for jt in range(ntiles):
                hit = jnp.where(keys_ref[jt] >= cand_s, 1.0, 0.0)
                cnt = cnt + jnp.sum(hit.reshape(kt // sub, sub, nq), axis=0)
            return jnp.where(jnp.sum(cnt, axis=0, keepdims=True) >= top_k, cand, prefix)

        prefix = lax.fori_loop(0, 32, bisect, jnp.zeros((1, nq), jnp.int32))
        tau_ref[...] = jnp.broadcast_to(jnp.maximum(prefix ^ INT_MIN, INT_MIN + 1), tau_ref.shape)

    step = 2 if nkt % 2 == 0 else 1
    for ntiles in range(step, nkt + 1, step):
        @pl.when((n_act > ntiles - step) & (n_act <= ntiles))
        def _():
            bisect_over(ntiles)

    tau = tau_ref[0:1, :]

    def tie_counts(jt, carry):
        gt, eq = carry
        kj = keys_ref[jt]
        gt = gt + jnp.sum(jnp.where(kj > tau, 1.0, 0.0).reshape(kt // sub, sub, nq), axis=0)
        eq = eq + jnp.sum(jnp.where(kj == tau, 1.0, 0.0).reshape(kt // sub, sub, nq), axis=0)
        return gt, eq

    zero8 = jnp.zeros((sub, nq), F32)
    gt8, eq8 = lax.fori_loop(0, n_act, tie_counts, (zero8, zero8))
    need = top_k - jnp.sum(gt8, axis=0, keepdims=True)
    surplus = jnp.sum(eq8, axis=0, keepdims=True) > need

    @pl.when(jnp.max(jnp.where(surplus, 1.0, 0.0)) > 0.0)
    def _():
        before_row = (lax.broadcasted_iota(jnp.int32, (kt, kt), 1)
                      < lax.broadcasted_iota(jnp.int32, (kt, kt), 0))
        tri = jnp.where(before_row, 1.0, 0.0).astype(BF16)

        def drop_surplus(jt, seen):
            kj = keys_ref[jt]
            eq = kj == tau
            eqf = jnp.where(eq, 1.0, 0.0)
            earlier = jnp.dot(tri, eqf.astype(BF16), preferred_element_type=F32) + seen
            keys_ref[jt] = jnp.where(eq & (earlier >= need), INT_MIN, kj)
            return seen + jnp.sum(eqf, axis=0, keepdims=True)

        lax.fori_loop(0, n_act, drop_surplus, jnp.zeros((1, nq), F32))

    rows = DSA_HEADS * nq
    rep = kt // LANES
    m_ref[...] = jnp.full(m_ref.shape, M_INIT, F32)
    l_ref[...] = jnp.zeros_like(l_ref)
    acc_ref[...] = jnp.zeros_like(acc_ref)

    def attend_tile(jt, carry):
        k0 = pl.multiple_of(jt * kt, kt)
        ck = ckv_ref[0, pl.ds(k0, kt), :]
        bias_ref[...] = jnp.where(keys_ref[jt] >= tau, 0.0, MASKED).T
        s_ref[...] = lax.dot_general(qabs_ref[0].reshape(rows, KV_LORA), ck, tn, preferred_element_type=F32)

        for h in range(DSA_HEADS):
            hr = slice(h * nq, (h + 1) * nq)
            s = s_ref[hr, :] + bias_ref[...]
            m_old = m_ref[hr, :]
            m_new = jnp.maximum(m_old, jnp.max(s, axis=1, keepdims=True))
            alpha = jnp.exp2(m_old - m_new)
            p = jnp.exp2(s - jnp.concatenate([m_new] * rep, axis=1))
            l_ref[hr, :] = alpha * l_ref[hr, :] + jnp.sum(p, axis=1, keepdims=True)
            m_ref[hr, :] = m_new
            acc_ref[hr, :] = acc_ref[hr, :] * jnp.concatenate([alpha] * (KV_LORA // LANES), axis=1)
            p_ref[hr, :] = p.astype(BF16)
        acc_ref[...] += jnp.dot(p_ref[...], ck, preferred_element_type=F32)
        return carry

    lax.fori_loop(0, n_act, attend_tile, 0)
    inv_l = 1.0 / l_ref[...]
    out = acc_ref[...] * jnp.concatenate([inv_l] * (KV_LORA // LANES), axis=1)
    o_ref[0] = out.reshape(DSA_HEADS, nq, KV_LORA).astype(o_ref.dtype)


def dsa_core(q_all, widx, kidx, ckv, qabs, b, t, kt=256):
    nblk = t // QBLOCK
    top_k = min(IDX_TOPK, t // 4)
    hw = IDX_HEADS * IDX_DIM
    rows = DSA_HEADS * QBLOCK
    return pl.pallas_call(
        functools.partial(_dsa_core_body, top_k=top_k, kt=kt),
        grid=(b, nblk),
        in_specs=[
            pl.BlockSpec((1, QBLOCK, hw), lambda bi, i: (bi, i, 1)),
            pl.BlockSpec((1, QBLOCK, LANES), lambda bi, i: (bi, i, 0)),
            pl.BlockSpec((1, t, IDX_DIM), lambda bi, i: (bi, 0, 0)),
            pl.BlockSpec((1, t, KV_LORA), lambda bi, i: (bi, 0, 0)),
            pl.BlockSpec((1, DSA_HEADS, QBLOCK, KV_LORA), lambda bi, i: (bi * nblk + i, 0, 0, 0)),
        ],
        out_specs=pl.BlockSpec((1, DSA_HEADS, QBLOCK, KV_LORA), lambda bi, i: (bi * nblk + i, 0, 0, 0)),
        out_shape=jax.ShapeDtypeStruct((b * nblk, DSA_HEADS, QBLOCK, KV_LORA), BF16),
        scratch_shapes=[pltpu.VMEM((t // kt, kt, QBLOCK), jnp.int32), pltpu.VMEM((8, QBLOCK), jnp.int32),
                        pltpu.VMEM((QBLOCK, kt), F32),
                        pltpu.VMEM((rows, kt), F32), pltpu.VMEM((rows, kt), BF16),
                        pltpu.VMEM((rows, LANES), F32), pltpu.VMEM((rows, LANES), F32),
                        pltpu.VMEM((rows, KV_LORA), F32)],
        compiler_params=_params("parallel", "arbitrary"),
        name="dsa_core",
    )(q_all.reshape(b, t, -1), widx.reshape(b, t, LANES), kidx.reshape(b, t, IDX_DIM),
      ckv.reshape(b, t, KV_LORA), qabs)


def dsa_layer(h2, u, b, t, next_gain, w_in, q_norm, kv_norm, kidx_norm, w_uq, w_uk, w_uv, w_out):
    width = Q_LORA + KV_LORA + IDX_DIM + LANES
    w_in_p = jnp.pad(w_in, ((0, 0), (0, width - w_in.shape[1]))).astype(BF16)
    proj = matmul(u, w_in_p, F32)
    qlat, ckv, kidx, widx = dsa_prep(proj, q_norm, kv_norm, kidx_norm)
    q_all = matmul(qlat, w_uq.astype(BF16), BF16)
    qabs = q_absorb(q_all, w_uk.astype(BF16))
    o_lat = dsa_core(q_all, widx, kidx, ckv, qabs, b, t)
    o = v_up(o_lat, w_uv.astype(BF16))
    return matmul_res_norm(o, w_out.astype(BF16), h2, next_gain, BF16)


def mlp(h2, u, next_gain, u_dtype, w_up, w_down):
    a = matmul(u, w_up.astype(BF16), BF16, relu2=True, tn=2048)
    return matmul_res_norm(a, w_down.astype(BF16), h2, next_gain, u_dtype)


def kernel(x, norm_mix, norm_mlp, norm_final, dn_w_in, dn_conv_w, dn_a_log, dn_dt_bias, dn_out_norm, dn_w_out, dsa_w_in, dsa_q_norm, dsa_kv_norm, dsa_kidx_norm, dsa_w_uq, dsa_w_uk, dsa_w_uv, dsa_w_out, mlp_w_up, mlp_w_down):
    b, t, d = x.shape
    depth = norm_mix.shape[0]
    h2 = x.reshape(b * t, d)
    u = rms_norm(h2, norm_mix[0], BF16)
    for i in range(depth):
        j = i // 2
        if i % 2 == 0:
            h2, u = gated_deltanet_layer(h2, u, b, t, norm_mlp[i], dn_w_in[j], dn_conv_w[j], dn_a_log[j],
                                         dn_dt_bias[j], dn_out_norm[j], dn_w_out[j])
        else:
            h2, u = dsa_layer(h2, u, b, t, norm_mlp[i], dsa_w_in[j], dsa_q_norm[j], dsa_kv_norm[j],
                              dsa_kidx_norm[j], dsa_w_uq[j], dsa_w_uk[j], dsa_w_uv[j], dsa_w_out[j])
        last = i == depth - 1
        h2, u = mlp(h2, u, norm_final if last else norm_mix[i + 1], x.dtype if last else BF16,
                    mlp_w_up[i], mlp_w_down[i])
    return u.reshape(b, t, d)
```

```python
import functools

import jax
import jax.numpy as jnp
from jax import lax
from jax.experimental import pallas as pl
from jax.experimental.pallas import tpu as pltpu

F32 = jnp.float32
BF16 = jnp.bfloat16
EPS = 1e-6
CHUNK = 64
QBLOCK = 128
DN_QK_HEADS = 16
DN_V_HEADS = 32
DN_HEAD_DIM = 128
DN_KEY_DIM = DN_QK_HEADS * DN_HEAD_DIM
DN_VAL_DIM = DN_V_HEADS * DN_HEAD_DIM
DN_CONV_DIM = 2 * DN_KEY_DIM + DN_VAL_DIM
DSA_HEADS = 16
DSA_HEAD_DIM = 128
Q_LORA = 512
KV_LORA = 256
IDX_HEADS = 16
IDX_DIM = 128
IDX_TOPK = 256
LANES = 128
VMEM_LIMIT_BYTES = 56 * 1024 * 1024
INT_MIN = -(2 ** 31)
MASKED = -1e30
M_INIT = -1e20
LOG2_E = 1.4426950408889634


def _params(*sem):
    return pltpu.CompilerParams(dimension_semantics=sem, vmem_limit_bytes=VMEM_LIMIT_BYTES)


def _sigmoid(x):
    return 1.0 / (1.0 + jnp.exp(-x))


def _norm_body(x_ref, g_ref, o_ref):
    x = x_ref[...]
    y = x * lax.rsqrt(jnp.mean(x * x, axis=-1, keepdims=True) + EPS) * g_ref[...]
    o_ref[...] = y.astype(o_ref.dtype)


def rms_norm(x, g, out_dtype, tm=512):
    n, d = x.shape
    return pl.pallas_call(
        _norm_body,
        grid=(n // tm,),
        in_specs=[pl.BlockSpec((tm, d), lambda i: (i, 0)), pl.BlockSpec((1, d), lambda i: (0, 0))],
        out_specs=pl.BlockSpec((tm, d), lambda i: (i, 0)),
        out_shape=jax.ShapeDtypeStruct((n, d), out_dtype),
        compiler_params=_params("parallel"),
        name="rms_norm",
    )(x, g.reshape(1, d))


MM_SUB_COLS = 512


def _mm_body(a_ref, w_ref, *rest, nk, relu2, residual):
    if residual:
        r_ref, o_ref = rest[0], rest[1]
    else:
        r_ref, o_ref = None, rest[0]
    acc_ref = rest[-1] if nk > 1 else None

    def finish(acc, cols=slice(None)):
        if relu2:
            acc = jnp.square(jnp.maximum(acc, 0.0))
        if residual:
            acc = acc + r_ref[:, cols]
        o_ref[:, cols] = acc.astype(o_ref.dtype)

    sub = min(MM_SUB_COLS, o_ref.shape[1])
    blocks = [slice(c0, c0 + sub) for c0 in range(0, o_ref.shape[1], sub)]

    def k_step(first, last):
        for cols in blocks:
            part = jnp.dot(a_ref[...], w_ref[:, cols], preferred_element_type=F32)
            if not first:
                part = part + acc_ref[:, cols]
            if last:
                finish(part, cols)
            else:
                acc_ref[:, cols] = part

    if nk == 1:
        k_step(True, True)
        return
    k = pl.program_id(2)
    pl.when(k == 0)(functools.partial(k_step, True, False))
    pl.when((k > 0) & (k < nk - 1))(functools.partial(k_step, False, False))
    pl.when(k == nk - 1)(functools.partial(k_step, False, True))


def matmul(a, w, out_dtype, *, relu2=False, residual=None, tm=1024, tn=1024, tk=2048):
    m, kdim = a.shape
    n = w.shape[1]
    tm, tn, tk = min(tm, m), min(tn, n), min(tk, kdim)
    assert m % tm == 0 and n % tn == 0 and kdim % tk == 0
    nk = kdim // tk
    in_specs = [pl.BlockSpec((tm, tk), lambda i, j, k: (i, k)), pl.BlockSpec((tk, tn), lambda i, j, k: (k, j))]
    args = [a, w]
    if residual is not None:
        in_specs.append(pl.BlockSpec((tm, tn), lambda i, j, k: (i, j)))
        args.append(residual)
    return pl.pallas_call(
        functools.partial(_mm_body, nk=nk, relu2=relu2, residual=residual is not None),
        grid=(m // tm, n // tn, nk),
        in_specs=in_specs,
        out_specs=pl.BlockSpec((tm, tn), lambda i, j, k: (i, j)),
        out_shape=jax.ShapeDtypeStruct((m, n), out_dtype),
        scratch_shapes=[pltpu.VMEM((tm, tn), F32)] if nk > 1 else [],
        compiler_params=_params("parallel", "parallel", "arbitrary"),
        name="matmul",
    )(*args)


def _mm_res_norm_body(a_ref, w_ref, r_ref, g_ref, h_ref, u_ref, *scratch, nk):
    acc_ref = scratch[0] if nk > 1 else None
    n = h_ref.shape[1]
    sub = min(MM_SUB_COLS, n)
    blocks = [slice(c0, c0 + sub) for c0 in range(0, n, sub)]

    def k_step(first, last):
        for cols in blocks:
            part = jnp.dot(a_ref[...], w_ref[:, cols], preferred_element_type=F32)
            if not first:
                part = part + acc_ref[:, cols]
            if last:
                h_ref[:, cols] = part + r_ref[:, cols]
            else:
                acc_ref[:, cols] = part
        if last:
            h = h_ref[...]
            scale = lax.rsqrt(jnp.mean(h * h, axis=-1, keepdims=True) + EPS)
            u_ref[...] = (h * scale * g_ref[...]).astype(u_ref.dtype)

    if nk == 1:
        k_step(True, True)
        return
    k = pl.program_id(1)
    pl.when(k == 0)(functools.partial(k_step, True, False))
    pl.when((k > 0) & (k < nk - 1))(functools.partial(k_step, False, False))
    pl.when(k == nk - 1)(functools.partial(k_step, False, True))


def matmul_res_norm(a, w, residual, gain, u_dtype, *, tm=512, tk=2048):
    m, kdim = a.shape
    n = w.shape[1]
    tm, tk = min(tm, m), min(tk, kdim)
    assert m % tm == 0 and kdim % tk == 0
    nk = kdim // tk
    row = pl.BlockSpec((tm, n), lambda i, k: (i, 0))
    return pl.pallas_call(
        functools.partial(_mm_res_norm_body, nk=nk),
        grid=(m // tm, nk),
        in_specs=[pl.BlockSpec((tm, tk), lambda i, k: (i, k)), pl.BlockSpec((tk, n), lambda i, k: (k, 0)),
                  row, pl.BlockSpec((1, n), lambda i, k: (0, 0))],
        out_specs=[row, row],
        out_shape=[jax.ShapeDtypeStruct((m, n), F32), jax.ShapeDtypeStruct((m, n), u_dtype)],
        scratch_shapes=[pltpu.VMEM((tm, n), F32)] if nk > 1 else [],
        compiler_params=_params("parallel", "arbitrary"),
        name="matmul_res_norm",
    )(a, w, residual, gain.reshape(1, n))


def _dn_gates_body(x_ref, alog_ref, dt_ref, o_ref):
    x = x_ref[...]
    g = -jnp.exp(alog_ref[...]) * (jnp.maximum(x + dt_ref[...], 0.0)
                                   + jnp.log(1.0 + jnp.exp(-jnp.abs(x + dt_ref[...]))))
    pos = lax.broadcasted_iota(jnp.int32, x.shape, 0) % CHUNK
    s = 1
    while s < CHUNK:
        g = g + jnp.where(pos >= s, pltpu.roll(g, s, axis=0), 0.0)
        s *= 2
    lane = lax.broadcasted_iota(jnp.int32, x.shape, 1)
    o_ref[...] = jnp.where(lane < DN_V_HEADS, _sigmoid(x), g)


def dn_gates(pba, alog_pad, dt_pad, tm=512):
    n = pba.shape[0]
    return pl.pallas_call(
        _dn_gates_body,
        grid=(n // tm,),
        in_specs=[pl.BlockSpec((tm, LANES), lambda i: (i, 0)),
                  pl.BlockSpec((1, LANES), lambda i: (0, 0)),
                  pl.BlockSpec((1, LANES), lambda i: (0, 0))],
        out_specs=pl.BlockSpec((tm, LANES), lambda i: (i, 0)),
        out_shape=jax.ShapeDtypeStruct((n, LANES), F32),
        compiler_params=_params("parallel"),
        name="dn_gates",
    )(pba, alog_pad, dt_pad)


def _conv_silu(ref, cw_ref, t0, tt):
    cw = cw_ref[...]
    acc = ref[0, t0:t0 + tt, :] * cw[3:4, :]
    for s in (1, 2, 3):
        if t0 == 0:
            x = ref[0, 0:tt, :]
            rows = lax.broadcasted_iota(jnp.int32, x.shape, 0)
            xs = jnp.where(rows >= s, pltpu.roll(x, s, axis=0), 0.0)
        else:
            xs = ref[0, t0 - s:t0 - s + tt, :]
        acc = acc + xs * cw[3 - s:4 - s, :]
    return acc * _sigmoid(acc)


def _l2norm(t, scale=1.0):
    return t * (lax.rsqrt(jnp.sum(t * t, axis=-1, keepdims=True) + EPS) * scale)


def _bmm(a, b):
    return jnp.einsum("nij,njk->nik", a.astype(BF16), b.astype(BF16), preferred_element_type=F32)


def _bmm_nt(a, b):
    return jnp.einsum("nid,njd->nij", a.astype(BF16), b.astype(BF16), preferred_element_type=F32)


DN_GROUP = 32


def _dn_prep_body(q_ref, k_ref, v_ref, cwq_ref, cwk_ref, cwv_ref, gates_ref, gcr_ref,
                  u_ref, w_ref, qd_ref, kd_ref, a_ref, qs_ref, ks_ref, vs_ref, bb_ref, gb_ref, *, t_len):
    j = pl.program_id(1)
    tt = 256
    hd = DN_HEAD_DIM
    for ti in range(t_len // tt):
        t0 = ti * tt
        qs_ref[t0:t0 + tt, :] = _l2norm(_conv_silu(q_ref, cwq_ref, t0, tt), hd ** -0.5)
        ks_ref[t0:t0 + tt, :] = _l2norm(_conv_silu(k_ref, cwk_ref, t0, tt))
        vs_ref[t0:t0 + tt, :] = _conv_silu(v_ref, cwv_ref, t0, tt)
        gt = gates_ref[0, t0:t0 + tt, :]
        lane = lax.broadcasted_iota(jnp.int32, gt.shape, 1)
        for hh in range(2):
            col = 2 * j + hh
            bsel = jnp.sum(jnp.where(lane == col, gt, 0.0), axis=1, keepdims=True)
            gsel = jnp.sum(jnp.where(lane == col + DN_V_HEADS, gt, 0.0), axis=1, keepdims=True)
            bb_ref[hh, t0:t0 + tt, :] = jnp.broadcast_to(bsel, (tt, LANES))
            gb_ref[hh, t0:t0 + tt, :] = jnp.broadcast_to(gsel, (tt, LANES))

    c = CHUNK
    g = min(DN_GROUP, t_len // c)
    rows = g * c
    ri = lax.broadcasted_iota(jnp.int32, (c, c), 0)
    ci = lax.broadcasted_iota(jnp.int32, (c, c), 1)
    lower = ri >= ci
    strict = ri > ci
    eye = jnp.where(ri == ci, 1.0, 0.0)
    pair_mask = strict & ((ri // 2) == (ci // 2))
    merge_masks = [((ri // (2 * s)) == (ci // (2 * s))) & ((ri // s) != (ci // s)) & strict
                   for s in (2, 4, 8, 16, 32)]

    def group_step(gi, carry):
        base = pl.multiple_of(gi * rows, rows)
        kc = ks_ref[pl.ds(base, rows), :].reshape(g, c, hd)
        qc = qs_ref[pl.ds(base, rows), :].reshape(g, c, hd)
        kk1 = _bmm_nt(kc, kc)
        qk1 = _bmm_nt(qc, kc)
        two = lambda x: jnp.concatenate([x, x], axis=0)
        kk, qk, kc2, qc2 = two(kk1), two(qk1), two(kc), two(qc)
        beta_b = jnp.concatenate([bb_ref[hh, pl.ds(base, rows), :].reshape(g, c, LANES) for hh in range(2)], axis=0)
        gcc_b = jnp.concatenate([gb_ref[hh, pl.ds(base, rows), :].reshape(g, c, LANES) for hh in range(2)], axis=0)
        grow = gcr_ref[0, 0, :, pl.ds(base, rows)]
        gcr = jnp.stack([grow[hh:hh + 1, n * c:(n + 1) * c] for hh in range(2) for n in range(g)], axis=0)
        vc = jnp.concatenate([vs_ref[pl.ds(base, rows), hh * hd:(hh + 1) * hd].reshape(g, c, hd)
                              for hh in range(2)], axis=0)
        gl_b = gcc_b[:, c - 1:c, :]
        dm = gcc_b[:, :, :c] - gcr
        decay = jnp.where(lower, jnp.exp(dm), 0.0)
        lm = jnp.where(strict, kk * beta_b[:, :, :c] * decay, 0.0)
        tb = (eye - jnp.where(pair_mask, lm, 0.0)).astype(BF16)
        lmb = lm.astype(BF16)
        zero_b = jnp.zeros_like(lmb)
        for mk in merge_masks:
            tb = tb - _bmm(_bmm(tb, jnp.where(mk, lmb, zero_b)), tb).astype(BF16)
        eg = jnp.exp(gcc_b)
        uw = _bmm(tb, jnp.concatenate([vc * beta_b, kc2 * (beta_b * eg)], axis=2))
        qd = qc2 * eg
        kd = kc2 * jnp.exp(gl_b - gcc_b)
        am = qk * decay
        for hh in range(2):
            cols = slice(hh * hd, (hh + 1) * hd)
            inst = slice(hh * g, (hh + 1) * g)
            u_ref[0, pl.ds(base, rows), cols] = uw[inst, :, :hd].reshape(rows, hd).astype(u_ref.dtype)
            w_ref[0, pl.ds(base, rows), cols] = uw[inst, :, hd:].reshape(rows, hd).astype(w_ref.dtype)
            qd_ref[0, pl.ds(base, rows), cols] = qd[inst].reshape(rows, hd).astype(qd_ref.dtype)
            kd_ref[0, pl.ds(base, rows), cols] = kd[inst].reshape(rows, hd).astype(kd_ref.dtype)
            a_ref[0, pl.ds(base, rows), hh * c:(hh + 1) * c] = am[inst].reshape(rows, c).astype(a_ref.dtype)
        return carry

    lax.fori_loop(0, t_len // rows, group_step, 0)


def dn_prep(proj, conv_w, gates, gcr):
    b, t, _ = proj.shape
    hd = DN_HEAD_DIM
    nq = DN_QK_HEADS
    v_blk0 = 2 * DN_KEY_DIM // (2 * hd)
    wide = pl.BlockSpec((1, t, 2 * hd), lambda bi, j: (bi, 0, j))
    big = jax.ShapeDtypeStruct((b, t, DN_VAL_DIM), BF16)
    return pl.pallas_call(
        functools.partial(_dn_prep_body, t_len=t),
        grid=(b, nq),
        in_specs=[
            pl.BlockSpec((1, t, hd), lambda bi, j: (bi, 0, j)),
            pl.BlockSpec((1, t, hd), lambda bi, j: (bi, 0, nq + j)),
            pl.BlockSpec((1, t, 2 * hd), lambda bi, j: (bi, 0, v_blk0 + j)),
            pl.BlockSpec((4, hd), lambda bi, j: (0, j)),
            pl.BlockSpec((4, hd), lambda bi, j: (0, nq + j)),
            pl.BlockSpec((4, 2 * hd), lambda bi, j: (0, v_blk0 + j)),
            pl.BlockSpec((1, t, LANES), lambda bi, j: (bi, 0, 0)),
            pl.BlockSpec((1, 1, 2, t), lambda bi, j: (bi, j, 0, 0)),
        ],
        out_specs=[wide, wide, wide, wide, pl.BlockSpec((1, t, 2 * CHUNK), lambda bi, j: (bi, 0, j))],
        out_shape=[big, big, big, big, jax.ShapeDtypeStruct((b, t, DN_V_HEADS * CHUNK), BF16)],
        scratch_shapes=[
            pltpu.VMEM((t, hd), F32), pltpu.VMEM((t, hd), F32), pltpu.VMEM((t, 2 * hd), F32),
            pltpu.VMEM((2, t, LANES), F32), pltpu.VMEM((2, t, LANES), F32),
        ],
        compiler_params=_params("parallel", "arbitrary"),
        name="dn_prep",
    )(proj, proj, proj, conv_w, conv_w, conv_w, gates, gcr)


DN_REC_HEADS = 32
DN_REC_ROWS = 256


def _dn_rec_body(u_ref, w_ref, qd_ref, kd_ref, a_ref, z_ref, gl_ref, onorm_ref, o_ref, s_ref, *, t_len):
    hd = DN_HEAD_DIM
    c = CHUNK

    @pl.when(pl.program_id(2) == 0)
    def _():
        s_ref[...] = jnp.zeros_like(s_ref)

    onorm = onorm_ref[...]

    def step(ci, carry):
        r0 = pl.multiple_of(ci * c, c)
        egl = jnp.exp(gl_ref[0, 0, ci])
        heads = range(DN_REC_HEADS)
        cols = [slice(h * hd, (h + 1) * hd) for h in heads]
        st = [s_ref[h] for h in heads]
        r = [jnp.dot(jnp.concatenate([w_ref[0, pl.ds(r0, c), cols[h]], qd_ref[0, pl.ds(r0, c), cols[h]]], axis=0),
                     st[h].astype(BF16), preferred_element_type=F32) for h in heads]
        vb = [(u_ref[0, pl.ds(r0, c), cols[h]].astype(F32) - r[h][:c]).astype(BF16) for h in heads]
        upd = [lax.dot_general(kd_ref[0, pl.ds(r0, c), cols[h]], vb[h], (((0,), (0,)), ((), ())),
                               preferred_element_type=F32) for h in heads]
        o = [r[h][c:] + jnp.dot(a_ref[0, pl.ds(r0, c), h * c:(h + 1) * c], vb[h], preferred_element_type=F32)
             for h in heads]
        for h in heads:
            s_ref[h] = st[h] * egl[:, h:h + 1] + upd[h]
        for h in heads:
            y = o[h] * lax.rsqrt(jnp.mean(o[h] * o[h], axis=-1, keepdims=True) + EPS) * onorm
            zz = z_ref[0, pl.ds(r0, c), cols[h]]
            o_ref[0, pl.ds(r0, c), cols[h]] = (y * (zz * _sigmoid(zz))).astype(o_ref.dtype)
        return carry

    lax.fori_loop(0, t_len // c, step, 0)


def dn_rec(u, w, qd, kd, a, proj, gl, out_norm):
    b, t, _ = u.shape
    hd = DN_HEAD_DIM
    hb = DN_REC_HEADS
    ng = DN_V_HEADS // hb
    tt = min(DN_REC_ROWS, t)
    z_blk0 = DN_CONV_DIM // (hb * hd)
    wide = pl.BlockSpec((1, tt, hb * hd), lambda bi, g, ti: (bi, ti, g))
    return pl.pallas_call(
        functools.partial(_dn_rec_body, t_len=tt),
        grid=(b, ng, t // tt),
        in_specs=[
            wide, wide, wide, wide,
            pl.BlockSpec((1, tt, hb * CHUNK), lambda bi, g, ti: (bi, ti, g)),
            pl.BlockSpec((1, tt, hb * hd), lambda bi, g, ti: (bi, ti, z_blk0 + g)),
            pl.BlockSpec((1, 1, tt // CHUNK, 1, hb), lambda bi, g, ti: (bi, g, ti, 0, 0)),
            pl.BlockSpec((1, hd), lambda bi, g, ti: (0, 0)),
        ],
        out_specs=wide,
        out_shape=jax.ShapeDtypeStruct((b, t, DN_VAL_DIM), BF16),
        scratch_shapes=[pltpu.VMEM((hb, hd, hd), F32)],
        compiler_params=_params("parallel", "parallel", "arbitrary"),
        name="dn_rec",
    )(u, w, qd, kd, a, proj, gl, out_norm.reshape(1, hd))


def gated_deltanet_layer(h2, u, b, t, next_gain, w_in, conv_w, a_log, dt_bias, out_norm, w_out):
    n = b * t
    n_qkvz = DN_CONV_DIM + DN_VAL_DIM
    proj = matmul(u, w_in[:, :n_qkvz].astype(BF16), F32, tn=2048).reshape(b, t, n_qkvz)
    w_ba = jnp.pad(w_in[:, n_qkvz:], ((0, 0), (0, LANES - 2 * DN_V_HEADS))).astype(BF16)
    pba = matmul(u, w_ba, F32)
    pad = lambda p: jnp.pad(p.astype(F32), (DN_V_HEADS, LANES - 2 * DN_V_HEADS)).reshape(1, LANES)
    gates = dn_gates(pba, pad(a_log), pad(dt_bias))
    gc = gates[:, DN_V_HEADS:2 * DN_V_HEADS].reshape(b, t, DN_V_HEADS)
    gcr = gc.reshape(b, t, DN_QK_HEADS, 2).transpose(0, 2, 3, 1)
    ng = DN_V_HEADS // DN_REC_HEADS
    gl = gc[:, CHUNK - 1::CHUNK, :].reshape(b, t // CHUNK, ng, 1, DN_REC_HEADS).transpose(0, 2, 1, 3, 4)
    uu, ww, qd, kd, am = dn_prep(proj, conv_w, gates.reshape(b, t, LANES), gcr)
    o = dn_rec(uu, ww, qd, kd, am, proj, gl, out_norm)
    return matmul_res_norm(o.reshape(n, DN_VAL_DIM), w_out.astype(BF16), h2, next_gain, BF16)


def _dsa_in_body(u_ref, w_ref, qn_ref, kvn_ref, kin_ref, q_out, kv_out, ki_out, w_out):
    def nrm(x, g):
        return x * lax.rsqrt(jnp.mean(x * x, axis=-1, keepdims=True) + EPS) * g

    a, b2, c2 = Q_LORA, Q_LORA + KV_LORA, Q_LORA + KV_LORA + IDX_DIM
    dot = lambda lo, hi: jnp.dot(u_ref[...], w_ref[:, lo:hi], preferred_element_type=F32)
    q_out[...] = nrm(dot(0, a), qn_ref[...]).astype(q_out.dtype)
    kv_out[...] = nrm(dot(a, b2), kvn_ref[...]).astype(kv_out.dtype)
    ki_out[...] = nrm(dot(b2, c2), kin_ref[...]).astype(ki_out.dtype)
    w_out[...] = dot(c2, c2 + LANES) * (IDX_HEADS ** -0.5 * IDX_DIM ** -0.5)


def dsa_in_proj(u, w_in_p, q_norm, kv_norm, kidx_norm, tm=1024):
    n, d = u.shape
    width = w_in_p.shape[1]
    row = lambda i: (i, 0)
    fix = lambda i: (0, 0)
    return pl.pallas_call(
        _dsa_in_body,
        grid=(n // tm,),
        in_specs=[pl.BlockSpec((tm, d), row), pl.BlockSpec((d, width), fix), pl.BlockSpec((1, Q_LORA), fix),
                  pl.BlockSpec((1, KV_LORA), fix), pl.BlockSpec((1, IDX_DIM), fix)],
        out_specs=[pl.BlockSpec((tm, Q_LORA), row), pl.BlockSpec((tm, KV_LORA), row),
                   pl.BlockSpec((tm, IDX_DIM), row), pl.BlockSpec((tm, LANES), row)],
        out_shape=[jax.ShapeDtypeStruct((n, Q_LORA), BF16), jax.ShapeDtypeStruct((n, KV_LORA), BF16),
                   jax.ShapeDtypeStruct((n, IDX_DIM), BF16), jax.ShapeDtypeStruct((n, LANES), F32)],
        compiler_params=_params("parallel"),
        name="dsa_in_proj",
    )(u, w_in_p, q_norm.reshape(1, -1), kv_norm.reshape(1, -1), kidx_norm.reshape(1, -1))


def _qabs_body(q_ref, w_ref, o_ref, *, r):
    for h in range(DSA_HEADS):
        res = jnp.dot(q_ref[:, h * DSA_HEAD_DIM:(h + 1) * DSA_HEAD_DIM], w_ref[h],
                      preferred_element_type=F32) * (DSA_HEAD_DIM ** -0.5 * LOG2_E)
        for rr in range(r):
            o_ref[rr, h] = res[rr * QBLOCK:(rr + 1) * QBLOCK].astype(o_ref.dtype)


def q_absorb(q_all, w_uk, tm=1024):
    n = q_all.shape[0]
    r = tm // QBLOCK
    hw = DSA_HEADS * DSA_HEAD_DIM
    return pl.pallas_call(
        functools.partial(_qabs_body, r=r),
        grid=(n // tm,),
        in_specs=[pl.BlockSpec((tm, hw), lambda i: (i, 0)),
                  pl.BlockSpec((DSA_HEADS, DSA_HEAD_DIM, KV_LORA), lambda i: (0, 0, 0))],
        out_specs=pl.BlockSpec((r, DSA_HEADS, QBLOCK, KV_LORA), lambda i: (i, 0, 0, 0)),
        out_shape=jax.ShapeDtypeStruct((n // QBLOCK, DSA_HEADS, QBLOCK, KV_LORA), BF16),
        compiler_params=_params("parallel"),
        name="q_absorb",
    )(q_all, w_uk)


def _vup_body(o_ref, w_ref, out_ref, *, r):
    for h in range(DSA_HEADS):
        x = o_ref[:, h].reshape(r * QBLOCK, KV_LORA)
        out_ref[:, h * DSA_HEAD_DIM:(h + 1) * DSA_HEAD_DIM] = jnp.dot(
            x, w_ref[h], preferred_element_type=F32).astype(out_ref.dtype)


def v_up(o_lat, w_uv, tm=1024):
    nb = o_lat.shape[0]
    r = tm // QBLOCK
    hw = DSA_HEADS * DSA_HEAD_DIM
    return pl.pallas_call(
        functools.partial(_vup_body, r=r),
        grid=(nb // r,),
        in_specs=[pl.BlockSpec((r, DSA_HEADS, QBLOCK, KV_LORA), lambda i: (i, 0, 0, 0)),
                  pl.BlockSpec((DSA_HEADS, KV_LORA, DSA_HEAD_DIM), lambda i: (0, 0, 0))],
        out_specs=pl.BlockSpec((tm, hw), lambda i: (i, 0)),
        out_shape=jax.ShapeDtypeStruct((nb * QBLOCK, hw), BF16),
        compiler_params=_params("parallel"),
        name="v_up",
    )(o_lat, w_uv)


def _dsa_core_body(qidx_ref, widx_ref, kidx_ref, ckv_ref, qabs_ref, o_ref,
                   keys_ref, tau_ref, bias_ref, s_ref, p_ref, m_ref, l_ref, acc_ref, *, top_k, kt):
    i = pl.program_id(1)
    nq = QBLOCK
    nkt = keys_ref.shape[0]
    n_act = ((i + 1) * nq + kt - 1) // kt
    wt = widx_ref[0].T
    krow = lax.broadcasted_iota(jnp.int32, (kt, nq), 0)
    qcol = lax.broadcasted_iota(jnp.int32, (kt, nq), 1)
    q_chunk = (i * nq + qcol) // CHUNK
    tn = (((1,), (1,)), ((), ()))
    sub = 8

    def index_tile(jt, carry):
        k0 = pl.multiple_of(jt * kt, kt)
        kblk = kidx_ref[0, pl.ds(k0, kt), :]
        sc = jnp.zeros((kt, nq), F32)
        for h in range(IDX_HEADS):
            d = lax.dot_general(kblk, qidx_ref[0, :, h * IDX_DIM:(h + 1) * IDX_DIM], tn,
                                preferred_element_type=F32)
            sc = sc + jnp.maximum(d, 0.0) * wt[h:h + 1, :]
        sc = jnp.where(sc == 0.0, 0.0, sc)
        bits = pltpu.bitcast(sc, jnp.int32)
        key = bits ^ ((bits >> 31) & 0x7FFFFFFF)
        adm = ((k0 + krow) // CHUNK) <= q_chunk
        keys_ref[jt] = jnp.where(adm, key, INT_MIN)
        return carry

    lax.fori_loop(0, n_act, index_tile, 0)

    for jt in range(nkt):
        @pl.when(jt >= n_act)
        def _():
            keys_ref[jt] = jnp.full((kt, nq), INT_MIN, jnp.int32)

    def bisect_over(ntiles):
        def bisect(it, prefix):
            cand = prefix | lax.shift_left(jnp.int32(1), 31 - it)
            cand_s = cand ^ INT_MIN
            cnt = jnp.zeros((sub, nq), F32)
            for jt in range(ntiles):
                hit = jnp.where(keys_ref[jt] >= cand_s, 1.0, 0.0)
                cnt = cnt + jnp.sum(hit.reshape(kt // sub, sub, nq), axis=0)
            return jnp.where(jnp.sum(cnt, axis=0, keepdims=True) >= top_k, cand, prefix)

        prefix = lax.fori_loop(0, 32, bisect, jnp.zeros((1, nq), jnp.int32))
        tau_ref[...] = jnp.broadcast_to(jnp.maximum(prefix ^ INT_MIN, INT_MIN + 1), tau_ref.shape)

    step = 2 if nkt % 2 == 0 else 1
    for ntiles in range(step, nkt + 1, step):
        @pl.when((n_act > ntiles - step) & (n_act <= ntiles))
        def _():
            bisect_over(ntiles)

    tau = tau_ref[0:1, :]

    def tie_counts(jt, carry):
        gt, eq = carry
        kj = keys_ref[jt]
        gt = gt + jnp.sum(jnp.where(kj > tau, 1.0, 0.0).reshape(kt // sub, sub, nq), axis=0)
        eq = eq + jnp.sum(jnp.where(kj == tau, 1.0, 0.0).reshape(kt // sub, sub, nq), axis=0)
        return gt, eq

    zero8 = jnp.zeros((sub, nq), F32)
    gt8, eq8 = lax.fori_loop(0, n_act, tie_counts, (zero8, zero8))
    need = top_k - jnp.sum(gt8, axis=0, keepdims=True)
    surplus = jnp.sum(eq8, axis=0, keepdims=True) > need

    @pl.when(jnp.max(jnp.where(surplus, 1.0, 0.0)) > 0.0)
    def _():
        before_row = (lax.broadcasted_iota(jnp.int32, (kt, kt), 1)
                      < lax.broadcasted_iota(jnp.int32, (kt, kt), 0))
        tri = jnp.where(before_row, 1.0, 0.0).astype(BF16)

        def drop_surplus(jt, seen):
            kj = keys_ref[jt]
            eq = kj == tau
            eqf = jnp.where(eq, 1.0, 0.0)
            earlier = jnp.dot(tri, eqf.astype(BF16), preferred_element_type=F32) + seen
            keys_ref[jt] = jnp.where(eq & (earlier >= need), INT_MIN, kj)
            return seen + jnp.sum(eqf, axis=0, keepdims=True)

        lax.fori_loop(0, n_act, drop_surplus, jnp.zeros((1, nq), F32))

    rows = DSA_HEADS * nq
    rep = kt // LANES
    def attend_tile(jt, first):
        k0 = pl.multiple_of(jt * kt, kt)
        ck = ckv_ref[0, pl.ds(k0, kt), :]
        bias_ref[...] = jnp.where(keys_ref[jt] >= tau, 0.0, MASKED).T
        s_ref[...] = lax.dot_general(qabs_ref[0].reshape(rows, KV_LORA), ck, tn, preferred_element_type=F32)

        for h in range(DSA_HEADS):
            hr = slice(h * nq, (h + 1) * nq)
            s = s_ref[hr, :] + bias_ref[...]
            m_old = jnp.full((nq, LANES), M_INIT, F32) if first else m_ref[hr, :]
            m_new = jnp.maximum(m_old, jnp.max(s, axis=1, keepdims=True))
            p = jnp.exp2(s - jnp.concatenate([m_new] * rep, axis=1))
            psum = jnp.sum(p, axis=1, keepdims=True)
            if first:
                l_ref[hr, :] = jnp.broadcast_to(psum, (nq, LANES))
            else:
                alpha = jnp.exp2(m_old - m_new)
                l_ref[hr, :] = alpha * l_ref[hr, :] + psum
                acc_ref[hr, :] = acc_ref[hr, :] * jnp.concatenate([alpha] * (KV_LORA // LANES), axis=1)
            m_ref[hr, :] = m_new
            p_ref[hr, :] = p.astype(BF16)
        pv = jnp.dot(p_ref[...], ck, preferred_element_type=F32)
        if first:
            acc_ref[...] = pv
        else:
            acc_ref[...] += pv

    attend_tile(0, True)

    def later_tile(jt, carry):
        attend_tile(jt, False)
        return carry

    lax.fori_loop(1, n_act, later_tile, 0)
    inv_l = 1.0 / l_ref[...]
    out = acc_ref[...] * jnp.concatenate([inv_l] * (KV_LORA // LANES), axis=1)
    o_ref[0] = out.reshape(DSA_HEADS, nq, KV_LORA).astype(o_ref.dtype)


def dsa_core(q_all, widx, kidx, ckv, qabs, b, t, kt=256):
    nblk = t // QBLOCK
    top_k = min(IDX_TOPK, t // 4)
    hw = IDX_HEADS * IDX_DIM
    rows = DSA_HEADS * QBLOCK
    return pl.pallas_call(
        functools.partial(_dsa_core_body, top_k=top_k, kt=kt),
        grid=(b, nblk),
        in_specs=[
            pl.BlockSpec((1, QBLOCK, hw), lambda bi, i: (bi, i, 1)),
            pl.BlockSpec((1, QBLOCK, LANES), lambda bi, i: (bi, i, 0)),
            pl.BlockSpec((1, t, IDX_DIM), lambda bi, i: (bi, 0, 0)),
            pl.BlockSpec((1, t, KV_LORA), lambda bi, i: (bi, 0, 0)),
            pl.BlockSpec((1, DSA_HEADS, QBLOCK, KV_LORA), lambda bi, i: (bi * nblk + i, 0, 0, 0)),
        ],
        out_specs=pl.BlockSpec((1, DSA_HEADS, QBLOCK, KV_LORA), lambda bi, i: (bi * nblk + i, 0, 0, 0)),
        out_shape=jax.ShapeDtypeStruct((b * nblk, DSA_HEADS, QBLOCK, KV_LORA), BF16),
        scratch_shapes=[pltpu.VMEM((t // kt, kt, QBLOCK), jnp.int32), pltpu.VMEM((8, QBLOCK), jnp.int32),
                        pltpu.VMEM((QBLOCK, kt), F32),
                        pltpu.VMEM((rows, kt), F32), pltpu.VMEM((rows, kt), BF16),
                        pltpu.VMEM((rows, LANES), F32), pltpu.VMEM((rows, LANES), F32),
                        pltpu.VMEM((rows, KV_LORA), F32)],
        compiler_params=_params("parallel", "arbitrary"),
        name="dsa_core",
    )(q_all.reshape(b, t, -1), widx.reshape(b, t, LANES), kidx.reshape(b, t, IDX_DIM),
      ckv.reshape(b, t, KV_LORA), qabs)


def dsa_layer(h2, u, b, t, next_gain, w_in, q_norm, kv_norm, kidx_norm, w_uq, w_uk, w_uv, w_out):
    width = Q_LORA + KV_LORA + IDX_DIM + LANES
    w_in_p = jnp.pad(w_in, ((0, 0), (0, width - w_in.shape[1]))).astype(BF16)
    qlat, ckv, kidx, widx = dsa_in_proj(u, w_in_p, q_norm, kv_norm, kidx_norm)
    q_all = matmul(qlat, w_uq.astype(BF16), BF16)
    qabs = q_absorb(q_all, w_uk.astype(BF16))
    o_lat = dsa_core(q_all, widx, kidx, ckv, qabs, b, t)
    o = v_up(o_lat, w_uv.astype(BF16))
    return matmul_res_norm(o, w_out.astype(BF16), h2, next_gain, BF16)


def mlp(h2, u, next_gain, u_dtype, w_up, w_down):
    a = matmul(u, w_up.astype(BF16), BF16, relu2=True, tn=2048)
    return matmul_res_norm(a, w_down.astype(BF16), h2, next_gain, u_dtype)


def kernel(x, norm_mix, norm_mlp, norm_final, dn_w_in, dn_conv_w, dn_a_log, dn_dt_bias, dn_out_norm, dn_w_out, dsa_w_in, dsa_q_norm, dsa_kv_norm, dsa_kidx_norm, dsa_w_uq, dsa_w_uk, dsa_w_uv, dsa_w_out, mlp_w_up, mlp_w_down):
    b, t, d = x.shape
    depth = norm_mix.shape[0]
    h2 = x.reshape(b * t, d)
    u = rms_norm(h2, norm_mix[0], BF16)
    for i in range(depth):
        j = i // 2
        if i % 2 == 0:
            h2, u = gated_deltanet_layer(h2, u, b, t, norm_mlp[i], dn_w_in[j], dn_conv_w[j], dn_a_log[j],
                                         dn_dt_bias[j], dn_out_norm[j], dn_w_out[j])
        else:
            h2, u = dsa_layer(h2, u, b, t, norm_mlp[i], dsa_w_in[j], dsa_q_norm[j], dsa_kv_norm[j],
                              dsa_kidx_norm[j], dsa_w_uq[j], dsa_w_uk[j], dsa_w_uv[j], dsa_w_out[j])
        last = i == depth - 1
        h2, u = mlp(h2, u, norm_final if last else norm_mix[i + 1], x.dtype if last else BF16,
                    mlp_w_up[i], mlp_w_down[i])
    return u.reshape(b, t, d)
```

```python
import functools

import jax
import jax.numpy as jnp
from jax import lax
from jax.experimental import pallas as pl
from jax.experimental.pallas import tpu as pltpu

F32 = jnp.float32
BF16 = jnp.bfloat16
EPS = 1e-6
CHUNK = 64
QBLOCK = 128
DN_QK_HEADS = 16
DN_V_HEADS = 32
DN_HEAD_DIM = 128
DN_KEY_DIM = DN_QK_HEADS * DN_HEAD_DIM
DN_VAL_DIM = DN_V_HEADS * DN_HEAD_DIM
DN_CONV_DIM = 2 * DN_KEY_DIM + DN_VAL_DIM
DSA_HEADS = 16
DSA_HEAD_DIM = 128
Q_LORA = 512
KV_LORA = 256
IDX_HEADS = 16
IDX_DIM = 128
IDX_TOPK = 256
LANES = 128
VMEM_LIMIT_BYTES = 56 * 1024 * 1024
INT_MIN = -(2 ** 31)
MASKED = -1e30
M_INIT = -1e20
LOG2_E = 1.4426950408889634


def _params(*sem):
    return pltpu.CompilerParams(dimension_semantics=sem, vmem_limit_bytes=VMEM_LIMIT_BYTES)


def _sigmoid(x):
    return 1.0 / (1.0 + jnp.exp2(x * (-LOG2_E)))


def _norm_body(x_ref, g_ref, o_ref):
    x = x_ref[...]
    y = x * lax.rsqrt(jnp.mean(x * x, axis=-1, keepdims=True) + EPS) * g_ref[...]
    o_ref[...] = y.astype(o_ref.dtype)


def rms_norm(x, g, out_dtype, tm=512):
    n, d = x.shape
    return pl.pallas_call(
        _norm_body,
        grid=(n // tm,),
        in_specs=[pl.BlockSpec((tm, d), lambda i: (i, 0)), pl.BlockSpec((1, d), lambda i: (0, 0))],
        out_specs=pl.BlockSpec((tm, d), lambda i: (i, 0)),
        out_shape=jax.ShapeDtypeStruct((n, d), out_dtype),
        compiler_params=_params("parallel"),
        name="rms_norm",
    )(x, g.reshape(1, d))


MM_SUB_COLS = 512


def _mm_body(a_ref, w_ref, *rest, nk, relu2, residual):
    if residual:
        r_ref, o_ref = rest[0], rest[1]
    else:
        r_ref, o_ref = None, rest[0]
    acc_ref = rest[-1] if nk > 1 else None

    def finish(acc, cols=slice(None)):
        if relu2:
            acc = jnp.square(jnp.maximum(acc, 0.0))
        if residual:
            acc = acc + r_ref[:, cols]
        o_ref[:, cols] = acc.astype(o_ref.dtype)

    sub = min(MM_SUB_COLS, o_ref.shape[1])
    blocks = [slice(c0, c0 + sub) for c0 in range(0, o_ref.shape[1], sub)]

    def k_step(first, last):
        for cols in blocks:
            part = jnp.dot(a_ref[...], w_ref[:, cols], preferred_element_type=F32)
            if not first:
                part = part + acc_ref[:, cols]
            if last:
                finish(part, cols)
            else:
                acc_ref[:, cols] = part

    if nk == 1:
        k_step(True, True)
        return
    k = pl.program_id(2)
    pl.when(k == 0)(functools.partial(k_step, True, False))
    pl.when((k > 0) & (k < nk - 1))(functools.partial(k_step, False, False))
    pl.when(k == nk - 1)(functools.partial(k_step, False, True))


def matmul(a, w, out_dtype, *, relu2=False, residual=None, tm=1024, tn=1024, tk=2048):
    m, kdim = a.shape
    n = w.shape[1]
    tm, tn, tk = min(tm, m), min(tn, n), min(tk, kdim)
    assert m % tm == 0 and n % tn == 0 and kdim % tk == 0
    nk = kdim // tk
    in_specs = [pl.BlockSpec((tm, tk), lambda i, j, k: (i, k)), pl.BlockSpec((tk, tn), lambda i, j, k: (k, j))]
    args = [a, w]
    if residual is not None:
        in_specs.append(pl.BlockSpec((tm, tn), lambda i, j, k: (i, j)))
        args.append(residual)
    return pl.pallas_call(
        functools.partial(_mm_body, nk=nk, relu2=relu2, residual=residual is not None),
        grid=(m // tm, n // tn, nk),
        in_specs=in_specs,
        out_specs=pl.BlockSpec((tm, tn), lambda i, j, k: (i, j)),
        out_shape=jax.ShapeDtypeStruct((m, n), out_dtype),
        scratch_shapes=[pltpu.VMEM((tm, tn), F32)] if nk > 1 else [],
        compiler_params=_params("parallel", "parallel", "arbitrary"),
        name="matmul",
    )(*args)


def _mm_res_norm_body(a_ref, w_ref, r_ref, g_ref, h_ref, u_ref, *scratch, nk):
    acc_ref = scratch[0] if nk > 1 else None
    n = h_ref.shape[1]
    sub = min(MM_SUB_COLS, n)
    blocks = [slice(c0, c0 + sub) for c0 in range(0, n, sub)]

    def k_step(first, last):
        for cols in blocks:
            part = jnp.dot(a_ref[...], w_ref[:, cols], preferred_element_type=F32)
            if not first:
                part = part + acc_ref[:, cols]
            if last:
                h_ref[:, cols] = part + r_ref[:, cols]
            else:
                acc_ref[:, cols] = part
        if last:
            h = h_ref[...]
            scale = lax.rsqrt(jnp.mean(h * h, axis=-1, keepdims=True) + EPS)
            u_ref[...] = (h * scale * g_ref[...]).astype(u_ref.dtype)

    if nk == 1:
        k_step(True, True)
        return
    k = pl.program_id(1)
    pl.when(k == 0)(functools.partial(k_step, True, False))
    pl.when((k > 0) & (k < nk - 1))(functools.partial(k_step, False, False))
    pl.when(k == nk - 1)(functools.partial(k_step, False, True))


def matmul_res_norm(a, w, residual, gain, u_dtype, *, tm=512, tk=2048):
    m, kdim = a.shape
    n = w.shape[1]
    tm, tk = min(tm, m), min(tk, kdim)
    assert m % tm == 0 and kdim % tk == 0
    nk = kdim // tk
    row = pl.BlockSpec((tm, n), lambda i, k: (i, 0))
    return pl.pallas_call(
        functools.partial(_mm_res_norm_body, nk=nk),
        grid=(m // tm, nk),
        in_specs=[pl.BlockSpec((tm, tk), lambda i, k: (i, k)), pl.BlockSpec((tk, n), lambda i, k: (k, 0)),
                  row, pl.BlockSpec((1, n), lambda i, k: (0, 0))],
        out_specs=[row, row],
        out_shape=[jax.ShapeDtypeStruct((m, n), F32), jax.ShapeDtypeStruct((m, n), u_dtype)],
        scratch_shapes=[pltpu.VMEM((tm, n), F32)] if nk > 1 else [],
        compiler_params=_params("parallel", "arbitrary"),
        name="matmul_res_norm",
    )(a, w, residual, gain.reshape(1, n))


def _dn_gates_body(x_ref, alog_ref, dt_ref, o_ref):
    x = x_ref[...]
    g = (-LOG2_E) * jnp.exp(alog_ref[...]) * (jnp.maximum(x + dt_ref[...], 0.0)
                                              + jnp.log(1.0 + jnp.exp(-jnp.abs(x + dt_ref[...]))))
    pos = lax.broadcasted_iota(jnp.int32, x.shape, 0) % CHUNK
    s = 1
    while s < CHUNK:
        g = g + jnp.where(pos >= s, pltpu.roll(g, s, axis=0), 0.0)
        s *= 2
    lane = lax.broadcasted_iota(jnp.int32, x.shape, 1)
    o_ref[...] = jnp.where(lane < DN_V_HEADS, _sigmoid(x), g)


def dn_gates(pba, alog_pad, dt_pad, tm=512):
    n = pba.shape[0]
    return pl.pallas_call(
        _dn_gates_body,
        grid=(n // tm,),
        in_specs=[pl.BlockSpec((tm, LANES), lambda i: (i, 0)),
                  pl.BlockSpec((1, LANES), lambda i: (0, 0)),
                  pl.BlockSpec((1, LANES), lambda i: (0, 0))],
        out_specs=pl.BlockSpec((tm, LANES), lambda i: (i, 0)),
        out_shape=jax.ShapeDtypeStruct((n, LANES), F32),
        compiler_params=_params("parallel"),
        name="dn_gates",
    )(pba, alog_pad, dt_pad)


def _conv_silu(ref, cw_ref, t0, tt):
    cw = cw_ref[...]
    acc = ref[0, t0:t0 + tt, :] * cw[3:4, :]
    for s in (1, 2, 3):
        if t0 == 0:
            x = ref[0, 0:tt, :]
            rows = lax.broadcasted_iota(jnp.int32, x.shape, 0)
            xs = jnp.where(rows >= s, pltpu.roll(x, s, axis=0), 0.0)
        else:
            xs = ref[0, t0 - s:t0 - s + tt, :]
        acc = acc + xs * cw[3 - s:4 - s, :]
    return acc * _sigmoid(acc)


def _l2norm(t, scale=1.0):
    return t * (lax.rsqrt(jnp.sum(t * t, axis=-1, keepdims=True) + EPS) * scale)


def _bmm(a, b):
    return jnp.einsum("nij,njk->nik", a.astype(BF16), b.astype(BF16), preferred_element_type=F32)


def _bmm_nt(a, b):
    return jnp.einsum("nid,njd->nij", a.astype(BF16), b.astype(BF16), preferred_element_type=F32)


DN_GROUP = 32


def _dn_prep_body(q_ref, k_ref, v_ref, cwq_ref, cwk_ref, cwv_ref, gates_ref, gcr_ref,
                  u_ref, w_ref, qd_ref, kd_ref, a_ref, qs_ref, ks_ref, vs_ref, bb_ref, gb_ref, *, t_len):
    j = pl.program_id(1)
    tt = 256
    hd = DN_HEAD_DIM
    for ti in range(t_len // tt):
        t0 = ti * tt
        qs_ref[t0:t0 + tt, :] = _l2norm(_conv_silu(q_ref, cwq_ref, t0, tt), hd ** -0.5)
        ks_ref[t0:t0 + tt, :] = _l2norm(_conv_silu(k_ref, cwk_ref, t0, tt))
        vs_ref[t0:t0 + tt, :] = _conv_silu(v_ref, cwv_ref, t0, tt)
        gt = gates_ref[0, t0:t0 + tt, :]
        lane = lax.broadcasted_iota(jnp.int32, gt.shape, 1)
        for hh in range(2):
            col = 2 * j + hh
            bsel = jnp.sum(jnp.where(lane == col, gt, 0.0), axis=1, keepdims=True)
            gsel = jnp.sum(jnp.where(lane == col + DN_V_HEADS, gt, 0.0), axis=1, keepdims=True)
            bb_ref[hh, t0:t0 + tt, :] = jnp.broadcast_to(bsel, (tt, LANES))
            gb_ref[hh, t0:t0 + tt, :] = jnp.broadcast_to(gsel, (tt, LANES))

    c = CHUNK
    g = min(DN_GROUP, t_len // c)
    rows = g * c
    ri = lax.broadcasted_iota(jnp.int32, (c, c), 0)
    ci = lax.broadcasted_iota(jnp.int32, (c, c), 1)
    lower = ri >= ci
    strict = ri > ci
    eye = jnp.where(ri == ci, 1.0, 0.0)
    pair_mask = strict & ((ri // 2) == (ci // 2))
    merge_masks = [((ri // (2 * s)) == (ci // (2 * s))) & ((ri // s) != (ci // s)) & strict
                   for s in (2, 4, 8, 16, 32)]

    def group_step(gi, carry):
        base = pl.multiple_of(gi * rows, rows)
        kc = ks_ref[pl.ds(base, rows), :].reshape(g, c, hd)
        qc = qs_ref[pl.ds(base, rows), :].reshape(g, c, hd)
        kk1 = _bmm_nt(kc, kc)
        qk1 = _bmm_nt(qc, kc)
        two = lambda x: jnp.concatenate([x, x], axis=0)
        kk, qk, kc2, qc2 = two(kk1), two(qk1), two(kc), two(qc)
        beta_b = jnp.concatenate([bb_ref[hh, pl.ds(base, rows), :].reshape(g, c, LANES) for hh in range(2)], axis=0)
        gcc_b = jnp.concatenate([gb_ref[hh, pl.ds(base, rows), :].reshape(g, c, LANES) for hh in range(2)], axis=0)
        grow = gcr_ref[0, 0, :, pl.ds(base, rows)]
        gcr = jnp.stack([grow[hh:hh + 1, n * c:(n + 1) * c] for hh in range(2) for n in range(g)], axis=0)
        vc = jnp.concatenate([vs_ref[pl.ds(base, rows), hh * hd:(hh + 1) * hd].reshape(g, c, hd)
                              for hh in range(2)], axis=0)
        gl_b = gcc_b[:, c - 1:c, :]
        dm = gcc_b[:, :, :c] - gcr
        decay = jnp.where(lower, jnp.exp2(dm), 0.0)
        lm = jnp.where(strict, kk * beta_b[:, :, :c] * decay, 0.0)
        tb = (eye - jnp.where(pair_mask, lm, 0.0)).astype(BF16)
        lmb = lm.astype(BF16)
        zero_b = jnp.zeros_like(lmb)
        for mk in merge_masks:
            tb = tb - _bmm(_bmm(tb, jnp.where(mk, lmb, zero_b)), tb).astype(BF16)
        eg = jnp.exp2(gcc_b)
        uw = _bmm(tb, jnp.concatenate([vc * beta_b, kc2 * (beta_b * eg)], axis=2))
        qd = qc2 * eg
        kd = kc2 * jnp.exp2(gl_b - gcc_b)
        am = qk * decay
        for hh in range(2):
            cols = slice(hh * hd, (hh + 1) * hd)
            inst = slice(hh * g, (hh + 1) * g)
            u_ref[0, pl.ds(base, rows), cols] = uw[inst, :, :hd].reshape(rows, hd).astype(u_ref.dtype)
            w_ref[0, pl.ds(base, rows), cols] = uw[inst, :, hd:].reshape(rows, hd).astype(w_ref.dtype)
            qd_ref[0, pl.ds(base, rows), cols] = qd[inst].reshape(rows, hd).astype(qd_ref.dtype)
            kd_ref[0, pl.ds(base, rows), cols] = kd[inst].reshape(rows, hd).astype(kd_ref.dtype)
            a_ref[0, pl.ds(base, rows), hh * c:(hh + 1) * c] = am[inst].reshape(rows, c).astype(a_ref.dtype)
        return carry

    lax.fori_loop(0, t_len // rows, group_step, 0)


def dn_prep(proj, conv_w, gates, gcr):
    b, t, _ = proj.shape
    hd = DN_HEAD_DIM
    nq = DN_QK_HEADS
    v_blk0 = 2 * DN_KEY_DIM // (2 * hd)
    wide = pl.BlockSpec((1, t, 2 * hd), lambda bi, j: (bi, 0, j))
    big = jax.ShapeDtypeStruct((b, t, DN_VAL_DIM), BF16)
    return pl.pallas_call(
        functools.partial(_dn_prep_body, t_len=t),
        grid=(b, nq),
        in_specs=[
            pl.BlockSpec((1, t, hd), lambda bi, j: (bi, 0, j)),
            pl.BlockSpec((1, t, hd), lambda bi, j: (bi, 0, nq + j)),
            pl.BlockSpec((1, t, 2 * hd), lambda bi, j: (bi, 0, v_blk0 + j)),
            pl.BlockSpec((4, hd), lambda bi, j: (0, j)),
            pl.BlockSpec((4, hd), lambda bi, j: (0, nq + j)),
            pl.BlockSpec((4, 2 * hd), lambda bi, j: (0, v_blk0 + j)),
            pl.BlockSpec((1, t, LANES), lambda bi, j: (bi, 0, 0)),
            pl.BlockSpec((1, 1, 2, t), lambda bi, j: (bi, j, 0, 0)),
        ],
        out_specs=[wide, wide, wide, wide, pl.BlockSpec((1, t, 2 * CHUNK), lambda bi, j: (bi, 0, j))],
        out_shape=[big, big, big, big, jax.ShapeDtypeStruct((b, t, DN_V_HEADS * CHUNK), BF16)],
        scratch_shapes=[
            pltpu.VMEM((t, hd), F32), pltpu.VMEM((t, hd), F32), pltpu.VMEM((t, 2 * hd), F32),
            pltpu.VMEM((2, t, LANES), F32), pltpu.VMEM((2, t, LANES), F32),
        ],
        compiler_params=_params("parallel", "arbitrary"),
        name="dn_prep",
    )(proj, proj, proj, conv_w, conv_w, conv_w, gates, gcr)


DN_REC_HEADS = 32
DN_REC_ROWS = 256


def _dn_rec_body(u_ref, w_ref, qd_ref, kd_ref, a_ref, z_ref, gl_ref, onorm_ref, o_ref, s_ref, *, t_len):
    hd = DN_HEAD_DIM
    c = CHUNK

    @pl.when(pl.program_id(2) == 0)
    def _():
        s_ref[...] = jnp.zeros_like(s_ref)

    onorm = onorm_ref[...]

    def step(ci, carry):
        r0 = pl.multiple_of(ci * c, c)
        egl = jnp.exp2(gl_ref[0, 0, ci])
        heads = range(DN_REC_HEADS)
        cols = [slice(h * hd, (h + 1) * hd) for h in heads]
        st = [s_ref[h] for h in heads]
        r = [jnp.dot(jnp.concatenate([w_ref[0, pl.ds(r0, c), cols[h]], qd_ref[0, pl.ds(r0, c), cols[h]]], axis=0),
                     st[h].astype(BF16), preferred_element_type=F32) for h in heads]
        vb = [(u_ref[0, pl.ds(r0, c), cols[h]].astype(F32) - r[h][:c]).astype(BF16) for h in heads]
        upd = [lax.dot_general(kd_ref[0, pl.ds(r0, c), cols[h]], vb[h], (((0,), (0,)), ((), ())),
                               preferred_element_type=F32) for h in heads]
        o = [r[h][c:] + jnp.dot(a_ref[0, pl.ds(r0, c), h * c:(h + 1) * c], vb[h], preferred_element_type=F32)
             for h in heads]
        for h in heads:
            s_ref[h] = st[h] * egl[:, h:h + 1] + upd[h]
        for h in heads:
            y = o[h] * lax.rsqrt(jnp.mean(o[h] * o[h], axis=-1, keepdims=True) + EPS) * onorm
            zz = z_ref[0, pl.ds(r0, c), cols[h]]
            o_ref[0, pl.ds(r0, c), cols[h]] = (y * (zz * _sigmoid(zz))).astype(o_ref.dtype)
        return carry

    lax.fori_loop(0, t_len // c, step, 0)


def dn_rec(u, w, qd, kd, a, proj, gl, out_norm):
    b, t, _ = u.shape
    hd = DN_HEAD_DIM
    hb = DN_REC_HEADS
    ng = DN_V_HEADS // hb
    tt = min(DN_REC_ROWS, t)
    z_blk0 = DN_CONV_DIM // (hb * hd)
    wide = pl.BlockSpec((1, tt, hb * hd), lambda bi, g, ti: (bi, ti, g))
    return pl.pallas_call(
        functools.partial(_dn_rec_body, t_len=tt),
        grid=(b, ng, t // tt),
        in_specs=[
            wide, wide, wide, wide,
            pl.BlockSpec((1, tt, hb * CHUNK), lambda bi, g, ti: (bi, ti, g)),
            pl.BlockSpec((1, tt, hb * hd), lambda bi, g, ti: (bi, ti, z_blk0 + g)),
            pl.BlockSpec((1, 1, tt // CHUNK, 1, hb), lambda bi, g, ti: (bi, g, ti, 0, 0)),
            pl.BlockSpec((1, hd), lambda bi, g, ti: (0, 0)),
        ],
        out_specs=wide,
        out_shape=jax.ShapeDtypeStruct((b, t, DN_VAL_DIM), BF16),
        scratch_shapes=[pltpu.VMEM((hb, hd, hd), F32)],
        compiler_params=_params("parallel", "parallel", "arbitrary"),
        name="dn_rec",
    )(u, w, qd, kd, a, proj, gl, out_norm.reshape(1, hd))


def gated_deltanet_layer(h2, u, b, t, next_gain, w_in, conv_w, a_log, dt_bias, out_norm, w_out):
    n = b * t
    n_qkvz = DN_CONV_DIM + DN_VAL_DIM
    proj = matmul(u, w_in[:, :n_qkvz].astype(BF16), F32, tn=2048).reshape(b, t, n_qkvz)
    w_ba = jnp.pad(w_in[:, n_qkvz:], ((0, 0), (0, LANES - 2 * DN_V_HEADS))).astype(BF16)
    pba = matmul(u, w_ba, F32)
    pad = lambda p: jnp.pad(p.astype(F32), (DN_V_HEADS, LANES - 2 * DN_V_HEADS)).reshape(1, LANES)
    gates = dn_gates(pba, pad(a_log), pad(dt_bias))
    gc = gates[:, DN_V_HEADS:2 * DN_V_HEADS].reshape(b, t, DN_V_HEADS)
    gcr = gc.reshape(b, t, DN_QK_HEADS, 2).transpose(0, 2, 3, 1)
    ng = DN_V_HEADS // DN_REC_HEADS
    gl = gc[:, CHUNK - 1::CHUNK, :].reshape(b, t // CHUNK, ng, 1, DN_REC_HEADS).transpose(0, 2, 1, 3, 4)
    uu, ww, qd, kd, am = dn_prep(proj, conv_w, gates.reshape(b, t, LANES), gcr)
    o = dn_rec(uu, ww, qd, kd, am, proj, gl, out_norm)
    return matmul_res_norm(o.reshape(n, DN_VAL_DIM), w_out.astype(BF16), h2, next_gain, BF16)


def _dsa_in_body(u_ref, w_ref, qn_ref, kvn_ref, kin_ref, q_out, kv_out, ki_out, w_out):
    def nrm(x, g):
        return x * lax.rsqrt(jnp.mean(x * x, axis=-1, keepdims=True) + EPS) * g

    a, b2, c2 = Q_LORA, Q_LORA + KV_LORA, Q_LORA + KV_LORA + IDX_DIM
    dot = lambda lo, hi: jnp.dot(u_ref[...], w_ref[:, lo:hi], preferred_element_type=F32)
    q_out[...] = nrm(dot(0, a), qn_ref[...]).astype(q_out.dtype)
    kv_out[...] = nrm(dot(a, b2), kvn_ref[...]).astype(kv_out.dtype)
    ki_out[...] = nrm(dot(b2, c2), kin_ref[...]).astype(ki_out.dtype)
    w_out[...] = dot(c2, c2 + LANES) * (IDX_HEADS ** -0.5 * IDX_DIM ** -0.5)


def dsa_in_proj(u, w_in_p, q_norm, kv_norm, kidx_norm, tm=1024):
    n, d = u.shape
    width = w_in_p.shape[1]
    row = lambda i: (i, 0)
    fix = lambda i: (0, 0)
    return pl.pallas_call(
        _dsa_in_body,
        grid=(n // tm,),
        in_specs=[pl.BlockSpec((tm, d), row), pl.BlockSpec((d, width), fix), pl.BlockSpec((1, Q_LORA), fix),
                  pl.BlockSpec((1, KV_LORA), fix), pl.BlockSpec((1, IDX_DIM), fix)],
        out_specs=[pl.BlockSpec((tm, Q_LORA), row), pl.BlockSpec((tm, KV_LORA), row),
                   pl.BlockSpec((tm, IDX_DIM), row), pl.BlockSpec((tm, LANES), row)],
        out_shape=[jax.ShapeDtypeStruct((n, Q_LORA), BF16), jax.ShapeDtypeStruct((n, KV_LORA), BF16),
                   jax.ShapeDtypeStruct((n, IDX_DIM), BF16), jax.ShapeDtypeStruct((n, LANES), F32)],
        compiler_params=_params("parallel"),
        name="dsa_in_proj",
    )(u, w_in_p, q_norm.reshape(1, -1), kv_norm.reshape(1, -1), kidx_norm.reshape(1, -1))


def _qabs_body(q_ref, w_ref, o_ref, *, r):
    for h in range(DSA_HEADS):
        res = jnp.dot(q_ref[:, h * DSA_HEAD_DIM:(h + 1) * DSA_HEAD_DIM], w_ref[h],
                      preferred_element_type=F32) * (DSA_HEAD_DIM ** -0.5 * LOG2_E)
        for rr in range(r):
            o_ref[rr, h] = res[rr * QBLOCK:(rr + 1) * QBLOCK].astype(o_ref.dtype)


def q_absorb(q_all, w_uk, tm=1024):
    n = q_all.shape[0]
    r = tm // QBLOCK
    hw = DSA_HEADS * DSA_HEAD_DIM
    return pl.pallas_call(
        functools.partial(_qabs_body, r=r),
        grid=(n // tm,),
        in_specs=[pl.BlockSpec((tm, hw), lambda i: (i, 0)),
                  pl.BlockSpec((DSA_HEADS, DSA_HEAD_DIM, KV_LORA), lambda i: (0, 0, 0))],
        out_specs=pl.BlockSpec((r, DSA_HEADS, QBLOCK, KV_LORA), lambda i: (i, 0, 0, 0)),
        out_shape=jax.ShapeDtypeStruct((n // QBLOCK, DSA_HEADS, QBLOCK, KV_LORA), BF16),
        compiler_params=_params("parallel"),
        name="q_absorb",
    )(q_all, w_uk)


def _vup_body(o_ref, w_ref, out_ref, *, r):
    for h in range(DSA_HEADS):
        x = o_ref[:, h].reshape(r * QBLOCK, KV_LORA)
        out_ref[:, h * DSA_HEAD_DIM:(h + 1) * DSA_HEAD_DIM] = jnp.dot(
            x, w_ref[h], preferred_element_type=F32).astype(out_ref.dtype)


def v_up(o_lat, w_uv, tm=1024):
    nb = o_lat.shape[0]
    r = tm // QBLOCK
    hw = DSA_HEADS * DSA_HEAD_DIM
    return pl.pallas_call(
        functools.partial(_vup_body, r=r),
        grid=(nb // r,),
        in_specs=[pl.BlockSpec((r, DSA_HEADS, QBLOCK, KV_LORA), lambda i: (i, 0, 0, 0)),
                  pl.BlockSpec((DSA_HEADS, KV_LORA, DSA_HEAD_DIM), lambda i: (0, 0, 0))],
        out_specs=pl.BlockSpec((tm, hw), lambda i: (i, 0)),
        out_shape=jax.ShapeDtypeStruct((nb * QBLOCK, hw), BF16),
        compiler_params=_params("parallel"),
        name="v_up",
    )(o_lat, w_uv)


def _dsa_core_body(qidx_ref, widx_ref, kidx_ref, ckv_ref, qabs_ref, o_ref,
                   keys_ref, tau_ref, bias_ref, s_ref, p_ref, m_ref, l_ref, acc_ref, *, top_k, kt):
    i = pl.program_id(1)
    nq = QBLOCK
    nkt = keys_ref.shape[0]
    n_act = ((i + 1) * nq + kt - 1) // kt
    wt = widx_ref[0].T
    krow = lax.broadcasted_iota(jnp.int32, (kt, nq), 0)
    qcol = lax.broadcasted_iota(jnp.int32, (kt, nq), 1)
    q_chunk = (i * nq + qcol) // CHUNK
    tn = (((1,), (1,)), ((), ()))
    sub = 8

    def index_tile(jt, carry):
        k0 = pl.multiple_of(jt * kt, kt)
        kblk = kidx_ref[0, pl.ds(k0, kt), :]
        sc = jnp.zeros((kt, nq), F32)
        for h in range(IDX_HEADS):
            d = lax.dot_general(kblk, qidx_ref[0, :, h * IDX_DIM:(h + 1) * IDX_DIM], tn,
                                preferred_element_type=F32)
            sc = sc + jnp.maximum(d, 0.0) * wt[h:h + 1, :]
        sc = jnp.where(sc == 0.0, 0.0, sc)
        bits = pltpu.bitcast(sc, jnp.int32)
        key = bits ^ ((bits >> 31) & 0x7FFFFFFF)
        adm = ((k0 + krow) // CHUNK) <= q_chunk
        keys_ref[jt] = jnp.where(adm, key, INT_MIN)
        return carry

    lax.fori_loop(0, n_act, index_tile, 0)

    for jt in range(nkt):
        @pl.when(jt >= n_act)
        def _():
            keys_ref[jt] = jnp.full((kt, nq), INT_MIN, jnp.int32)

    def bisect_over(ntiles):
        def bisect(it, prefix):
            cand = prefix | lax.shift_left(jnp.int32(1), 31 - it)
            cand_s = cand ^ INT_MIN
            cnt = jnp.zeros((sub, nq), F32)
            for jt in range(ntiles):
                hit = jnp.where(keys_ref[jt] >= cand_s, 1.0, 0.0)
                cnt = cnt + jnp.sum(hit.reshape(kt // sub, sub, nq), axis=0)
            return jnp.where(jnp.sum(cnt, axis=0, keepdims=True) >= top_k, cand, prefix)

        prefix = lax.fori_loop(0, 32, bisect, jnp.zeros((1, nq), jnp.int32))
        tau_ref[...] = jnp.broadcast_to(jnp.maximum(prefix ^ INT_MIN, INT_MIN + 1), tau_ref.shape)

    step = 2 if nkt % 2 == 0 else 1
    for ntiles in range(step, nkt + 1, step):
        @pl.when((n_act > ntiles - step) & (n_act <= ntiles))
        def _():
            bisect_over(ntiles)

    tau = tau_ref[0:1, :]

    def tie_counts(jt, carry):
        gt, eq = carry
        kj = keys_ref[jt]
        gt = gt + jnp.sum(jnp.where(kj > tau, 1.0, 0.0).reshape(kt // sub, sub, nq), axis=0)
        eq = eq + jnp.sum(jnp.where(kj == tau, 1.0, 0.0).reshape(kt // sub, sub, nq), axis=0)
        return gt, eq

    zero8 = jnp.zeros((sub, nq), F32)
    gt8, eq8 = lax.fori_loop(0, n_act, tie_counts, (zero8, zero8))
    need = top_k - jnp.sum(gt8, axis=0, keepdims=True)
    surplus = jnp.sum(eq8, axis=0, keepdims=True) > need

    @pl.when(jnp.max(jnp.where(surplus, 1.0, 0.0)) > 0.0)
    def _():
        before_row = (lax.broadcasted_iota(jnp.int32, (kt, kt), 1)
                      < lax.broadcasted_iota(jnp.int32, (kt, kt), 0))
        tri = jnp.where(before_row, 1.0, 0.0).astype(BF16)

        def drop_surplus(jt, seen):
            kj = keys_ref[jt]
            eq = kj == tau
            eqf = jnp.where(eq, 1.0, 0.0)
            earlier = jnp.dot(tri, eqf.astype(BF16), preferred_element_type=F32) + seen
            keys_ref[jt] = jnp.where(eq & (earlier >= need), INT_MIN, kj)
            return seen + jnp.sum(eqf, axis=0, keepdims=True)

        lax.fori_loop(0, n_act, drop_surplus, jnp.zeros((1, nq), F32))

    rows = DSA_HEADS * nq
    rep = kt // LANES
    def attend_tile(jt, first):
        k0 = pl.multiple_of(jt * kt, kt)
        ck = ckv_ref[0, pl.ds(k0, kt), :]
        bias_ref[...] = jnp.where(keys_ref[jt] >= tau, 0.0, MASKED).T
        s_ref[...] = lax.dot_general(qabs_ref[0].reshape(rows, KV_LORA), ck, tn, preferred_element_type=F32)

        for h in range(DSA_HEADS):
            hr = slice(h * nq, (h + 1) * nq)
            s = s_ref[hr, :] + bias_ref[...]
            m_old = jnp.full((nq, LANES), M_INIT, F32) if first else m_ref[hr, :]
            m_new = jnp.maximum(m_old, jnp.max(s, axis=1, keepdims=True))
            p = jnp.exp2(s - jnp.concatenate([m_new] * rep, axis=1))
            psum = jnp.sum(p, axis=1, keepdims=True)
            if first:
                l_ref[hr, :] = jnp.broadcast_to(psum, (nq, LANES))
            else:
                alpha = jnp.exp2(m_old - m_new)
                l_ref[hr, :] = alpha * l_ref[hr, :] + psum
                acc_ref[hr, :] = acc_ref[hr, :] * jnp.concatenate([alpha] * (KV_LORA // LANES), axis=1)
            m_ref[hr, :] = m_new
            p_ref[hr, :] = p.astype(BF16)
        pv = jnp.dot(p_ref[...], ck, preferred_element_type=F32)
        if first:
            acc_ref[...] = pv
        else:
            acc_ref[...] += pv

    attend_tile(0, True)

    def later_tile(jt, carry):
        attend_tile(jt, False)
        return carry

    lax.fori_loop(1, n_act, later_tile, 0)
    inv_l = 1.0 / l_ref[...]
    out = acc_ref[...] * jnp.concatenate([inv_l] * (KV_LORA // LANES), axis=1)
    o_ref[0] = out.reshape(DSA_HEADS, nq, KV_LORA).astype(o_ref.dtype)


def dsa_core(q_all, widx, kidx, ckv, qabs, b, t, kt=256):
    nblk = t // QBLOCK
    top_k = min(IDX_TOPK, t // 4)
    hw = IDX_HEADS * IDX_DIM
    rows = DSA_HEADS * QBLOCK
    return pl.pallas_call(
        functools.partial(_dsa_core_body, top_k=top_k, kt=kt),
        grid=(b, nblk),
        in_specs=[
            pl.BlockSpec((1, QBLOCK, hw), lambda bi, i: (bi, i, 1)),
            pl.BlockSpec((1, QBLOCK, LANES), lambda bi, i: (bi, i, 0)),
            pl.BlockSpec((1, t, IDX_DIM), lambda bi, i: (bi, 0, 0)),
            pl.BlockSpec((1, t, KV_LORA), lambda bi, i: (bi, 0, 0)),
            pl.BlockSpec((1, DSA_HEADS, QBLOCK, KV_LORA), lambda bi, i: (bi * nblk + i, 0, 0, 0)),
        ],
        out_specs=pl.BlockSpec((1, DSA_HEADS, QBLOCK, KV_LORA), lambda bi, i: (bi * nblk + i, 0, 0, 0)),
        out_shape=jax.ShapeDtypeStruct((b * nblk, DSA_HEADS, QBLOCK, KV_LORA), BF16),
        scratch_shapes=[pltpu.VMEM((t // kt, kt, QBLOCK), jnp.int32), pltpu.VMEM((8, QBLOCK), jnp.int32),
                        pltpu.VMEM((QBLOCK, kt), F32),
                        pltpu.VMEM((rows, kt), F32), pltpu.VMEM((rows, kt), BF16),
                        pltpu.VMEM((rows, LANES), F32), pltpu.VMEM((rows, LANES), F32),
                        pltpu.VMEM((rows, KV_LORA), F32)],
        compiler_params=_params("parallel", "arbitrary"),
        name="dsa_core",
    )(q_all.reshape(b, t, -1), widx.reshape(b, t, LANES), kidx.reshape(b, t, IDX_DIM),
      ckv.reshape(b, t, KV_LORA), qabs)


def dsa_layer(h2, u, b, t, next_gain, w_in, q_norm, kv_norm, kidx_norm, w_uq, w_uk, w_uv, w_out):
    width = Q_LORA + KV_LORA + IDX_DIM + LANES
    w_in_p = jnp.pad(w_in, ((0, 0), (0, width - w_in.shape[1]))).astype(BF16)
    qlat, ckv, kidx, widx = dsa_in_proj(u, w_in_p, q_norm, kv_norm, kidx_norm)
    q_all = matmul(qlat, w_uq.astype(BF16), BF16)
    qabs = q_absorb(q_all, w_uk.astype(BF16))
    o_lat = dsa_core(q_all, widx, kidx, ckv, qabs, b, t)
    o = v_up(o_lat, w_uv.astype(BF16))
    return matmul_res_norm(o, w_out.astype(BF16), h2, next_gain, BF16)


def mlp(h2, u, next_gain, u_dtype, w_up, w_down):
    a = matmul(u, w_up.astype(BF16), BF16, relu2=True, tn=2048)
    return matmul_res_norm(a, w_down.astype(BF16), h2, next_gain, u_dtype)


def kernel(x, norm_mix, norm_mlp, norm_final, dn_w_in, dn_conv_w, dn_a_log, dn_dt_bias, dn_out_norm, dn_w_out, dsa_w_in, dsa_q_norm, dsa_kv_norm, dsa_kidx_norm, dsa_w_uq, dsa_w_uk, dsa_w_uv, dsa_w_out, mlp_w_up, mlp_w_down):
    b, t, d = x.shape
    depth = norm_mix.shape[0]
    h2 = x.reshape(b * t, d)
    u = rms_norm(h2, norm_mix[0], BF16)
    for i in range(depth):
        j = i // 2
        if i % 2 == 0:
            h2, u = gated_deltanet_layer(h2, u, b, t, norm_mlp[i], dn_w_in[j], dn_conv_w[j], dn_a_log[j],
                                         dn_dt_bias[j], dn_out_norm[j], dn_w_out[j])
        else:
            h2, u = dsa_layer(h2, u, b, t, norm_mlp[i], dsa_w_in[j], dsa_q_norm[j], dsa_kv_norm[j],
                              dsa_kidx_norm[j], dsa_w_uq[j], dsa_w_uk[j], dsa_w_uv[j], dsa_w_out[j])
        last = i == depth - 1
        h2, u = mlp(h2, u, norm_final if last else norm_mix[i + 1], x.dtype if last else BF16,
                    mlp_w_up[i], mlp_w_down[i])
    return u.reshape(b, t, d)
```

```python
import functools

import jax
import jax.numpy as jnp
from jax import lax
from jax.experimental import pallas as pl
from jax.experimental.pallas import tpu as pltpu

F32 = jnp.float32
BF16 = jnp.bfloat16
EPS = 1e-6
CHUNK = 64
QBLOCK = 128
DN_QK_HEADS = 16
DN_V_HEADS = 32
DN_HEAD_DIM = 128
DN_KEY_DIM = DN_QK_HEADS * DN_HEAD_DIM
DN_VAL_DIM = DN_V_HEADS * DN_HEAD_DIM
DN_CONV_DIM = 2 * DN_KEY_DIM + DN_VAL_DIM
DSA_HEADS = 16
DSA_HEAD_DIM = 128
Q_LORA = 512
KV_LORA = 256
IDX_HEADS = 16
IDX_DIM = 128
IDX_TOPK = 256
LANES = 128
VMEM_LIMIT_BYTES = 56 * 1024 * 1024
INT_MIN = -(2 ** 31)
MASKED = -1e30
M_INIT = -1e20
LOG2_E = 1.4426950408889634


def _params(*sem):
    return pltpu.CompilerParams(dimension_semantics=sem, vmem_limit_bytes=VMEM_LIMIT_BYTES)


def _sigmoid(x):
    return 1.0 / (1.0 + jnp.exp2(x * (-LOG2_E)))


def _norm_body(x_ref, g_ref, o_ref):
    x = x_ref[...]
    y = x * lax.rsqrt(jnp.mean(x * x, axis=-1, keepdims=True) + EPS) * g_ref[...]
    o_ref[...] = y.astype(o_ref.dtype)


def rms_norm(x, g, out_dtype, tm=512):
    n, d = x.shape
    return pl.pallas_call(
        _norm_body,
        grid=(n // tm,),
        in_specs=[pl.BlockSpec((tm, d), lambda i: (i, 0)), pl.BlockSpec((1, d), lambda i: (0, 0))],
        out_specs=pl.BlockSpec((tm, d), lambda i: (i, 0)),
        out_shape=jax.ShapeDtypeStruct((n, d), out_dtype),
        compiler_params=_params("parallel"),
        name="rms_norm",
    )(x, g.reshape(1, d))


MM_SUB_COLS = 512


def _mm_body(a_ref, w_ref, *rest, nk, relu2, residual):
    if residual:
        r_ref, o_ref = rest[0], rest[1]
    else:
        r_ref, o_ref = None, rest[0]
    acc_ref = rest[-1] if nk > 1 else None

    def finish(acc, cols=slice(None)):
        if relu2:
            acc = jnp.square(jnp.maximum(acc, 0.0))
        if residual:
            acc = acc + r_ref[:, cols]
        o_ref[:, cols] = acc.astype(o_ref.dtype)

    sub = min(MM_SUB_COLS, o_ref.shape[1])
    blocks = [slice(c0, c0 + sub) for c0 in range(0, o_ref.shape[1], sub)]

    def k_step(first, last):
        for cols in blocks:
            part = jnp.dot(a_ref[...], w_ref[:, cols], preferred_element_type=F32)
            if not first:
                part = part + acc_ref[:, cols]
            if last:
                finish(part, cols)
            else:
                acc_ref[:, cols] = part

    if nk == 1:
        k_step(True, True)
        return
    k = pl.program_id(2)
    pl.when(k == 0)(functools.partial(k_step, True, False))
    pl.when((k > 0) & (k < nk - 1))(functools.partial(k_step, False, False))
    pl.when(k == nk - 1)(functools.partial(k_step, False, True))


def matmul(a, w, out_dtype, *, relu2=False, residual=None, tm=1024, tn=1024, tk=2048):
    m, kdim = a.shape
    n = w.shape[1]
    tm, tn, tk = min(tm, m), min(tn, n), min(tk, kdim)
    assert m % tm == 0 and n % tn == 0 and kdim % tk == 0
    nk = kdim // tk
    in_specs = [pl.BlockSpec((tm, tk), lambda i, j, k: (i, k)), pl.BlockSpec((tk, tn), lambda i, j, k: (k, j))]
    args = [a, w]
    if residual is not None:
        in_specs.append(pl.BlockSpec((tm, tn), lambda i, j, k: (i, j)))
        args.append(residual)
    return pl.pallas_call(
        functools.partial(_mm_body, nk=nk, relu2=relu2, residual=residual is not None),
        grid=(m // tm, n // tn, nk),
        in_specs=in_specs,
        out_specs=pl.BlockSpec((tm, tn), lambda i, j, k: (i, j)),
        out_shape=jax.ShapeDtypeStruct((m, n), out_dtype),
        scratch_shapes=[pltpu.VMEM((tm, tn), F32)] if nk > 1 else [],
        compiler_params=_params("parallel", "parallel", "arbitrary"),
        name="matmul",
    )(*args)


def _mm_res_norm_body(a_ref, w_ref, r_ref, g_ref, h_ref, u_ref, *scratch, nk):
    acc_ref = scratch[0] if nk > 1 else None
    n = h_ref.shape[1]
    sub = min(MM_SUB_COLS, n)
    blocks = [slice(c0, c0 + sub) for c0 in range(0, n, sub)]

    def k_step(first, last):
        for cols in blocks:
            part = jnp.dot(a_ref[...], w_ref[:, cols], preferred_element_type=F32)
            if not first:
                part = part + acc_ref[:, cols]
            if last:
                h_ref[:, cols] = part + r_ref[:, cols]
            else:
                acc_ref[:, cols] = part
        if last:
            h = h_ref[...]
            scale = lax.rsqrt(jnp.mean(h * h, axis=-1, keepdims=True) + EPS)
            u_ref[...] = (h * scale * g_ref[...]).astype(u_ref.dtype)

    if nk == 1:
        k_step(True, True)
        return
    k = pl.program_id(1)
    pl.when(k == 0)(functools.partial(k_step, True, False))
    pl.when((k > 0) & (k < nk - 1))(functools.partial(k_step, False, False))
    pl.when(k == nk - 1)(functools.partial(k_step, False, True))


def matmul_res_norm(a, w, residual, gain, u_dtype, *, tm=512, tk=2048):
    m, kdim = a.shape
    n = w.shape[1]
    tm, tk = min(tm, m), min(tk, kdim)
    assert m % tm == 0 and kdim % tk == 0
    nk = kdim // tk
    row = pl.BlockSpec((tm, n), lambda i, k: (i, 0))
    return pl.pallas_call(
        functools.partial(_mm_res_norm_body, nk=nk),
        grid=(m // tm, nk),
        in_specs=[pl.BlockSpec((tm, tk), lambda i, k: (i, k)), pl.BlockSpec((tk, n), lambda i, k: (k, 0)),
                  row, pl.BlockSpec((1, n), lambda i, k: (0, 0))],
        out_specs=[row, row],
        out_shape=[jax.ShapeDtypeStruct((m, n), F32), jax.ShapeDtypeStruct((m, n), u_dtype)],
        scratch_shapes=[pltpu.VMEM((tm, n), F32)] if nk > 1 else [],
        compiler_params=_params("parallel", "arbitrary"),
        name="matmul_res_norm",
    )(a, w, residual, gain.reshape(1, n))


def _dn_gates_body(u_ref, w_ref, alog_ref, dt_ref, o_ref):
    x = jnp.dot(u_ref[...], w_ref[...], preferred_element_type=F32)
    g = (-LOG2_E) * jnp.exp(alog_ref[...]) * (jnp.maximum(x + dt_ref[...], 0.0)
                                              + jnp.log(1.0 + jnp.exp(-jnp.abs(x + dt_ref[...]))))
    pos = lax.broadcasted_iota(jnp.int32, x.shape, 0) % CHUNK
    s = 1
    while s < CHUNK:
        g = g + jnp.where(pos >= s, pltpu.roll(g, s, axis=0), 0.0)
        s *= 2
    lane = lax.broadcasted_iota(jnp.int32, x.shape, 1)
    o_ref[...] = jnp.where(lane < DN_V_HEADS, _sigmoid(x), g)


def dn_gates(u, w_ba, alog_pad, dt_pad, tm=512):
    n, d = u.shape
    return pl.pallas_call(
        _dn_gates_body,
        grid=(n // tm,),
        in_specs=[pl.BlockSpec((tm, d), lambda i: (i, 0)),
                  pl.BlockSpec((d, LANES), lambda i: (0, 0)),
                  pl.BlockSpec((1, LANES), lambda i: (0, 0)),
                  pl.BlockSpec((1, LANES), lambda i: (0, 0))],
        out_specs=pl.BlockSpec((tm, LANES), lambda i: (i, 0)),
        out_shape=jax.ShapeDtypeStruct((n, LANES), F32),
        compiler_params=_params("parallel"),
        name="dn_gates",
    )(u, w_ba, alog_pad, dt_pad)


def _conv_silu(ref, cw_ref, t0, tt):
    cw = cw_ref[...]
    acc = ref[0, t0:t0 + tt, :] * cw[3:4, :]
    for s in (1, 2, 3):
        if t0 == 0:
            x = ref[0, 0:tt, :]
            rows = lax.broadcasted_iota(jnp.int32, x.shape, 0)
            xs = jnp.where(rows >= s, pltpu.roll(x, s, axis=0), 0.0)
        else:
            xs = ref[0, t0 - s:t0 - s + tt, :]
        acc = acc + xs * cw[3 - s:4 - s, :]
    return acc * _sigmoid(acc)


def _l2norm(t, scale=1.0):
    return t * (lax.rsqrt(jnp.sum(t * t, axis=-1, keepdims=True) + EPS) * scale)


def _bmm(a, b):
    return jnp.einsum("nij,njk->nik", a.astype(BF16), b.astype(BF16), preferred_element_type=F32)


def _bmm_nt(a, b):
    return jnp.einsum("nid,njd->nij", a.astype(BF16), b.astype(BF16), preferred_element_type=F32)


DN_GROUP = 32


def _dn_prep_body(q_ref, k_ref, v_ref, cwq_ref, cwk_ref, cwv_ref, gates_ref, gcr_ref,
                  u_ref, w_ref, qd_ref, kd_ref, a_ref, qs_ref, ks_ref, vs_ref, bb_ref, gb_ref, *, t_len):
    j = pl.program_id(1)
    tt = 256
    hd = DN_HEAD_DIM
    for ti in range(t_len // tt):
        t0 = ti * tt
        qs_ref[t0:t0 + tt, :] = _l2norm(_conv_silu(q_ref, cwq_ref, t0, tt), hd ** -0.5)
        ks_ref[t0:t0 + tt, :] = _l2norm(_conv_silu(k_ref, cwk_ref, t0, tt))
        vs_ref[t0:t0 + tt, :] = _conv_silu(v_ref, cwv_ref, t0, tt)
        gt = gates_ref[0, t0:t0 + tt, :]
        lane = lax.broadcasted_iota(jnp.int32, gt.shape, 1)
        for hh in range(2):
            col = 2 * j + hh
            bsel = jnp.sum(jnp.where(lane == col, gt, 0.0), axis=1, keepdims=True)
            gsel = jnp.sum(jnp.where(lane == col + DN_V_HEADS, gt, 0.0), axis=1, keepdims=True)
            bb_ref[hh, t0:t0 + tt, :] = jnp.broadcast_to(bsel, (tt, LANES))
            gb_ref[hh, t0:t0 + tt, :] = jnp.broadcast_to(gsel, (tt, LANES))

    c = CHUNK
    g = min(DN_GROUP, t_len // c)
    rows = g * c
    ri = lax.broadcasted_iota(jnp.int32, (c, c), 0)
    ci = lax.broadcasted_iota(jnp.int32, (c, c), 1)
    lower = ri >= ci
    strict = ri > ci
    eye = jnp.where(ri == ci, 1.0, 0.0)
    pair_mask = strict & ((ri // 2) == (ci // 2))
    merge_masks = [((ri // (2 * s)) == (ci // (2 * s))) & ((ri // s) != (ci // s)) & strict
                   for s in (2, 4, 8, 16, 32)]

    def group_step(gi, carry):
        base = pl.multiple_of(gi * rows, rows)
        kc = ks_ref[pl.ds(base, rows), :].reshape(g, c, hd)
        qc = qs_ref[pl.ds(base, rows), :].reshape(g, c, hd)
        kk1 = _bmm_nt(kc, kc)
        qk1 = _bmm_nt(qc, kc)
        two = lambda x: jnp.concatenate([x, x], axis=0)
        kk, qk, kc2, qc2 = two(kk1), two(qk1), two(kc), two(qc)
        beta_b = jnp.concatenate([bb_ref[hh, pl.ds(base, rows), :].reshape(g, c, LANES) for hh in range(2)], axis=0)
        gcc_b = jnp.concatenate([gb_ref[hh, pl.ds(base, rows), :].reshape(g, c, LANES) for hh in range(2)], axis=0)
        grow = gcr_ref[0, 0, :, pl.ds(base, rows)]
        gcr = jnp.stack([grow[hh:hh + 1, n * c:(n + 1) * c] for hh in range(2) for n in range(g)], axis=0)
        vc = jnp.concatenate([vs_ref[pl.ds(base, rows), hh * hd:(hh + 1) * hd].reshape(g, c, hd)
                              for hh in range(2)], axis=0)
        gl_b = gcc_b[:, c - 1:c, :]
        dm = gcc_b[:, :, :c] - gcr
        decay = jnp.where(lower, jnp.exp2(dm), 0.0)
        lm = jnp.where(strict, kk * beta_b[:, :, :c] * decay, 0.0)
        tb = (eye - jnp.where(pair_mask, lm, 0.0)).astype(BF16)
        lmb = lm.astype(BF16)
        zero_b = jnp.zeros_like(lmb)
        for mk in merge_masks:
            tb = tb - _bmm(_bmm(tb, jnp.where(mk, lmb, zero_b)), tb).astype(BF16)
        eg = jnp.exp2(gcc_b)
        uw = _bmm(tb, jnp.concatenate([vc * beta_b, kc2 * (beta_b * eg)], axis=2))
        qd = qc2 * eg
        kd = kc2 * jnp.exp2(gl_b - gcc_b)
        am = qk * decay
        for hh in range(2):
            cols = slice(hh * hd, (hh + 1) * hd)
            inst = slice(hh * g, (hh + 1) * g)
            u_ref[0, pl.ds(base, rows), cols] = uw[inst, :, :hd].reshape(rows, hd).astype(u_ref.dtype)
            w_ref[0, pl.ds(base, rows), cols] = uw[inst, :, hd:].reshape(rows, hd).astype(w_ref.dtype)
            qd_ref[0, pl.ds(base, rows), cols] = qd[inst].reshape(rows, hd).astype(qd_ref.dtype)
            kd_ref[0, pl.ds(base, rows), cols] = kd[inst].reshape(rows, hd).astype(kd_ref.dtype)
            a_ref[0, pl.ds(base, rows), hh * c:(hh + 1) * c] = am[inst].reshape(rows, c).astype(a_ref.dtype)
        return carry

    lax.fori_loop(0, t_len // rows, group_step, 0)


def dn_prep(proj, conv_w, gates, gcr):
    b, t, _ = proj.shape
    hd = DN_HEAD_DIM
    nq = DN_QK_HEADS
    v_blk0 = 2 * DN_KEY_DIM // (2 * hd)
    wide = pl.BlockSpec((1, t, 2 * hd), lambda bi, j: (bi, 0, j))
    big = jax.ShapeDtypeStruct((b, t, DN_VAL_DIM), BF16)
    return pl.pallas_call(
        functools.partial(_dn_prep_body, t_len=t),
        grid=(b, nq),
        in_specs=[
            pl.BlockSpec((1, t, hd), lambda bi, j: (bi, 0, j)),
            pl.BlockSpec((1, t, hd), lambda bi, j: (bi, 0, nq + j)),
            pl.BlockSpec((1, t, 2 * hd), lambda bi, j: (bi, 0, v_blk0 + j)),
            pl.BlockSpec((4, hd), lambda bi, j: (0, j)),
            pl.BlockSpec((4, hd), lambda bi, j: (0, nq + j)),
            pl.BlockSpec((4, 2 * hd), lambda bi, j: (0, v_blk0 + j)),
            pl.BlockSpec((1, t, LANES), lambda bi, j: (bi, 0, 0)),
            pl.BlockSpec((1, 1, 2, t), lambda bi, j: (bi, j, 0, 0)),
        ],
        out_specs=[wide, wide, wide, wide, pl.BlockSpec((1, t, 2 * CHUNK), lambda bi, j: (bi, 0, j))],
        out_shape=[big, big, big, big, jax.ShapeDtypeStruct((b, t, DN_V_HEADS * CHUNK), BF16)],
        scratch_shapes=[
            pltpu.VMEM((t, hd), F32), pltpu.VMEM((t, hd), F32), pltpu.VMEM((t, 2 * hd), F32),
            pltpu.VMEM((2, t, LANES), F32), pltpu.VMEM((2, t, LANES), F32),
        ],
        compiler_params=_params("parallel", "arbitrary"),
        name="dn_prep",
    )(proj, proj, proj, conv_w, conv_w, conv_w, gates, gcr)


DN_REC_HEADS = 32
DN_REC_ROWS = 256


def _dn_rec_body(u_ref, w_ref, qd_ref, kd_ref, a_ref, z_ref, gl_ref, onorm_ref, o_ref, s_ref, *, t_len):
    hd = DN_HEAD_DIM
    c = CHUNK

    @pl.when(pl.program_id(2) == 0)
    def _():
        s_ref[...] = jnp.zeros_like(s_ref)

    onorm = onorm_ref[...]

    def step(ci, carry):
        r0 = pl.multiple_of(ci * c, c)
        egl = jnp.exp2(gl_ref[0, 0, ci])
        heads = range(DN_REC_HEADS)
        cols = [slice(h * hd, (h + 1) * hd) for h in heads]
        st = [s_ref[h] for h in heads]
        r = [jnp.dot(jnp.concatenate([w_ref[0, pl.ds(r0, c), cols[h]], qd_ref[0, pl.ds(r0, c), cols[h]]], axis=0),
                     st[h].astype(BF16), preferred_element_type=F32) for h in heads]
        vb = [(u_ref[0, pl.ds(r0, c), cols[h]].astype(F32) - r[h][:c]).astype(BF16) for h in heads]
        upd = [lax.dot_general(kd_ref[0, pl.ds(r0, c), cols[h]], vb[h], (((0,), (0,)), ((), ())),
                               preferred_element_type=F32) for h in heads]
        o = [r[h][c:] + jnp.dot(a_ref[0, pl.ds(r0, c), h * c:(h + 1) * c], vb[h], preferred_element_type=F32)
             for h in heads]
        for h in heads:
            s_ref[h] = st[h] * egl[:, h:h + 1] + upd[h]
        for h in heads:
            y = o[h] * lax.rsqrt(jnp.mean(o[h] * o[h], axis=-1, keepdims=True) + EPS) * onorm
            zz = z_ref[0, pl.ds(r0, c), cols[h]]
            o_ref[0, pl.ds(r0, c), cols[h]] = (y * (zz * _sigmoid(zz))).astype(o_ref.dtype)
        return carry

    lax.fori_loop(0, t_len // c, step, 0)


def dn_rec(u, w, qd, kd, a, proj, gl, out_norm):
    b, t, _ = u.shape
    hd = DN_HEAD_DIM
    hb = DN_REC_HEADS
    ng = DN_V_HEADS // hb
    tt = min(DN_REC_ROWS, t)
    z_blk0 = DN_CONV_DIM // (hb * hd)
    wide = pl.BlockSpec((1, tt, hb * hd), lambda bi, g, ti: (bi, ti, g))
    return pl.pallas_call(
        functools.partial(_dn_rec_body, t_len=tt),
        grid=(b, ng, t // tt),
        in_specs=[
            wide, wide, wide, wide,
            pl.BlockSpec((1, tt, hb * CHUNK), lambda bi, g, ti: (bi, ti, g)),
            pl.BlockSpec((1, tt, hb * hd), lambda bi, g, ti: (bi, ti, z_blk0 + g)),
            pl.BlockSpec((1, 1, tt // CHUNK, 1, hb), lambda bi, g, ti: (bi, g, ti, 0, 0)),
            pl.BlockSpec((1, hd), lambda bi, g, ti: (0, 0)),
        ],
        out_specs=wide,
        out_shape=jax.ShapeDtypeStruct((b, t, DN_VAL_DIM), BF16),
        scratch_shapes=[pltpu.VMEM((hb, hd, hd), F32)],
        compiler_params=_params("parallel", "parallel", "arbitrary"),
        name="dn_rec",
    )(u, w, qd, kd, a, proj, gl, out_norm.reshape(1, hd))


def gated_deltanet_layer(h2, u, b, t, next_gain, w_in, conv_w, a_log, dt_bias, out_norm, w_out):
    n = b * t
    n_qkvz = DN_CONV_DIM + DN_VAL_DIM
    proj = matmul(u, w_in[:, :n_qkvz].astype(BF16), F32, tn=2048).reshape(b, t, n_qkvz)
    w_ba = jnp.pad(w_in[:, n_qkvz:], ((0, 0), (0, LANES - 2 * DN_V_HEADS))).astype(BF16)
    pad = lambda p: jnp.pad(p.astype(F32), (DN_V_HEADS, LANES - 2 * DN_V_HEADS)).reshape(1, LANES)
    gates = dn_gates(u, w_ba, pad(a_log), pad(dt_bias))
    gc = gates[:, DN_V_HEADS:2 * DN_V_HEADS].reshape(b, t, DN_V_HEADS)
    gcr = gc.reshape(b, t, DN_QK_HEADS, 2).transpose(0, 2, 3, 1)
    ng = DN_V_HEADS // DN_REC_HEADS
    gl = gc[:, CHUNK - 1::CHUNK, :].reshape(b, t // CHUNK, ng, 1, DN_REC_HEADS).transpose(0, 2, 1, 3, 4)
    uu, ww, qd, kd, am = dn_prep(proj, conv_w, gates.reshape(b, t, LANES), gcr)
    o = dn_rec(uu, ww, qd, kd, am, proj, gl, out_norm)
    return matmul_res_norm(o.reshape(n, DN_VAL_DIM), w_out.astype(BF16), h2, next_gain, BF16)


def _dsa_in_body(u_ref, w_ref, qn_ref, kvn_ref, kin_ref, q_out, kv_out, ki_out, w_out):
    def nrm(x, g):
        return x * lax.rsqrt(jnp.mean(x * x, axis=-1, keepdims=True) + EPS) * g

    a, b2, c2 = Q_LORA, Q_LORA + KV_LORA, Q_LORA + KV_LORA + IDX_DIM
    dot = lambda lo, hi: jnp.dot(u_ref[...], w_ref[:, lo:hi], preferred_element_type=F32)
    q_out[...] = nrm(dot(0, a), qn_ref[...]).astype(q_out.dtype)
    kv_out[...] = nrm(dot(a, b2), kvn_ref[...]).astype(kv_out.dtype)
    ki_out[...] = nrm(dot(b2, c2), kin_ref[...]).astype(ki_out.dtype)
    w_out[...] = dot(c2, c2 + LANES) * (IDX_HEADS ** -0.5 * IDX_DIM ** -0.5)


def dsa_in_proj(u, w_in_p, q_norm, kv_norm, kidx_norm, tm=1024):
    n, d = u.shape
    width = w_in_p.shape[1]
    row = lambda i: (i, 0)
    fix = lambda i: (0, 0)
    return pl.pallas_call(
        _dsa_in_body,
        grid=(n // tm,),
        in_specs=[pl.BlockSpec((tm, d), row), pl.BlockSpec((d, width), fix), pl.BlockSpec((1, Q_LORA), fix),
                  pl.BlockSpec((1, KV_LORA), fix), pl.BlockSpec((1, IDX_DIM), fix)],
        out_specs=[pl.BlockSpec((tm, Q_LORA), row), pl.BlockSpec((tm, KV_LORA), row),
                   pl.BlockSpec((tm, IDX_DIM), row), pl.BlockSpec((tm, LANES), row)],
        out_shape=[jax.ShapeDtypeStruct((n, Q_LORA), BF16), jax.ShapeDtypeStruct((n, KV_LORA), BF16),
                   jax.ShapeDtypeStruct((n, IDX_DIM), BF16), jax.ShapeDtypeStruct((n, LANES), F32)],
        compiler_params=_params("parallel"),
        name="dsa_in_proj",
    )(u, w_in_p, q_norm.reshape(1, -1), kv_norm.reshape(1, -1), kidx_norm.reshape(1, -1))


def _qabs_body(q_ref, w_ref, o_ref, *, r):
    for h in range(DSA_HEADS):
        res = jnp.dot(q_ref[:, h * DSA_HEAD_DIM:(h + 1) * DSA_HEAD_DIM], w_ref[h],
                      preferred_element_type=F32) * (DSA_HEAD_DIM ** -0.5 * LOG2_E)
        for rr in range(r):
            o_ref[rr, h] = res[rr * QBLOCK:(rr + 1) * QBLOCK].astype(o_ref.dtype)


def q_absorb(q_all, w_uk, tm=1024):
    n = q_all.shape[0]
    r = tm // QBLOCK
    hw = DSA_HEADS * DSA_HEAD_DIM
    return pl.pallas_call(
        functools.partial(_qabs_body, r=r),
        grid=(n // tm,),
        in_specs=[pl.BlockSpec((tm, hw), lambda i: (i, 0)),
                  pl.BlockSpec((DSA_HEADS, DSA_HEAD_DIM, KV_LORA), lambda i: (0, 0, 0))],
        out_specs=pl.BlockSpec((r, DSA_HEADS, QBLOCK, KV_LORA), lambda i: (i, 0, 0, 0)),
        out_shape=jax.ShapeDtypeStruct((n // QBLOCK, DSA_HEADS, QBLOCK, KV_LORA), BF16),
        compiler_params=_params("parallel"),
        name="q_absorb",
    )(q_all, w_uk)


def _vup_body(o_ref, w_ref, out_ref, *, r):
    for h in range(DSA_HEADS):
        x = o_ref[:, h].reshape(r * QBLOCK, KV_LORA)
        out_ref[:, h * DSA_HEAD_DIM:(h + 1) * DSA_HEAD_DIM] = jnp.dot(
            x, w_ref[h], preferred_element_type=F32).astype(out_ref.dtype)


def v_up(o_lat, w_uv, tm=1024):
    nb = o_lat.shape[0]
    r = tm // QBLOCK
    hw = DSA_HEADS * DSA_HEAD_DIM
    return pl.pallas_call(
        functools.partial(_vup_body, r=r),
        grid=(nb // r,),
        in_specs=[pl.BlockSpec((r, DSA_HEADS, QBLOCK, KV_LORA), lambda i: (i, 0, 0, 0)),
                  pl.BlockSpec((DSA_HEADS, KV_LORA, DSA_HEAD_DIM), lambda i: (0, 0, 0))],
        out_specs=pl.BlockSpec((tm, hw), lambda i: (i, 0)),
        out_shape=jax.ShapeDtypeStruct((nb * QBLOCK, hw), BF16),
        compiler_params=_params("parallel"),
        name="v_up",
    )(o_lat, w_uv)


def _dsa_core_body(qidx_ref, widx_ref, kidx_ref, ckv_ref, qabs_ref, o_ref,
                   keys_ref, tau_ref, bias_ref, s_ref, p_ref, m_ref, l_ref, acc_ref, *, top_k, kt):
    i = pl.program_id(1)
    nq = QBLOCK
    nkt = keys_ref.shape[0]
    n_act = ((i + 1) * nq + kt - 1) // kt
    wt = widx_ref[0].T
    krow = lax.broadcasted_iota(jnp.int32, (kt, nq), 0)
    qcol = lax.broadcasted_iota(jnp.int32, (kt, nq), 1)
    q_chunk = (i * nq + qcol) // CHUNK
    tn = (((1,), (1,)), ((), ()))
    sub = 8

    def index_tile(jt, carry):
        k0 = pl.multiple_of(jt * kt, kt)
        kblk = kidx_ref[0, pl.ds(k0, kt), :]
        sc = jnp.zeros((kt, nq), F32)
        for h in range(IDX_HEADS):
            d = lax.dot_general(kblk, qidx_ref[0, :, h * IDX_DIM:(h + 1) * IDX_DIM], tn,
                                preferred_element_type=F32)
            sc = sc + jnp.maximum(d, 0.0) * wt[h:h + 1, :]
        sc = jnp.where(sc == 0.0, 0.0, sc)
        bits = pltpu.bitcast(sc, jnp.int32)
        key = bits ^ ((bits >> 31) & 0x7FFFFFFF)
        adm = ((k0 + krow) // CHUNK) <= q_chunk
        keys_ref[jt] = jnp.where(adm, key, INT_MIN)
        return carry

    lax.fori_loop(0, n_act, index_tile, 0)

    for jt in range(nkt):
        @pl.when(jt >= n_act)
        def _():
            keys_ref[jt] = jnp.full((kt, nq), INT_MIN, jnp.int32)

    def bisect_over(ntiles):
        def bisect(it, prefix):
            cand = prefix | lax.shift_left(jnp.int32(1), 31 - it)
            cand_s = cand ^ INT_MIN
            cnt = jnp.zeros((sub, nq), F32)
            for jt in range(ntiles):
                hit = jnp.where(keys_ref[jt] >= cand_s, 1.0, 0.0)
                cnt = cnt + jnp.sum(hit.reshape(kt // sub, sub, nq), axis=0)
            return jnp.where(jnp.sum(cnt, axis=0, keepdims=True) >= top_k, cand, prefix)

        prefix = lax.fori_loop(0, 32, bisect, jnp.zeros((1, nq), jnp.int32))
        tau_ref[...] = jnp.broadcast_to(jnp.maximum(prefix ^ INT_MIN, INT_MIN + 1), tau_ref.shape)

    search = (i + 1) * nq > top_k
    step = 2 if nkt % 2 == 0 else 1
    for ntiles in range(step, nkt + 1, step):
        @pl.when(search & (n_act > ntiles - step) & (n_act <= ntiles))
        def _():
            bisect_over(ntiles)

    @pl.when(jnp.logical_not(search))
    def _():
        tau_ref[...] = jnp.full(tau_ref.shape, INT_MIN + 1, jnp.int32)

    tau = tau_ref[0:1, :]

    def tie_counts(jt, carry):
        gt, eq = carry
        kj = keys_ref[jt]
        gt = gt + jnp.sum(jnp.where(kj > tau, 1.0, 0.0).reshape(kt // sub, sub, nq), axis=0)
        eq = eq + jnp.sum(jnp.where(kj == tau, 1.0, 0.0).reshape(kt // sub, sub, nq), axis=0)
        return gt, eq

    zero8 = jnp.zeros((sub, nq), F32)
    gt8, eq8 = lax.fori_loop(0, n_act, tie_counts, (zero8, zero8))
    need = top_k - jnp.sum(gt8, axis=0, keepdims=True)
    surplus = jnp.sum(eq8, axis=0, keepdims=True) > need

    @pl.when(jnp.max(jnp.where(surplus, 1.0, 0.0)) > 0.0)
    def _():
        before_row = (lax.broadcasted_iota(jnp.int32, (kt, kt), 1)
                      < lax.broadcasted_iota(jnp.int32, (kt, kt), 0))
        tri = jnp.where(before_row, 1.0, 0.0).astype(BF16)

        def drop_surplus(jt, seen):
            kj = keys_ref[jt]
            eq = kj == tau
            eqf = jnp.where(eq, 1.0, 0.0)
            earlier = jnp.dot(tri, eqf.astype(BF16), preferred_element_type=F32) + seen
            keys_ref[jt] = jnp.where(eq & (earlier >= need), INT_MIN, kj)
            return seen + jnp.sum(eqf, axis=0, keepdims=True)

        lax.fori_loop(0, n_act, drop_surplus, jnp.zeros((1, nq), F32))

    rows = DSA_HEADS * nq
    rep = kt // LANES
    def attend_tile(jt, first):
        k0 = pl.multiple_of(jt * kt, kt)
        ck = ckv_ref[0, pl.ds(k0, kt), :]
        bias_ref[...] = jnp.where(keys_ref[jt] >= tau, 0.0, MASKED).T
        s_ref[...] = lax.dot_general(qabs_ref[0].reshape(rows, KV_LORA), ck, tn, preferred_element_type=F32)

        for h in range(DSA_HEADS):
            hr = slice(h * nq, (h + 1) * nq)
            s = s_ref[hr, :] + bias_ref[...]
            m_old = jnp.full((nq, LANES), M_INIT, F32) if first else m_ref[hr, :]
            m_new = jnp.maximum(m_old, jnp.max(s, axis=1, keepdims=True))
            p = jnp.exp2(s - jnp.concatenate([m_new] * rep, axis=1))
            psum = jnp.sum(p, axis=1, keepdims=True)
            if first:
                l_ref[hr, :] = jnp.broadcast_to(psum, (nq, LANES))
            else:
                alpha = jnp.exp2(m_old - m_new)
                l_ref[hr, :] = alpha * l_ref[hr, :] + psum
                acc_ref[hr, :] = acc_ref[hr, :] * jnp.concatenate([alpha] * (KV_LORA // LANES), axis=1)
            m_ref[hr, :] = m_new
            p_ref[hr, :] = p.astype(BF16)
        pv = jnp.dot(p_ref[...], ck, preferred_element_type=F32)
        if first:
            acc_ref[...] = pv
        else:
            acc_ref[...] += pv

    attend_tile(0, True)

    def later_tile(jt, carry):
        attend_tile(jt, False)
        return carry

    lax.fori_loop(1, n_act, later_tile, 0)
    inv_l = 1.0 / l_ref[...]
    out = acc_ref[...] * jnp.concatenate([inv_l] * (KV_LORA // LANES), axis=1)
    o_ref[0] = out.reshape(DSA_HEADS, nq, KV_LORA).astype(o_ref.dtype)


def dsa_core(q_all, widx, kidx, ckv, qabs, b, t, kt=256):
    nblk = t // QBLOCK
    top_k = min(IDX_TOPK, t // 4)
    hw = IDX_HEADS * IDX_DIM
    rows = DSA_HEADS * QBLOCK
    return pl.pallas_call(
        functools.partial(_dsa_core_body, top_k=top_k, kt=kt),
        grid=(b, nblk),
        in_specs=[
            pl.BlockSpec((1, QBLOCK, hw), lambda bi, i: (bi, i, 1)),
            pl.BlockSpec((1, QBLOCK, LANES), lambda bi, i: (bi, i, 0)),
            pl.BlockSpec((1, t, IDX_DIM), lambda bi, i: (bi, 0, 0)),
            pl.BlockSpec((1, t, KV_LORA), lambda bi, i: (bi, 0, 0)),
            pl.BlockSpec((1, DSA_HEADS, QBLOCK, KV_LORA), lambda bi, i: (bi * nblk + i, 0, 0, 0)),
        ],
        out_specs=pl.BlockSpec((1, DSA_HEADS, QBLOCK, KV_LORA), lambda bi, i: (bi * nblk + i, 0, 0, 0)),
        out_shape=jax.ShapeDtypeStruct((b * nblk, DSA_HEADS, QBLOCK, KV_LORA), BF16),
        scratch_shapes=[pltpu.VMEM((t // kt, kt, QBLOCK), jnp.int32), pltpu.VMEM((8, QBLOCK), jnp.int32),
                        pltpu.VMEM((QBLOCK, kt), F32),
                        pltpu.VMEM((rows, kt), F32), pltpu.VMEM((rows, kt), BF16),
                        pltpu.VMEM((rows, LANES), F32), pltpu.VMEM((rows, LANES), F32),
                        pltpu.VMEM((rows, KV_LORA), F32)],
        compiler_params=_params("parallel", "arbitrary"),
        name="dsa_core",
    )(q_all.reshape(b, t, -1), widx.reshape(b, t, LANES), kidx.reshape(b, t, IDX_DIM),
      ckv.reshape(b, t, KV_LORA), qabs)


def dsa_layer(h2, u, b, t, next_gain, w_in, q_norm, kv_norm, kidx_norm, w_uq, w_uk, w_uv, w_out):
    width = Q_LORA + KV_LORA + IDX_DIM + LANES
    w_in_p = jnp.pad(w_in, ((0, 0), (0, width - w_in.shape[1]))).astype(BF16)
    qlat, ckv, kidx, widx = dsa_in_proj(u, w_in_p, q_norm, kv_norm, kidx_norm)
    q_all = matmul(qlat, w_uq.astype(BF16), BF16)
    qabs = q_absorb(q_all, w_uk.astype(BF16))
    o_lat = dsa_core(q_all, widx, kidx, ckv, qabs, b, t)
    o = v_up(o_lat, w_uv.astype(BF16))
    return matmul_res_norm(o, w_out.astype(BF16), h2, next_gain, BF16)


def mlp(h2, u, next_gain, u_dtype, w_up, w_down):
    a = matmul(u, w_up.astype(BF16), BF16, relu2=True, tn=2048)
    return matmul_res_norm(a, w_down.astype(BF16), h2, next_gain, u_dtype)


def kernel(x, norm_mix, norm_mlp, norm_final, dn_w_in, dn_conv_w, dn_a_log, dn_dt_bias, dn_out_norm, dn_w_out, dsa_w_in, dsa_q_norm, dsa_kv_norm, dsa_kidx_norm, dsa_w_uq, dsa_w_uk, dsa_w_uv, dsa_w_out, mlp_w_up, mlp_w_down):
    b, t, d = x.shape
    depth = norm_mix.shape[0]
    h2 = x.reshape(b * t, d)
    u = rms_norm(h2, norm_mix[0], BF16)
    for i in range(depth):
        j = i // 2
        if i % 2 == 0:
            h2, u = gated_deltanet_layer(h2, u, b, t, norm_mlp[i], dn_w_in[j], dn_conv_w[j], dn_a_log[j],
                                         dn_dt_bias[j], dn_out_norm[j], dn_w_out[j])
        else:
            h2, u = dsa_layer(h2, u, b, t, norm_mlp[i], dsa_w_in[j], dsa_q_norm[j], dsa_kv_norm[j],
                              dsa_kidx_norm[j], dsa_w_uq[j], dsa_w_uk[j], dsa_w_uv[j], dsa_w_out[j])
        last = i == depth - 1
        h2, u = mlp(h2, u, norm_final if last else norm_mix[i + 1], x.dtype if last else BF16,
                    mlp_w_up[i], mlp_w_down[i])
    return u.reshape(b, t, d)
```

```python
import functools

import jax
import jax.numpy as jnp
from jax import lax
from jax.experimental import pallas as pl
from jax.experimental.pallas import tpu as pltpu

F32 = jnp.float32
BF16 = jnp.bfloat16
EPS = 1e-6
CHUNK = 64
QBLOCK = 128
DN_QK_HEADS = 16
DN_V_HEADS = 32
DN_HEAD_DIM = 128
DN_KEY_DIM = DN_QK_HEADS * DN_HEAD_DIM
DN_VAL_DIM = DN_V_HEADS * DN_HEAD_DIM
DN_CONV_DIM = 2 * DN_KEY_DIM + DN_VAL_DIM
DSA_HEADS = 16
DSA_HEAD_DIM = 128
Q_LORA = 512
KV_LORA = 256
IDX_HEADS = 16
IDX_DIM = 128
IDX_TOPK = 256
LANES = 128
VMEM_LIMIT_BYTES = 56 * 1024 * 1024
INT_MIN = -(2 ** 31)
MASKED = -1e30
M_INIT = -1e20
LOG2_E = 1.4426950408889634


def _params(*sem):
    return pltpu.CompilerParams(dimension_semantics=sem, vmem_limit_bytes=VMEM_LIMIT_BYTES)


def _sigmoid(x):
    return 1.0 / (1.0 + jnp.exp2(x * (-LOG2_E)))


def _norm_body(x_ref, g_ref, o_ref):
    x = x_ref[...]
    y = x * lax.rsqrt(jnp.mean(x * x, axis=-1, keepdims=True) + EPS) * g_ref[...]
    o_ref[...] = y.astype(o_ref.dtype)


def rms_norm(x, g, out_dtype, tm=512):
    n, d = x.shape
    return pl.pallas_call(
        _norm_body,
        grid=(n // tm,),
        in_specs=[pl.BlockSpec((tm, d), lambda i: (i, 0)), pl.BlockSpec((1, d), lambda i: (0, 0))],
        out_specs=pl.BlockSpec((tm, d), lambda i: (i, 0)),
        out_shape=jax.ShapeDtypeStruct((n, d), out_dtype),
        compiler_params=_params("parallel"),
        name="rms_norm",
    )(x, g.reshape(1, d))


MM_SUB_COLS = 512


def _mm_body(a_ref, w_ref, *rest, nk, relu2, residual):
    if residual:
        r_ref, o_ref = rest[0], rest[1]
    else:
        r_ref, o_ref = None, rest[0]
    acc_ref = rest[-1] if nk > 1 else None

    def finish(acc, cols=slice(None)):
        if relu2:
            acc = jnp.square(jnp.maximum(acc, 0.0))
        if residual:
            acc = acc + r_ref[:, cols]
        o_ref[:, cols] = acc.astype(o_ref.dtype)

    sub = min(MM_SUB_COLS, o_ref.shape[1])
    blocks = [slice(c0, c0 + sub) for c0 in range(0, o_ref.shape[1], sub)]

    def k_step(first, last):
        for cols in blocks:
            part = jnp.dot(a_ref[...], w_ref[:, cols], preferred_element_type=F32)
            if not first:
                part = part + acc_ref[:, cols]
            if last:
                finish(part, cols)
            else:
                acc_ref[:, cols] = part

    if nk == 1:
        k_step(True, True)
        return
    k = pl.program_id(2)
    pl.when(k == 0)(functools.partial(k_step, True, False))
    pl.when((k > 0) & (k < nk - 1))(functools.partial(k_step, False, False))
    pl.when(k == nk - 1)(functools.partial(k_step, False, True))


def matmul(a, w, out_dtype, *, relu2=False, residual=None, tm=1024, tn=1024, tk=2048):
    m, kdim = a.shape
    n = w.shape[1]
    tm, tn, tk = min(tm, m), min(tn, n), min(tk, kdim)
    assert m % tm == 0 and n % tn == 0 and kdim % tk == 0
    nk = kdim // tk
    in_specs = [pl.BlockSpec((tm, tk), lambda i, j, k: (i, k)), pl.BlockSpec((tk, tn), lambda i, j, k: (k, j))]
    args = [a, w]
    if residual is not None:
        in_specs.append(pl.BlockSpec((tm, tn), lambda i, j, k: (i, j)))
        args.append(residual)
    return pl.pallas_call(
        functools.partial(_mm_body, nk=nk, relu2=relu2, residual=residual is not None),
        grid=(m // tm, n // tn, nk),
        in_specs=in_specs,
        out_specs=pl.BlockSpec((tm, tn), lambda i, j, k: (i, j)),
        out_shape=jax.ShapeDtypeStruct((m, n), out_dtype),
        scratch_shapes=[pltpu.VMEM((tm, tn), F32)] if nk > 1 else [],
        compiler_params=_params("parallel", "parallel", "arbitrary"),
        name="matmul",
    )(*args)


def _mm_res_norm_body(a_ref, w_ref, r_ref, g_ref, h_ref, u_ref, *scratch, nk):
    acc_ref = scratch[0] if nk > 1 else None
    n = h_ref.shape[1]
    sub = min(MM_SUB_COLS, n)
    blocks = [slice(c0, c0 + sub) for c0 in range(0, n, sub)]

    def k_step(first, last):
        for cols in blocks:
            part = jnp.dot(a_ref[...], w_ref[:, cols], preferred_element_type=F32)
            if not first:
                part = part + acc_ref[:, cols]
            if last:
                h_ref[:, cols] = part + r_ref[:, cols]
            else:
                acc_ref[:, cols] = part
        if last:
            h = h_ref[...]
            scale = lax.rsqrt(jnp.mean(h * h, axis=-1, keepdims=True) + EPS)
            u_ref[...] = (h * scale * g_ref[...]).astype(u_ref.dtype)

    if nk == 1:
        k_step(True, True)
        return
    k = pl.program_id(1)
    pl.when(k == 0)(functools.partial(k_step, True, False))
    pl.when((k > 0) & (k < nk - 1))(functools.partial(k_step, False, False))
    pl.when(k == nk - 1)(functools.partial(k_step, False, True))


def matmul_res_norm(a, w, residual, gain, u_dtype, *, tm=512, tk=2048):
    m, kdim = a.shape
    n = w.shape[1]
    tm, tk = min(tm, m), min(tk, kdim)
    assert m % tm == 0 and kdim % tk == 0
    nk = kdim // tk
    row = pl.BlockSpec((tm, n), lambda i, k: (i, 0))
    return pl.pallas_call(
        functools.partial(_mm_res_norm_body, nk=nk),
        grid=(m // tm, nk),
        in_specs=[pl.BlockSpec((tm, tk), lambda i, k: (i, k)), pl.BlockSpec((tk, n), lambda i, k: (k, 0)),
                  row, pl.BlockSpec((1, n), lambda i, k: (0, 0))],
        out_specs=[row, row],
        out_shape=[jax.ShapeDtypeStruct((m, n), F32), jax.ShapeDtypeStruct((m, n), u_dtype)],
        scratch_shapes=[pltpu.VMEM((tm, n), F32)] if nk > 1 else [],
        compiler_params=_params("parallel", "arbitrary"),
        name="matmul_res_norm",
    )(a, w, residual, gain.reshape(1, n))


def _dn_gates_body(u_ref, w_ref, alog_ref, dt_ref, o_ref):
    x = jnp.dot(u_ref[...], w_ref[...], preferred_element_type=F32)
    g = (-LOG2_E) * jnp.exp(alog_ref[...]) * (jnp.maximum(x + dt_ref[...], 0.0)
                                              + jnp.log(1.0 + jnp.exp(-jnp.abs(x + dt_ref[...]))))
    pos = lax.broadcasted_iota(jnp.int32, x.shape, 0) % CHUNK
    s = 1
    while s < CHUNK:
        g = g + jnp.where(pos >= s, pltpu.roll(g, s, axis=0), 0.0)
        s *= 2
    lane = lax.broadcasted_iota(jnp.int32, x.shape, 1)
    o_ref[...] = jnp.where(lane < DN_V_HEADS, _sigmoid(x), g)


def dn_gates(u, w_ba, alog_pad, dt_pad, tm=512):
    n, d = u.shape
    return pl.pallas_call(
        _dn_gates_body,
        grid=(n // tm,),
        in_specs=[pl.BlockSpec((tm, d), lambda i: (i, 0)),
                  pl.BlockSpec((d, LANES), lambda i: (0, 0)),
                  pl.BlockSpec((1, LANES), lambda i: (0, 0)),
                  pl.BlockSpec((1, LANES), lambda i: (0, 0))],
        out_specs=pl.BlockSpec((tm, LANES), lambda i: (i, 0)),
        out_shape=jax.ShapeDtypeStruct((n, LANES), F32),
        compiler_params=_params("parallel"),
        name="dn_gates",
    )(u, w_ba, alog_pad, dt_pad)


def _conv_silu(ref, cw_ref, t0, tt):
    cw = cw_ref[...]
    acc = ref[0, t0:t0 + tt, :] * cw[3:4, :]
    for s in (1, 2, 3):
        if t0 == 0:
            x = ref[0, 0:tt, :]
            rows = lax.broadcasted_iota(jnp.int32, x.shape, 0)
            xs = jnp.where(rows >= s, pltpu.roll(x, s, axis=0), 0.0)
        else:
            xs = ref[0, t0 - s:t0 - s + tt, :]
        acc = acc + xs * cw[3 - s:4 - s, :]
    return acc * _sigmoid(acc)


def _l2norm(t, scale=1.0):
    return t * (lax.rsqrt(jnp.sum(t * t, axis=-1, keepdims=True) + EPS) * scale)


def _bmm(a, b):
    return jnp.einsum("nij,njk->nik", a.astype(BF16), b.astype(BF16), preferred_element_type=F32)


def _bmm_nt(a, b):
    return jnp.einsum("nid,njd->nij", a.astype(BF16), b.astype(BF16), preferred_element_type=F32)


DN_GROUP = 32


def _dn_prep_body(q_ref, k_ref, v_ref, cwq_ref, cwk_ref, cwv_ref, gates_ref, gcr_ref,
                  u_ref, w_ref, qd_ref, kd_ref, a_ref, qs_ref, ks_ref, vs_ref, bb_ref, gb_ref, *, t_len):
    j = pl.program_id(1)
    tt = 256
    hd = DN_HEAD_DIM
    for ti in range(t_len // tt):
        t0 = ti * tt
        qs_ref[t0:t0 + tt, :] = _l2norm(_conv_silu(q_ref, cwq_ref, t0, tt), hd ** -0.5)
        ks_ref[t0:t0 + tt, :] = _l2norm(_conv_silu(k_ref, cwk_ref, t0, tt))
        vs_ref[t0:t0 + tt, :] = _conv_silu(v_ref, cwv_ref, t0, tt)
        gt = gates_ref[0, t0:t0 + tt, :]
        lane = lax.broadcasted_iota(jnp.int32, gt.shape, 1)
        for hh in range(2):
            col = 2 * j + hh
            bsel = jnp.sum(jnp.where(lane == col, gt, 0.0), axis=1, keepdims=True)
            gsel = jnp.sum(jnp.where(lane == col + DN_V_HEADS, gt, 0.0), axis=1, keepdims=True)
            bb_ref[hh, t0:t0 + tt, :] = jnp.broadcast_to(bsel, (tt, LANES))
            gb_ref[hh, t0:t0 + tt, :] = jnp.broadcast_to(gsel, (tt, LANES))

    c = CHUNK
    g = min(DN_GROUP, t_len // c)
    rows = g * c
    ri = lax.broadcasted_iota(jnp.int32, (c, c), 0)
    ci = lax.broadcasted_iota(jnp.int32, (c, c), 1)
    lower = ri >= ci
    strict = ri > ci
    eye = jnp.where(ri == ci, 1.0, 0.0)
    pair_mask = strict & ((ri // 2) == (ci // 2))
    merge_masks = [((ri // (2 * s)) == (ci // (2 * s))) & ((ri // s) != (ci // s)) & strict
                   for s in (2, 4, 8, 16, 32)]

    def group_step(gi, carry):
        base = pl.multiple_of(gi * rows, rows)
        kc = ks_ref[pl.ds(base, rows), :].reshape(g, c, hd)
        qc = qs_ref[pl.ds(base, rows), :].reshape(g, c, hd)
        kk1 = _bmm_nt(kc, kc)
        qk1 = _bmm_nt(qc, kc)
        two = lambda x: jnp.concatenate([x, x], axis=0)
        kk, qk, kc2, qc2 = two(kk1), two(qk1), two(kc), two(qc)
        beta_b = jnp.concatenate([bb_ref[hh, pl.ds(base, rows), :].reshape(g, c, LANES) for hh in range(2)], axis=0)
        gcc_b = jnp.concatenate([gb_ref[hh, pl.ds(base, rows), :].reshape(g, c, LANES) for hh in range(2)], axis=0)
        grow = gcr_ref[0, 0, :, pl.ds(base, rows)]
        gcr = jnp.stack([grow[hh:hh + 1, n * c:(n + 1) * c] for hh in range(2) for n in range(g)], axis=0)
        vc = jnp.concatenate([vs_ref[pl.ds(base, rows), hh * hd:(hh + 1) * hd].reshape(g, c, hd)
                              for hh in range(2)], axis=0)
        gl_b = gcc_b[:, c - 1:c, :]
        dm = gcc_b[:, :, :c] - gcr
        decay = jnp.where(lower, jnp.exp2(dm), 0.0)
        lm = jnp.where(strict, kk * beta_b[:, :, :c] * decay, 0.0)
        tb = (eye - jnp.where(pair_mask, lm, 0.0)).astype(BF16)
        lmb = lm.astype(BF16)
        zero_b = jnp.zeros_like(lmb)
        for mk in merge_masks:
            tb = tb - _bmm(_bmm(tb, jnp.where(mk, lmb, zero_b)), tb).astype(BF16)
        eg = jnp.exp2(gcc_b)
        uw = _bmm(tb, jnp.concatenate([vc * beta_b, kc2 * (beta_b * eg)], axis=2))
        qd = qc2 * eg
        kd = kc2 * jnp.exp2(gl_b - gcc_b)
        am = qk * decay
        for hh in range(2):
            cols = slice(hh * hd, (hh + 1) * hd)
            inst = slice(hh * g, (hh + 1) * g)
            u_ref[0, pl.ds(base, rows), cols] = uw[inst, :, :hd].reshape(rows, hd).astype(u_ref.dtype)
            w_ref[0, pl.ds(base, rows), cols] = uw[inst, :, hd:].reshape(rows, hd).astype(w_ref.dtype)
            qd_ref[0, pl.ds(base, rows), cols] = qd[inst].reshape(rows, hd).astype(qd_ref.dtype)
            kd_ref[0, pl.ds(base, rows), cols] = kd[inst].reshape(rows, hd).astype(kd_ref.dtype)
            a_ref[0, pl.ds(base, rows), hh * c:(hh + 1) * c] = am[inst].reshape(rows, c).astype(a_ref.dtype)
        return carry

    lax.fori_loop(0, t_len // rows, group_step, 0)


def dn_prep(proj, conv_w, gates, gcr):
    b, t, _ = proj.shape
    hd = DN_HEAD_DIM
    nq = DN_QK_HEADS
    v_blk0 = 2 * DN_KEY_DIM // (2 * hd)
    wide = pl.BlockSpec((1, t, 2 * hd), lambda bi, j: (bi, 0, j))
    big = jax.ShapeDtypeStruct((b, t, DN_VAL_DIM), BF16)
    return pl.pallas_call(
        functools.partial(_dn_prep_body, t_len=t),
        grid=(b, nq),
        in_specs=[
            pl.BlockSpec((1, t, hd), lambda bi, j: (bi, 0, j)),
            pl.BlockSpec((1, t, hd), lambda bi, j: (bi, 0, nq + j)),
            pl.BlockSpec((1, t, 2 * hd), lambda bi, j: (bi, 0, v_blk0 + j)),
            pl.BlockSpec((4, hd), lambda bi, j: (0, j)),
            pl.BlockSpec((4, hd), lambda bi, j: (0, nq + j)),
            pl.BlockSpec((4, 2 * hd), lambda bi, j: (0, v_blk0 + j)),
            pl.BlockSpec((1, t, LANES), lambda bi, j: (bi, 0, 0)),
            pl.BlockSpec((1, 1, 2, t), lambda bi, j: (bi, j, 0, 0)),
        ],
        out_specs=[wide, wide, wide, wide, pl.BlockSpec((1, t, 2 * CHUNK), lambda bi, j: (bi, 0, j))],
        out_shape=[big, big, big, big, jax.ShapeDtypeStruct((b, t, DN_V_HEADS * CHUNK), BF16)],
        scratch_shapes=[
            pltpu.VMEM((t, hd), F32), pltpu.VMEM((t, hd), F32), pltpu.VMEM((t, 2 * hd), F32),
            pltpu.VMEM((2, t, LANES), F32), pltpu.VMEM((2, t, LANES), F32),
        ],
        compiler_params=_params("parallel", "arbitrary"),
        name="dn_prep",
    )(proj, proj, proj, conv_w, conv_w, conv_w, gates, gcr)


DN_REC_HEADS = 32
DN_REC_ROWS = 256


def _dn_rec_body(u_ref, w_ref, qd_ref, kd_ref, a_ref, z_ref, gl_ref, onorm_ref, o_ref, s_ref, *, t_len):
    hd = DN_HEAD_DIM
    c = CHUNK

    @pl.when(pl.program_id(2) == 0)
    def _():
        s_ref[...] = jnp.zeros_like(s_ref)

    onorm = onorm_ref[...]

    def step(ci, carry):
        r0 = pl.multiple_of(ci * c, c)
        egl = jnp.exp2(gl_ref[0, 0, ci])
        heads = range(DN_REC_HEADS)
        cols = [slice(h * hd, (h + 1) * hd) for h in heads]
        st = [s_ref[h] for h in heads]
        r = [jnp.dot(jnp.concatenate([w_ref[0, pl.ds(r0, c), cols[h]], qd_ref[0, pl.ds(r0, c), cols[h]]], axis=0),
                     st[h].astype(BF16), preferred_element_type=F32) for h in heads]
        vb = [(u_ref[0, pl.ds(r0, c), cols[h]].astype(F32) - r[h][:c]).astype(BF16) for h in heads]
        upd = [lax.dot_general(kd_ref[0, pl.ds(r0, c), cols[h]], vb[h], (((0,), (0,)), ((), ())),
                               preferred_element_type=F32) for h in heads]
        o = [r[h][c:] + jnp.dot(a_ref[0, pl.ds(r0, c), h * c:(h + 1) * c], vb[h], preferred_element_type=F32)
             for h in heads]
        for h in heads:
            s_ref[h] = st[h] * egl[:, h:h + 1] + upd[h]
        for h in heads:
            y = o[h] * lax.rsqrt(jnp.mean(o[h] * o[h], axis=-1, keepdims=True) + EPS) * onorm
            zz = z_ref[0, pl.ds(r0, c), cols[h]]
            o_ref[0, pl.ds(r0, c), cols[h]] = (y * (zz * _sigmoid(zz))).astype(o_ref.dtype)
        return carry

    lax.fori_loop(0, t_len // c, step, 0)


def dn_rec(u, w, qd, kd, a, proj, gl, out_norm):
    b, t, _ = u.shape
    hd = DN_HEAD_DIM
    hb = DN_REC_HEADS
    ng = DN_V_HEADS // hb
    tt = min(DN_REC_ROWS, t)
    z_blk0 = DN_CONV_DIM // (hb * hd)
    wide = pl.BlockSpec((1, tt, hb * hd), lambda bi, g, ti: (bi, ti, g))
    return pl.pallas_call(
        functools.partial(_dn_rec_body, t_len=tt),
        grid=(b, ng, t // tt),
        in_specs=[
            wide, wide, wide, wide,
            pl.BlockSpec((1, tt, hb * CHUNK), lambda bi, g, ti: (bi, ti, g)),
            pl.BlockSpec((1, tt, hb * hd), lambda bi, g, ti: (bi, ti, z_blk0 + g)),
            pl.BlockSpec((1, 1, tt // CHUNK, 1, hb), lambda bi, g, ti: (bi, g, ti, 0, 0)),
            pl.BlockSpec((1, hd), lambda bi, g, ti: (0, 0)),
        ],
        out_specs=wide,
        out_shape=jax.ShapeDtypeStruct((b, t, DN_VAL_DIM), BF16),
        scratch_shapes=[pltpu.VMEM((hb, hd, hd), F32)],
        compiler_params=_params("parallel", "parallel", "arbitrary"),
        name="dn_rec",
    )(u, w, qd, kd, a, proj, gl, out_norm.reshape(1, hd))


def gated_deltanet_layer(h2, u, b, t, next_gain, w_in, conv_w, a_log, dt_bias, out_norm, w_out):
    n = b * t
    n_qkvz = DN_CONV_DIM + DN_VAL_DIM
    proj = matmul(u, w_in[:, :n_qkvz].astype(BF16), F32, tn=2048).reshape(b, t, n_qkvz)
    w_ba = jnp.pad(w_in[:, n_qkvz:], ((0, 0), (0, LANES - 2 * DN_V_HEADS))).astype(BF16)
    pad = lambda p: jnp.pad(p.astype(F32), (DN_V_HEADS, LANES - 2 * DN_V_HEADS)).reshape(1, LANES)
    gates = dn_gates(u, w_ba, pad(a_log), pad(dt_bias))
    gc = gates[:, DN_V_HEADS:2 * DN_V_HEADS].reshape(b, t, DN_V_HEADS)
    gcr = gc.reshape(b, t, DN_QK_HEADS, 2).transpose(0, 2, 3, 1)
    ng = DN_V_HEADS // DN_REC_HEADS
    gl = gc[:, CHUNK - 1::CHUNK, :].reshape(b, t // CHUNK, ng, 1, DN_REC_HEADS).transpose(0, 2, 1, 3, 4)
    uu, ww, qd, kd, am = dn_prep(proj, conv_w, gates.reshape(b, t, LANES), gcr)
    o = dn_rec(uu, ww, qd, kd, am, proj, gl, out_norm)
    return matmul_res_norm(o.reshape(n, DN_VAL_DIM), w_out.astype(BF16), h2, next_gain, BF16)


def _dsa_in_body(u_ref, w_ref, qn_ref, kvn_ref, kin_ref, q_out, kv_out, ki_out, w_out):
    def nrm(x, g):
        return x * lax.rsqrt(jnp.mean(x * x, axis=-1, keepdims=True) + EPS) * g

    a, b2, c2 = Q_LORA, Q_LORA + KV_LORA, Q_LORA + KV_LORA + IDX_DIM
    dot = lambda lo, hi: jnp.dot(u_ref[...], w_ref[:, lo:hi], preferred_element_type=F32)
    q_out[...] = nrm(dot(0, a), qn_ref[...]).astype(q_out.dtype)
    kv_out[...] = nrm(dot(a, b2), kvn_ref[...]).astype(kv_out.dtype)
    ki_out[...] = nrm(dot(b2, c2), kin_ref[...]).astype(ki_out.dtype)
    w_out[...] = dot(c2, c2 + LANES) * (IDX_HEADS ** -0.5 * IDX_DIM ** -0.5)


def dsa_in_proj(u, w_in_p, q_norm, kv_norm, kidx_norm, tm=1024):
    n, d = u.shape
    width = w_in_p.shape[1]
    row = lambda i: (i, 0)
    fix = lambda i: (0, 0)
    return pl.pallas_call(
        _dsa_in_body,
        grid=(n // tm,),
        in_specs=[pl.BlockSpec((tm, d), row), pl.BlockSpec((d, width), fix), pl.BlockSpec((1, Q_LORA), fix),
                  pl.BlockSpec((1, KV_LORA), fix), pl.BlockSpec((1, IDX_DIM), fix)],
        out_specs=[pl.BlockSpec((tm, Q_LORA), row), pl.BlockSpec((tm, KV_LORA), row),
                   pl.BlockSpec((tm, IDX_DIM), row), pl.BlockSpec((tm, LANES), row)],
        out_shape=[jax.ShapeDtypeStruct((n, Q_LORA), BF16), jax.ShapeDtypeStruct((n, KV_LORA), BF16),
                   jax.ShapeDtypeStruct((n, IDX_DIM), BF16), jax.ShapeDtypeStruct((n, LANES), F32)],
        compiler_params=_params("parallel"),
        name="dsa_in_proj",
    )(u, w_in_p, q_norm.reshape(1, -1), kv_norm.reshape(1, -1), kidx_norm.reshape(1, -1))


def _qabs_body(q_ref, w_ref, o_ref, *, r):
    for h in range(DSA_HEADS):
        res = jnp.dot(q_ref[:, h * DSA_HEAD_DIM:(h + 1) * DSA_HEAD_DIM], w_ref[h],
                      preferred_element_type=F32) * (DSA_HEAD_DIM ** -0.5 * LOG2_E)
        for rr in range(r):
            o_ref[rr, h] = res[rr * QBLOCK:(rr + 1) * QBLOCK].astype(o_ref.dtype)


def q_absorb(q_all, w_uk, tm=1024):
    n = q_all.shape[0]
    r = tm // QBLOCK
    hw = DSA_HEADS * DSA_HEAD_DIM
    return pl.pallas_call(
        functools.partial(_qabs_body, r=r),
        grid=(n // tm,),
        in_specs=[pl.BlockSpec((tm, hw), lambda i: (i, 0)),
                  pl.BlockSpec((DSA_HEADS, DSA_HEAD_DIM, KV_LORA), lambda i: (0, 0, 0))],
        out_specs=pl.BlockSpec((r, DSA_HEADS, QBLOCK, KV_LORA), lambda i: (i, 0, 0, 0)),
        out_shape=jax.ShapeDtypeStruct((n // QBLOCK, DSA_HEADS, QBLOCK, KV_LORA), BF16),
        compiler_params=_params("parallel"),
        name="q_absorb",
    )(q_all, w_uk)


def _vup_body(o_ref, w_ref, out_ref, *, r):
    for h in range(DSA_HEADS):
        x = o_ref[:, h].reshape(r * QBLOCK, KV_LORA)
        out_ref[:, h * DSA_HEAD_DIM:(h + 1) * DSA_HEAD_DIM] = jnp.dot(
            x, w_ref[h], preferred_element_type=F32).astype(out_ref.dtype)


def v_up(o_lat, w_uv, tm=1024):
    nb = o_lat.shape[0]
    r = tm // QBLOCK
    hw = DSA_HEADS * DSA_HEAD_DIM
    return pl.pallas_call(
        functools.partial(_vup_body, r=r),
        grid=(nb // r,),
        in_specs=[pl.BlockSpec((r, DSA_HEADS, QBLOCK, KV_LORA), lambda i: (i, 0, 0, 0)),
                  pl.BlockSpec((DSA_HEADS, KV_LORA, DSA_HEAD_DIM), lambda i: (0, 0, 0))],
        out_specs=pl.BlockSpec((tm, hw), lambda i: (i, 0)),
        out_shape=jax.ShapeDtypeStruct((nb * QBLOCK, hw), BF16),
        compiler_params=_params("parallel"),
        name="v_up",
    )(o_lat, w_uv)


def _dsa_core_body(qidx_ref, widx_ref, kidx_ref, ckv_ref, qabs_ref, o_ref,
                   keys_ref, tau_ref, count_ref, bias_ref, s_ref, p_ref, m_ref, l_ref, acc_ref, *, top_k, kt):
    i = pl.program_id(1)
    nq = QBLOCK
    nkt = keys_ref.shape[0]
    n_act = ((i + 1) * nq + kt - 1) // kt
    wt = widx_ref[0].T
    krow = lax.broadcasted_iota(jnp.int32, (kt, nq), 0)
    qcol = lax.broadcasted_iota(jnp.int32, (kt, nq), 1)
    q_chunk = (i * nq + qcol) // CHUNK
    tn = (((1,), (1,)), ((), ()))
    sub = 8

    def index_tile(jt, carry):
        k0 = pl.multiple_of(jt * kt, kt)
        kblk = kidx_ref[0, pl.ds(k0, kt), :]
        sc = jnp.zeros((kt, nq), F32)
        for h in range(IDX_HEADS):
            d = lax.dot_general(kblk, qidx_ref[0, :, h * IDX_DIM:(h + 1) * IDX_DIM], tn,
                                preferred_element_type=F32)
            sc = sc + jnp.maximum(d, 0.0) * wt[h:h + 1, :]
        sc = jnp.where(sc == 0.0, 0.0, sc)
        bits = pltpu.bitcast(sc, jnp.int32)
        key = bits ^ ((bits >> 31) & 0x7FFFFFFF)
        adm = ((k0 + krow) // CHUNK) <= q_chunk
        keys_ref[jt] = jnp.where(adm, key, INT_MIN)
        return carry

    lax.fori_loop(0, n_act, index_tile, 0)

    for jt in range(nkt):
        @pl.when(jt >= n_act)
        def _():
            keys_ref[jt] = jnp.full((kt, nq), INT_MIN, jnp.int32)

    def bisect_over(ntiles):
        def bisect(it, carry):
            prefix, at_prefix = carry
            cand = prefix | lax.shift_left(jnp.int32(1), 31 - it)
            cand_s = cand ^ INT_MIN
            cnt = jnp.zeros((sub, nq), F32)
            for jt in range(ntiles):
                hit = jnp.where(keys_ref[jt] >= cand_s, 1.0, 0.0)
                cnt = cnt + jnp.sum(hit.reshape(kt // sub, sub, nq), axis=0)
            cnt = jnp.sum(cnt, axis=0, keepdims=True)
            take = cnt >= top_k
            return jnp.where(take, cand, prefix), jnp.where(take, cnt, at_prefix)

        prefix, at_prefix = lax.fori_loop(0, 32, bisect,
                                          (jnp.zeros((1, nq), jnp.int32), jnp.zeros((1, nq), F32)))
        tau_ref[...] = jnp.broadcast_to(jnp.maximum(prefix ^ INT_MIN, INT_MIN + 1), tau_ref.shape)
        count_ref[...] = jnp.broadcast_to(at_prefix, count_ref.shape)

    search = (i + 1) * nq > top_k
    step = 2 if nkt % 2 == 0 else 1
    for ntiles in range(step, nkt + 1, step):
        @pl.when(search & (n_act > ntiles - step) & (n_act <= ntiles))
        def _():
            bisect_over(ntiles)

    @pl.when(jnp.logical_not(search))
    def _():
        tau_ref[...] = jnp.full(tau_ref.shape, INT_MIN + 1, jnp.int32)
        count_ref[...] = jnp.zeros_like(count_ref)

    tau = tau_ref[0:1, :]

    @pl.when(jnp.max(count_ref[0:1, :]) > top_k)
    def _():
        def count_above(jt, gt):
            return gt + jnp.sum(jnp.where(keys_ref[jt] > tau, 1.0, 0.0).reshape(kt // sub, sub, nq), axis=0)

        gt8 = lax.fori_loop(0, n_act, count_above, jnp.zeros((sub, nq), F32))
        need = top_k - jnp.sum(gt8, axis=0, keepdims=True)
        before_row = (lax.broadcasted_iota(jnp.int32, (kt, kt), 1)
                      < lax.broadcasted_iota(jnp.int32, (kt, kt), 0))
        tri = jnp.where(before_row, 1.0, 0.0).astype(BF16)

        def drop_surplus(jt, seen):
            kj = keys_ref[jt]
            eq = kj == tau
            eqf = jnp.where(eq, 1.0, 0.0)
            earlier = jnp.dot(tri, eqf.astype(BF16), preferred_element_type=F32) + seen
            keys_ref[jt] = jnp.where(eq & (earlier >= need), INT_MIN, kj)
            return seen + jnp.sum(eqf, axis=0, keepdims=True)

        lax.fori_loop(0, n_act, drop_surplus, jnp.zeros((1, nq), F32))

    rows = DSA_HEADS * nq
    rep = kt // LANES
    def attend_tile(jt, first):
        k0 = pl.multiple_of(jt * kt, kt)
        ck = ckv_ref[0, pl.ds(k0, kt), :]
        bias_ref[...] = jnp.where(keys_ref[jt] >= tau, 0.0, MASKED).T
        s_ref[...] = lax.dot_general(qabs_ref[0].reshape(rows, KV_LORA), ck, tn, preferred_element_type=F32)

        for h in range(DSA_HEADS):
            hr = slice(h * nq, (h + 1) * nq)
            s = s_ref[hr, :] + bias_ref[...]
            m_old = jnp.full((nq, LANES), M_INIT, F32) if first else m_ref[hr, :]
            m_new = jnp.maximum(m_old, jnp.max(s, axis=1, keepdims=True))
            p = jnp.exp2(s - jnp.concatenate([m_new] * rep, axis=1))
            psum = jnp.sum(p, axis=1, keepdims=True)
            if first:
                l_ref[hr, :] = jnp.broadcast_to(psum, (nq, LANES))
            else:
                alpha = jnp.exp2(m_old - m_new)
                l_ref[hr, :] = alpha * l_ref[hr, :] + psum
                acc_ref[hr, :] = acc_ref[hr, :] * jnp.concatenate([alpha] * (KV_LORA // LANES), axis=1)
            m_ref[hr, :] = m_new
            p_ref[hr, :] = p.astype(BF16)
        pv = jnp.dot(p_ref[...], ck, preferred_element_type=F32)
        if first:
            acc_ref[...] = pv
        else:
            acc_ref[...] += pv

    attend_tile(0, True)

    def later_tile(jt, carry):
        attend_tile(jt, False)
        return carry

    lax.fori_loop(1, n_act, later_tile, 0)
    inv_l = 1.0 / l_ref[...]
    out = acc_ref[...] * jnp.concatenate([inv_l] * (KV_LORA // LANES), axis=1)
    o_ref[0] = out.reshape(DSA_HEADS, nq, KV_LORA).astype(o_ref.dtype)


def dsa_core(q_all, widx, kidx, ckv, qabs, b, t, kt=256):
    nblk = t // QBLOCK
    top_k = min(IDX_TOPK, t // 4)
    hw = IDX_HEADS * IDX_DIM
    rows = DSA_HEADS * QBLOCK
    return pl.pallas_call(
        functools.partial(_dsa_core_body, top_k=top_k, kt=kt),
        grid=(b, nblk),
        in_specs=[
            pl.BlockSpec((1, QBLOCK, hw), lambda bi, i: (bi, i, 1)),
            pl.BlockSpec((1, QBLOCK, LANES), lambda bi, i: (bi, i, 0)),
            pl.BlockSpec((1, t, IDX_DIM), lambda bi, i: (bi, 0, 0)),
            pl.BlockSpec((1, t, KV_LORA), lambda bi, i: (bi, 0, 0)),
            pl.BlockSpec((1, DSA_HEADS, QBLOCK, KV_LORA), lambda bi, i: (bi * nblk + i, 0, 0, 0)),
        ],
        out_specs=pl.BlockSpec((1, DSA_HEADS, QBLOCK, KV_LORA), lambda bi, i: (bi * nblk + i, 0, 0, 0)),
        out_shape=jax.ShapeDtypeStruct((b * nblk, DSA_HEADS, QBLOCK, KV_LORA), BF16),
        scratch_shapes=[pltpu.VMEM((t // kt, kt, QBLOCK), jnp.int32), pltpu.VMEM((8, QBLOCK), jnp.int32),
                        pltpu.VMEM((8, QBLOCK), F32),
                        pltpu.VMEM((QBLOCK, kt), F32),
                        pltpu.VMEM((rows, kt), F32), pltpu.VMEM((rows, kt), BF16),
                        pltpu.VMEM((rows, LANES), F32), pltpu.VMEM((rows, LANES), F32),
                        pltpu.VMEM((rows, KV_LORA), F32)],
        compiler_params=_params("parallel", "arbitrary"),
        name="dsa_core",
    )(q_all.reshape(b, t, -1), widx.reshape(b, t, LANES), kidx.reshape(b, t, IDX_DIM),
      ckv.reshape(b, t, KV_LORA), qabs)


def dsa_layer(h2, u, b, t, next_gain, w_in, q_norm, kv_norm, kidx_norm, w_uq, w_uk, w_uv, w_out):
    width = Q_LORA + KV_LORA + IDX_DIM + LANES
    w_in_p = jnp.pad(w_in, ((0, 0), (0, width - w_in.shape[1]))).astype(BF16)
    qlat, ckv, kidx, widx = dsa_in_proj(u, w_in_p, q_norm, kv_norm, kidx_norm)
    q_all = matmul(qlat, w_uq.astype(BF16), BF16)
    qabs = q_absorb(q_all, w_uk.astype(BF16))
    o_lat = dsa_core(q_all, widx, kidx, ckv, qabs, b, t)
    o = v_up(o_lat, w_uv.astype(BF16))
    return matmul_res_norm(o, w_out.astype(BF16), h2, next_gain, BF16)


def mlp(h2, u, next_gain, u_dtype, w_up, w_down):
    a = matmul(u, w_up.astype(BF16), BF16, relu2=True, tn=2048)
    return matmul_res_norm(a, w_down.astype(BF16), h2, next_gain, u_dtype)


def kernel(x, norm_mix, norm_mlp, norm_final, dn_w_in, dn_conv_w, dn_a_log, dn_dt_bias, dn_out_norm, dn_w_out, dsa_w_in, dsa_q_norm, dsa_kv_norm, dsa_kidx_norm, dsa_w_uq, dsa_w_uk, dsa_w_uv, dsa_w_out, mlp_w_up, mlp_w_down):
    b, t, d = x.shape
    depth = norm_mix.shape[0]
    h2 = x.reshape(b * t, d)
    u = rms_norm(h2, norm_mix[0], BF16)
    for i in range(depth):
        j = i // 2
        if i % 2 == 0:
            h2, u = gated_deltanet_layer(h2, u, b, t, norm_mlp[i], dn_w_in[j], dn_conv_w[j], dn_a_log[j],
                                         dn_dt_bias[j], dn_out_norm[j], dn_w_out[j])
        else:
            h2, u = dsa_layer(h2, u, b, t, norm_mlp[i], dsa_w_in[j], dsa_q_norm[j], dsa_kv_norm[j],
                              dsa_kidx_norm[j], dsa_w_uq[j], dsa_w_uk[j], dsa_w_uv[j], dsa_w_out[j])
        last = i == depth - 1
        h2, u = mlp(h2, u, norm_final if last else norm_mix[i + 1], x.dtype if last else BF16,
                    mlp_w_up[i], mlp_w_down[i])
    return u.reshape(b, t, d)
```

```python
import functools

import jax
import jax.numpy as jnp
from jax import lax
from jax.experimental import pallas as pl
from jax.experimental.pallas import tpu as pltpu

F32 = jnp.float32
BF16 = jnp.bfloat16
EPS = 1e-6
CHUNK = 64
QBLOCK = 128
DN_QK_HEADS = 16
DN_V_HEADS = 32
DN_HEAD_DIM = 128
DN_KEY_DIM = DN_QK_HEADS * DN_HEAD_DIM
DN_VAL_DIM = DN_V_HEADS * DN_HEAD_DIM
DN_CONV_DIM = 2 * DN_KEY_DIM + DN_VAL_DIM
DSA_HEADS = 16
DSA_HEAD_DIM = 128
Q_LORA = 512
KV_LORA = 256
IDX_HEADS = 16
IDX_DIM = 128
IDX_TOPK = 256
LANES = 128
VMEM_LIMIT_BYTES = 56 * 1024 * 1024
INT_MIN = -(2 ** 31)
MASKED = -1e30
M_INIT = -1e20
LOG2_E = 1.4426950408889634


def _params(*sem):
    return pltpu.CompilerParams(dimension_semantics=sem, vmem_limit_bytes=VMEM_LIMIT_BYTES)


def _sigmoid(x):
    return 1.0 / (1.0 + jnp.exp2(x * (-LOG2_E)))


def _norm_body(x_ref, g_ref, o_ref):
    x = x_ref[...]
    y = x * lax.rsqrt(jnp.mean(x * x, axis=-1, keepdims=True) + EPS) * g_ref[...]
    o_ref[...] = y.astype(o_ref.dtype)


def rms_norm(x, g, out_dtype, tm=512):
    n, d = x.shape
    return pl.pallas_call(
        _norm_body,
        grid=(n // tm,),
        in_specs=[pl.BlockSpec((tm, d), lambda i: (i, 0)), pl.BlockSpec((1, d), lambda i: (0, 0))],
        out_specs=pl.BlockSpec((tm, d), lambda i: (i, 0)),
        out_shape=jax.ShapeDtypeStruct((n, d), out_dtype),
        compiler_params=_params("parallel"),
        name="rms_norm",
    )(x, g.reshape(1, d))


MM_SUB_COLS = 512


def _mm_body(a_ref, w_ref, *rest, nk, relu2, residual):
    if residual:
        r_ref, o_ref = rest[0], rest[1]
    else:
        r_ref, o_ref = None, rest[0]
    acc_ref = rest[-1] if nk > 1 else None

    def finish(acc, cols=slice(None)):
        if relu2:
            acc = jnp.square(jnp.maximum(acc, 0.0))
        if residual:
            acc = acc + r_ref[:, cols]
        o_ref[:, cols] = acc.astype(o_ref.dtype)

    sub = min(MM_SUB_COLS, o_ref.shape[1])
    blocks = [slice(c0, c0 + sub) for c0 in range(0, o_ref.shape[1], sub)]

    def k_step(first, last):
        for cols in blocks:
            part = jnp.dot(a_ref[...], w_ref[:, cols], preferred_element_type=F32)
            if not first:
                part = part + acc_ref[:, cols]
            if last:
                finish(part, cols)
            else:
                acc_ref[:, cols] = part

    if nk == 1:
        k_step(True, True)
        return
    k = pl.program_id(2)
    pl.when(k == 0)(functools.partial(k_step, True, False))
    pl.when((k > 0) & (k < nk - 1))(functools.partial(k_step, False, False))
    pl.when(k == nk - 1)(functools.partial(k_step, False, True))


def matmul(a, w, out_dtype, *, relu2=False, residual=None, tm=1024, tn=1024, tk=2048):
    m, kdim = a.shape
    n = w.shape[1]
    tm, tn, tk = min(tm, m), min(tn, n), min(tk, kdim)
    assert m % tm == 0 and n % tn == 0 and kdim % tk == 0
    nk = kdim // tk
    in_specs = [pl.BlockSpec((tm, tk), lambda i, j, k: (i, k)), pl.BlockSpec((tk, tn), lambda i, j, k: (k, j))]
    args = [a, w]
    if residual is not None:
        in_specs.append(pl.BlockSpec((tm, tn), lambda i, j, k: (i, j)))
        args.append(residual)
    return pl.pallas_call(
        functools.partial(_mm_body, nk=nk, relu2=relu2, residual=residual is not None),
        grid=(m // tm, n // tn, nk),
        in_specs=in_specs,
        out_specs=pl.BlockSpec((tm, tn), lambda i, j, k: (i, j)),
        out_shape=jax.ShapeDtypeStruct((m, n), out_dtype),
        scratch_shapes=[pltpu.VMEM((tm, tn), F32)] if nk > 1 else [],
        compiler_params=_params("parallel", "parallel", "arbitrary"),
        name="matmul",
    )(*args)


def _mm_res_norm_body(a_ref, w_ref, r_ref, g_ref, h_ref, u_ref, *scratch, nk):
    acc_ref = scratch[0] if scratch else None
    n = h_ref.shape[1]
    sub = min(MM_SUB_COLS, n)
    blocks = [slice(c0, c0 + sub) for c0 in range(0, n, sub)]

    def k_step(first, last):
        for cols in blocks:
            part = jnp.dot(a_ref[...], w_ref[:, cols], preferred_element_type=F32)
            if acc_ref is None:
                h_ref[:, cols] = part + (r_ref[:, cols] if first else h_ref[:, cols])
                continue
            if not first:
                part = part + acc_ref[:, cols]
            if last:
                h_ref[:, cols] = part + r_ref[:, cols]
            else:
                acc_ref[:, cols] = part
        if last:
            h = h_ref[...]
            scale = lax.rsqrt(jnp.mean(h * h, axis=-1, keepdims=True) + EPS)
            u_ref[...] = (h * scale * g_ref[...]).astype(u_ref.dtype)

    if nk == 1:
        k_step(True, True)
        return
    k = pl.program_id(1)
    pl.when(k == 0)(functools.partial(k_step, True, False))
    pl.when((k > 0) & (k < nk - 1))(functools.partial(k_step, False, False))
    pl.when(k == nk - 1)(functools.partial(k_step, False, True))


def matmul_res_norm(a, w, residual, gain, u_dtype, *, tm=512, tk=2048, acc_in_h=False):
    m, kdim = a.shape
    n = w.shape[1]
    tm, tk = min(tm, m), min(tk, kdim)
    assert m % tm == 0 and kdim % tk == 0
    nk = kdim // tk
    row = pl.BlockSpec((tm, n), lambda i, k: (i, 0))
    return pl.pallas_call(
        functools.partial(_mm_res_norm_body, nk=nk),
        grid=(m // tm, nk),
        in_specs=[pl.BlockSpec((tm, tk), lambda i, k: (i, k)), pl.BlockSpec((tk, n), lambda i, k: (k, 0)),
                  row, pl.BlockSpec((1, n), lambda i, k: (0, 0))],
        out_specs=[row, row],
        out_shape=[jax.ShapeDtypeStruct((m, n), F32), jax.ShapeDtypeStruct((m, n), u_dtype)],
        scratch_shapes=[pltpu.VMEM((tm, n), F32)] if nk > 1 and not acc_in_h else [],
        compiler_params=_params("parallel", "arbitrary"),
        name="matmul_res_norm",
    )(a, w, residual, gain.reshape(1, n))


def _dn_gates_body(u_ref, w_ref, alog_ref, dt_ref, o_ref):
    x = jnp.dot(u_ref[...], w_ref[...], preferred_element_type=F32)
    g = (-LOG2_E) * jnp.exp(alog_ref[...]) * (jnp.maximum(x + dt_ref[...], 0.0)
                                              + jnp.log(1.0 + jnp.exp(-jnp.abs(x + dt_ref[...]))))
    pos = lax.broadcasted_iota(jnp.int32, x.shape, 0) % CHUNK
    s = 1
    while s < CHUNK:
        g = g + jnp.where(pos >= s, pltpu.roll(g, s, axis=0), 0.0)
        s *= 2
    lane = lax.broadcasted_iota(jnp.int32, x.shape, 1)
    o_ref[...] = jnp.where(lane < DN_V_HEADS, _sigmoid(x), g)


def dn_gates(u, w_ba, alog_pad, dt_pad, tm=512):
    n, d = u.shape
    return pl.pallas_call(
        _dn_gates_body,
        grid=(n // tm,),
        in_specs=[pl.BlockSpec((tm, d), lambda i: (i, 0)),
                  pl.BlockSpec((d, LANES), lambda i: (0, 0)),
                  pl.BlockSpec((1, LANES), lambda i: (0, 0)),
                  pl.BlockSpec((1, LANES), lambda i: (0, 0))],
        out_specs=pl.BlockSpec((tm, LANES), lambda i: (i, 0)),
        out_shape=jax.ShapeDtypeStruct((n, LANES), F32),
        compiler_params=_params("parallel"),
        name="dn_gates",
    )(u, w_ba, alog_pad, dt_pad)


def _conv_silu(ref, cw_ref, t0, tt):
    cw = cw_ref[...]
    acc = ref[0, t0:t0 + tt, :] * cw[3:4, :]
    for s in (1, 2, 3):
        if t0 == 0:
            x = ref[0, 0:tt, :]
            rows = lax.broadcasted_iota(jnp.int32, x.shape, 0)
            xs = jnp.where(rows >= s, pltpu.roll(x, s, axis=0), 0.0)
        else:
            xs = ref[0, t0 - s:t0 - s + tt, :]
        acc = acc + xs * cw[3 - s:4 - s, :]
    return acc * _sigmoid(acc)


def _l2norm(t, scale=1.0):
    return t * (lax.rsqrt(jnp.sum(t * t, axis=-1, keepdims=True) + EPS) * scale)


def _bmm(a, b):
    return jnp.einsum("nij,njk->nik", a.astype(BF16), b.astype(BF16), preferred_element_type=F32)


def _bmm_nt(a, b):
    return jnp.einsum("nid,njd->nij", a.astype(BF16), b.astype(BF16), preferred_element_type=F32)


DN_GROUP = 32


def _dn_prep_body(q_ref, k_ref, v_ref, cwq_ref, cwk_ref, cwv_ref, gates_ref, gcr_ref,
                  u_ref, w_ref, qd_ref, kd_ref, a_ref, qs_ref, ks_ref, vs_ref, bb_ref, gb_ref, *, t_len):
    j = pl.program_id(1)
    tt = 256
    hd = DN_HEAD_DIM
    for ti in range(t_len // tt):
        t0 = ti * tt
        qs_ref[t0:t0 + tt, :] = _l2norm(_conv_silu(q_ref, cwq_ref, t0, tt), hd ** -0.5)
        ks_ref[t0:t0 + tt, :] = _l2norm(_conv_silu(k_ref, cwk_ref, t0, tt))
        vs_ref[t0:t0 + tt, :] = _conv_silu(v_ref, cwv_ref, t0, tt)
        gt = gates_ref[0, t0:t0 + tt, :]
        lane = lax.broadcasted_iota(jnp.int32, gt.shape, 1)
        for hh in range(2):
            col = 2 * j + hh
            bsel = jnp.sum(jnp.where(lane == col, gt, 0.0), axis=1, keepdims=True)
            gsel = jnp.sum(jnp.where(lane == col + DN_V_HEADS, gt, 0.0), axis=1, keepdims=True)
            bb_ref[hh, t0:t0 + tt, :] = jnp.broadcast_to(bsel, (tt, LANES))
            gb_ref[hh, t0:t0 + tt, :] = jnp.broadcast_to(gsel, (tt, LANES))

    c = CHUNK
    g = min(DN_GROUP, t_len // c)
    rows = g * c
    ri = lax.broadcasted_iota(jnp.int32, (c, c), 0)
    ci = lax.broadcasted_iota(jnp.int32, (c, c), 1)
    lower = ri >= ci
    strict = ri > ci
    eye = jnp.where(ri == ci, 1.0, 0.0)
    pair_mask = strict & ((ri // 2) == (ci // 2))
    merge_masks = [((ri // (2 * s)) == (ci // (2 * s))) & ((ri // s) != (ci // s)) & strict
                   for s in (2, 4, 8, 16, 32)]

    def group_step(gi, carry):
        base = pl.multiple_of(gi * rows, rows)
        kc = ks_ref[pl.ds(base, rows), :].reshape(g, c, hd)
        qc = qs_ref[pl.ds(base, rows), :].reshape(g, c, hd)
        kk1 = _bmm_nt(kc, kc)
        qk1 = _bmm_nt(qc, kc)
        two = lambda x: jnp.concatenate([x, x], axis=0)
        kk, qk, kc2, qc2 = two(kk1), two(qk1), two(kc), two(qc)
        beta_b = jnp.concatenate([bb_ref[hh, pl.ds(base, rows), :].reshape(g, c, LANES) for hh in range(2)], axis=0)
        gcc_b = jnp.concatenate([gb_ref[hh, pl.ds(base, rows), :].reshape(g, c, LANES) for hh in range(2)], axis=0)
        grow = gcr_ref[0, 0, :, pl.ds(base, rows)]
        gcr = jnp.stack([grow[hh:hh + 1, n * c:(n + 1) * c] for hh in range(2) for n in range(g)], axis=0)
        vc = jnp.concatenate([vs_ref[pl.ds(base, rows), hh * hd:(hh + 1) * hd].reshape(g, c, hd)
                              for hh in range(2)], axis=0)
        gl_b = gcc_b[:, c - 1:c, :]
        dm = gcc_b[:, :, :c] - gcr
        decay = jnp.where(lower, jnp.exp2(dm), 0.0)
        lm = jnp.where(strict, kk * beta_b[:, :, :c] * decay, 0.0)
        tb = (eye - jnp.where(pair_mask, lm, 0.0)).astype(BF16)
        lmb = lm.astype(BF16)
        zero_b = jnp.zeros_like(lmb)
        for mk in merge_masks:
            tb = tb - _bmm(_bmm(tb, jnp.where(mk, lmb, zero_b)), tb).astype(BF16)
        eg = jnp.exp2(gcc_b)
        uw = _bmm(tb, jnp.concatenate([vc * beta_b, kc2 * (beta_b * eg)], axis=2))
        qd = qc2 * eg
        kd = kc2 * jnp.exp2(gl_b - gcc_b)
        am = qk * decay
        for hh in range(2):
            cols = slice(hh * hd, (hh + 1) * hd)
            inst = slice(hh * g, (hh + 1) * g)
            u_ref[0, pl.ds(base, rows), cols] = uw[inst, :, :hd].reshape(rows, hd).astype(u_ref.dtype)
            w_ref[0, pl.ds(base, rows), cols] = uw[inst, :, hd:].reshape(rows, hd).astype(w_ref.dtype)
            qd_ref[0, pl.ds(base, rows), cols] = qd[inst].reshape(rows, hd).astype(qd_ref.dtype)
            kd_ref[0, pl.ds(base, rows), cols] = kd[inst].reshape(rows, hd).astype(kd_ref.dtype)
            a_ref[0, pl.ds(base, rows), hh * c:(hh + 1) * c] = am[inst].reshape(rows, c).astype(a_ref.dtype)
        return carry

    lax.fori_loop(0, t_len // rows, group_step, 0)


def dn_prep(proj, conv_w, gates, gcr):
    b, t, _ = proj.shape
    hd = DN_HEAD_DIM
    nq = DN_QK_HEADS
    v_blk0 = 2 * DN_KEY_DIM // (2 * hd)
    wide = pl.BlockSpec((1, t, 2 * hd), lambda bi, j: (bi, 0, j))
    big = jax.ShapeDtypeStruct((b, t, DN_VAL_DIM), BF16)
    return pl.pallas_call(
        functools.partial(_dn_prep_body, t_len=t),
        grid=(b, nq),
        in_specs=[
            pl.BlockSpec((1, t, hd), lambda bi, j: (bi, 0, j)),
            pl.BlockSpec((1, t, hd), lambda bi, j: (bi, 0, nq + j)),
            pl.BlockSpec((1, t, 2 * hd), lambda bi, j: (bi, 0, v_blk0 + j)),
            pl.BlockSpec((4, hd), lambda bi, j: (0, j)),
            pl.BlockSpec((4, hd), lambda bi, j: (0, nq + j)),
            pl.BlockSpec((4, 2 * hd), lambda bi, j: (0, v_blk0 + j)),
            pl.BlockSpec((1, t, LANES), lambda bi, j: (bi, 0, 0)),
            pl.BlockSpec((1, 1, 2, t), lambda bi, j: (bi, j, 0, 0)),
        ],
        out_specs=[wide, wide, wide, wide, pl.BlockSpec((1, t, 2 * CHUNK), lambda bi, j: (bi, 0, j))],
        out_shape=[big, big, big, big, jax.ShapeDtypeStruct((b, t, DN_V_HEADS * CHUNK), BF16)],
        scratch_shapes=[
            pltpu.VMEM((t, hd), F32), pltpu.VMEM((t, hd), F32), pltpu.VMEM((t, 2 * hd), F32),
            pltpu.VMEM((2, t, LANES), F32), pltpu.VMEM((2, t, LANES), F32),
        ],
        compiler_params=_params("parallel", "arbitrary"),
        name="dn_prep",
    )(proj, proj, proj, conv_w, conv_w, conv_w, gates, gcr)


DN_REC_HEADS = 32
DN_REC_ROWS = 256


def _dn_rec_body(u_ref, w_ref, qd_ref, kd_ref, a_ref, z_ref, gl_ref, onorm_ref, o_ref, s_ref, *, t_len):
    hd = DN_HEAD_DIM
    c = CHUNK

    @pl.when(pl.program_id(2) == 0)
    def _():
        s_ref[...] = jnp.zeros_like(s_ref)

    onorm = onorm_ref[...]

    def step(ci, carry):
        r0 = pl.multiple_of(ci * c, c)
        egl = jnp.exp2(gl_ref[0, 0, ci])
        heads = range(DN_REC_HEADS)
        cols = [slice(h * hd, (h + 1) * hd) for h in heads]
        st = [s_ref[h] for h in heads]
        r = [jnp.dot(jnp.concatenate([w_ref[0, pl.ds(r0, c), cols[h]], qd_ref[0, pl.ds(r0, c), cols[h]]], axis=0),
                     st[h].astype(BF16), preferred_element_type=F32) for h in heads]
        vb = [(u_ref[0, pl.ds(r0, c), cols[h]].astype(F32) - r[h][:c]).astype(BF16) for h in heads]
        upd = [lax.dot_general(kd_ref[0, pl.ds(r0, c), cols[h]], vb[h], (((0,), (0,)), ((), ())),
                               preferred_element_type=F32) for h in heads]
        o = [r[h][c:] + jnp.dot(a_ref[0, pl.ds(r0, c), h * c:(h + 1) * c], vb[h], preferred_element_type=F32)
             for h in heads]
        for h in heads:
            s_ref[h] = st[h] * egl[:, h:h + 1] + upd[h]
        for h in heads:
            y = o[h] * lax.rsqrt(jnp.mean(o[h] * o[h], axis=-1, keepdims=True) + EPS) * onorm
            zz = z_ref[0, pl.ds(r0, c), cols[h]]
            o_ref[0, pl.ds(r0, c), cols[h]] = (y * (zz * _sigmoid(zz))).astype(o_ref.dtype)
        return carry

    lax.fori_loop(0, t_len // c, step, 0)


def dn_rec(u, w, qd, kd, a, proj, gl, out_norm):
    b, t, _ = u.shape
    hd = DN_HEAD_DIM
    hb = DN_REC_HEADS
    ng = DN_V_HEADS // hb
    tt = min(DN_REC_ROWS, t)
    z_blk0 = DN_CONV_DIM // (hb * hd)
    wide = pl.BlockSpec((1, tt, hb * hd), lambda bi, g, ti: (bi, ti, g))
    return pl.pallas_call(
        functools.partial(_dn_rec_body, t_len=tt),
        grid=(b, ng, t // tt),
        in_specs=[
            wide, wide, wide, wide,
            pl.BlockSpec((1, tt, hb * CHUNK), lambda bi, g, ti: (bi, ti, g)),
            pl.BlockSpec((1, tt, hb * hd), lambda bi, g, ti: (bi, ti, z_blk0 + g)),
            pl.BlockSpec((1, 1, tt // CHUNK, 1, hb), lambda bi, g, ti: (bi, g, ti, 0, 0)),
            pl.BlockSpec((1, hd), lambda bi, g, ti: (0, 0)),
        ],
        out_specs=wide,
        out_shape=jax.ShapeDtypeStruct((b, t, DN_VAL_DIM), BF16),
        scratch_shapes=[pltpu.VMEM((hb, hd, hd), F32)],
        compiler_params=_params("parallel", "parallel", "arbitrary"),
        name="dn_rec",
    )(u, w, qd, kd, a, proj, gl, out_norm.reshape(1, hd))


def gated_deltanet_layer(h2, u, b, t, next_gain, w_in, conv_w, a_log, dt_bias, out_norm, w_out):
    n = b * t
    n_qkvz = DN_CONV_DIM + DN_VAL_DIM
    proj = matmul(u, w_in[:, :n_qkvz].astype(BF16), F32, tn=2048).reshape(b, t, n_qkvz)
    w_ba = jnp.pad(w_in[:, n_qkvz:], ((0, 0), (0, LANES - 2 * DN_V_HEADS))).astype(BF16)
    pad = lambda p: jnp.pad(p.astype(F32), (DN_V_HEADS, LANES - 2 * DN_V_HEADS)).reshape(1, LANES)
    gates = dn_gates(u, w_ba, pad(a_log), pad(dt_bias))
    gc = gates[:, DN_V_HEADS:2 * DN_V_HEADS].reshape(b, t, DN_V_HEADS)
    gcr = gc.reshape(b, t, DN_QK_HEADS, 2).transpose(0, 2, 3, 1)
    ng = DN_V_HEADS // DN_REC_HEADS
    gl = gc[:, CHUNK - 1::CHUNK, :].reshape(b, t // CHUNK, ng, 1, DN_REC_HEADS).transpose(0, 2, 1, 3, 4)
    uu, ww, qd, kd, am = dn_prep(proj, conv_w, gates.reshape(b, t, LANES), gcr)
    o = dn_rec(uu, ww, qd, kd, am, proj, gl, out_norm)
    return matmul_res_norm(o.reshape(n, DN_VAL_DIM), w_out.astype(BF16), h2, next_gain, BF16)


def _dsa_in_body(u_ref, w_ref, qn_ref, kvn_ref, kin_ref, q_out, kv_out, ki_out, w_out):
    def nrm(x, g):
        return x * lax.rsqrt(jnp.mean(x * x, axis=-1, keepdims=True) + EPS) * g

    a, b2, c2 = Q_LORA, Q_LORA + KV_LORA, Q_LORA + KV_LORA + IDX_DIM
    dot = lambda lo, hi: jnp.dot(u_ref[...], w_ref[:, lo:hi], preferred_element_type=F32)
    q_out[...] = nrm(dot(0, a), qn_ref[...]).astype(q_out.dtype)
    kv_out[...] = nrm(dot(a, b2), kvn_ref[...]).astype(kv_out.dtype)
    ki_out[...] = nrm(dot(b2, c2), kin_ref[...]).astype(ki_out.dtype)
    w_out[...] = dot(c2, c2 + LANES) * (IDX_HEADS ** -0.5 * IDX_DIM ** -0.5)


def dsa_in_proj(u, w_in_p, q_norm, kv_norm, kidx_norm, tm=1024):
    n, d = u.shape
    width = w_in_p.shape[1]
    row = lambda i: (i, 0)
    fix = lambda i: (0, 0)
    return pl.pallas_call(
        _dsa_in_body,
        grid=(n // tm,),
        in_specs=[pl.BlockSpec((tm, d), row), pl.BlockSpec((d, width), fix), pl.BlockSpec((1, Q_LORA), fix),
                  pl.BlockSpec((1, KV_LORA), fix), pl.BlockSpec((1, IDX_DIM), fix)],
        out_specs=[pl.BlockSpec((tm, Q_LORA), row), pl.BlockSpec((tm, KV_LORA), row),
                   pl.BlockSpec((tm, IDX_DIM), row), pl.BlockSpec((tm, LANES), row)],
        out_shape=[jax.ShapeDtypeStruct((n, Q_LORA), BF16), jax.ShapeDtypeStruct((n, KV_LORA), BF16),
                   jax.ShapeDtypeStruct((n, IDX_DIM), BF16), jax.ShapeDtypeStruct((n, LANES), F32)],
        compiler_params=_params("parallel"),
        name="dsa_in_proj",
    )(u, w_in_p, q_norm.reshape(1, -1), kv_norm.reshape(1, -1), kidx_norm.reshape(1, -1))


def _qabs_body(q_ref, w_ref, o_ref, *, r):
    for h in range(DSA_HEADS):
        res = jnp.dot(q_ref[:, h * DSA_HEAD_DIM:(h + 1) * DSA_HEAD_DIM], w_ref[h],
                      preferred_element_type=F32) * (DSA_HEAD_DIM ** -0.5 * LOG2_E)
        for rr in range(r):
            o_ref[rr, h] = res[rr * QBLOCK:(rr + 1) * QBLOCK].astype(o_ref.dtype)


def q_absorb(q_all, w_uk, tm=1024):
    n = q_all.shape[0]
    r = tm // QBLOCK
    hw = DSA_HEADS * DSA_HEAD_DIM
    return pl.pallas_call(
        functools.partial(_qabs_body, r=r),
        grid=(n // tm,),
        in_specs=[pl.BlockSpec((tm, hw), lambda i: (i, 0)),
                  pl.BlockSpec((DSA_HEADS, DSA_HEAD_DIM, KV_LORA), lambda i: (0, 0, 0))],
        out_specs=pl.BlockSpec((r, DSA_HEADS, QBLOCK, KV_LORA), lambda i: (i, 0, 0, 0)),
        out_shape=jax.ShapeDtypeStruct((n // QBLOCK, DSA_HEADS, QBLOCK, KV_LORA), BF16),
        compiler_params=_params("parallel"),
        name="q_absorb",
    )(q_all, w_uk)


def _vup_body(o_ref, w_ref, out_ref, *, r):
    for h in range(DSA_HEADS):
        x = o_ref[:, h].reshape(r * QBLOCK, KV_LORA)
        out_ref[:, h * DSA_HEAD_DIM:(h + 1) * DSA_HEAD_DIM] = jnp.dot(
            x, w_ref[h], preferred_element_type=F32).astype(out_ref.dtype)


def v_up(o_lat, w_uv, tm=1024):
    nb = o_lat.shape[0]
    r = tm // QBLOCK
    hw = DSA_HEADS * DSA_HEAD_DIM
    return pl.pallas_call(
        functools.partial(_vup_body, r=r),
        grid=(nb // r,),
        in_specs=[pl.BlockSpec((r, DSA_HEADS, QBLOCK, KV_LORA), lambda i: (i, 0, 0, 0)),
                  pl.BlockSpec((DSA_HEADS, KV_LORA, DSA_HEAD_DIM), lambda i: (0, 0, 0))],
        out_specs=pl.BlockSpec((tm, hw), lambda i: (i, 0)),
        out_shape=jax.ShapeDtypeStruct((nb * QBLOCK, hw), BF16),
        compiler_params=_params("parallel"),
        name="v_up",
    )(o_lat, w_uv)


def _dsa_core_body(qidx_ref, widx_ref, kidx_ref, ckv_ref, qabs_ref, o_ref,
                   keys_ref, tau_ref, count_ref, bias_ref, s_ref, p_ref, m_ref, l_ref, acc_ref, *, top_k, kt):
    i = pl.program_id(1)
    nq = QBLOCK
    nkt = keys_ref.shape[0]
    n_act = ((i + 1) * nq + kt - 1) // kt
    wt = widx_ref[0].T
    krow = lax.broadcasted_iota(jnp.int32, (kt, nq), 0)
    qcol = lax.broadcasted_iota(jnp.int32, (kt, nq), 1)
    q_chunk = (i * nq + qcol) // CHUNK
    tn = (((1,), (1,)), ((), ()))
    sub = 8

    def index_tile(jt, carry):
        k0 = pl.multiple_of(jt * kt, kt)
        kblk = kidx_ref[0, pl.ds(k0, kt), :]
        sc = jnp.zeros((kt, nq), F32)
        for h in range(IDX_HEADS):
            d = lax.dot_general(kblk, qidx_ref[0, :, h * IDX_DIM:(h + 1) * IDX_DIM], tn,
                                preferred_element_type=F32)
            sc = sc + jnp.maximum(d, 0.0) * wt[h:h + 1, :]
        sc = jnp.where(sc == 0.0, 0.0, sc)
        bits = pltpu.bitcast(sc, jnp.int32)
        key = bits ^ ((bits >> 31) & 0x7FFFFFFF)
        adm = ((k0 + krow) // CHUNK) <= q_chunk
        keys_ref[jt] = jnp.where(adm, key, INT_MIN)
        return carry

    lax.fori_loop(0, n_act, index_tile, 0)

    for jt in range(nkt):
        @pl.when(jt >= n_act)
        def _():
            keys_ref[jt] = jnp.full((kt, nq), INT_MIN, jnp.int32)

    def bisect_over(ntiles):
        def bisect(it, carry):
            prefix, at_prefix = carry
            cand = prefix | lax.shift_left(jnp.int32(1), 31 - it)
            cand_s = cand ^ INT_MIN
            cnt = jnp.zeros((sub, nq), F32)
            for jt in range(ntiles):
                hit = jnp.where(keys_ref[jt] >= cand_s, 1.0, 0.0)
                cnt = cnt + jnp.sum(hit.reshape(kt // sub, sub, nq), axis=0)
            cnt = jnp.sum(cnt, axis=0, keepdims=True)
            take = cnt >= top_k
            return jnp.where(take, cand, prefix), jnp.where(take, cnt, at_prefix)

        prefix, at_prefix = lax.fori_loop(0, 32, bisect,
                                          (jnp.zeros((1, nq), jnp.int32), jnp.zeros((1, nq), F32)))
        tau_ref[...] = jnp.broadcast_to(jnp.maximum(prefix ^ INT_MIN, INT_MIN + 1), tau_ref.shape)
        count_ref[...] = jnp.broadcast_to(at_prefix, count_ref.shape)

    search = (i + 1) * nq > top_k
    step = 2 if nkt % 2 == 0 else 1
    for ntiles in range(step, nkt + 1, step):
        @pl.when(search & (n_act > ntiles - step) & (n_act <= ntiles))
        def _():
            bisect_over(ntiles)

    @pl.when(jnp.logical_not(search))
    def _():
        tau_ref[...] = jnp.full(tau_ref.shape, INT_MIN + 1, jnp.int32)
        count_ref[...] = jnp.zeros_like(count_ref)

    tau = tau_ref[0:1, :]

    @pl.when(jnp.max(count_ref[0:1, :]) > top_k)
    def _():
        def count_above(jt, gt):
            return gt + jnp.sum(jnp.where(keys_ref[jt] > tau, 1.0, 0.0).reshape(kt // sub, sub, nq), axis=0)

        gt8 = lax.fori_loop(0, n_act, count_above, jnp.zeros((sub, nq), F32))
        need = top_k - jnp.sum(gt8, axis=0, keepdims=True)
        before_row = (lax.broadcasted_iota(jnp.int32, (kt, kt), 1)
                      < lax.broadcasted_iota(jnp.int32, (kt, kt), 0))
        tri = jnp.where(before_row, 1.0, 0.0).astype(BF16)

        def drop_surplus(jt, seen):
            kj = keys_ref[jt]
            eq = kj == tau
            eqf = jnp.where(eq, 1.0, 0.0)
            earlier = jnp.dot(tri, eqf.astype(BF16), preferred_element_type=F32) + seen
            keys_ref[jt] = jnp.where(eq & (earlier >= need), INT_MIN, kj)
            return seen + jnp.sum(eqf, axis=0, keepdims=True)

        lax.fori_loop(0, n_act, drop_surplus, jnp.zeros((1, nq), F32))

    rows = DSA_HEADS * nq
    rep = kt // LANES
    def attend_tile(jt, first):
        k0 = pl.multiple_of(jt * kt, kt)
        ck = ckv_ref[0, pl.ds(k0, kt), :]
        bias_ref[...] = jnp.where(keys_ref[jt] >= tau, 0.0, MASKED).T
        s_ref[...] = lax.dot_general(qabs_ref[0].reshape(rows, KV_LORA), ck, tn, preferred_element_type=F32)

        for h in range(DSA_HEADS):
            hr = slice(h * nq, (h + 1) * nq)
            s = s_ref[hr, :] + bias_ref[...]
            m_old = jnp.full((nq, LANES), M_INIT, F32) if first else m_ref[hr, :]
            m_new = jnp.maximum(m_old, jnp.max(s, axis=1, keepdims=True))
            p = jnp.exp2(s - jnp.concatenate([m_new] * rep, axis=1))
            psum = jnp.sum(p, axis=1, keepdims=True)
            if first:
                l_ref[hr, :] = jnp.broadcast_to(psum, (nq, LANES))
            else:
                alpha = jnp.exp2(m_old - m_new)
                l_ref[hr, :] = alpha * l_ref[hr, :] + psum
                acc_ref[hr, :] = acc_ref[hr, :] * jnp.concatenate([alpha] * (KV_LORA // LANES), axis=1)
            m_ref[hr, :] = m_new
            p_ref[hr, :] = p.astype(BF16)
        pv = jnp.dot(p_ref[...], ck, preferred_element_type=F32)
        if first:
            acc_ref[...] = pv
        else:
            acc_ref[...] += pv

    attend_tile(0, True)

    def later_tile(jt, carry):
        attend_tile(jt, False)
        return carry

    lax.fori_loop(1, n_act, later_tile, 0)
    inv_l = 1.0 / l_ref[...]
    out = acc_ref[...] * jnp.concatenate([inv_l] * (KV_LORA // LANES), axis=1)
    o_ref[0] = out.reshape(DSA_HEADS, nq, KV_LORA).astype(o_ref.dtype)


def dsa_core(q_all, widx, kidx, ckv, qabs, b, t, kt=256):
    nblk = t // QBLOCK
    top_k = min(IDX_TOPK, t // 4)
    hw = IDX_HEADS * IDX_DIM
    rows = DSA_HEADS * QBLOCK
    return pl.pallas_call(
        functools.partial(_dsa_core_body, top_k=top_k, kt=kt),
        grid=(b, nblk),
        in_specs=[
            pl.BlockSpec((1, QBLOCK, hw), lambda bi, i: (bi, i, 1)),
            pl.BlockSpec((1, QBLOCK, LANES), lambda bi, i: (bi, i, 0)),
            pl.BlockSpec((1, t, IDX_DIM), lambda bi, i: (bi, 0, 0)),
            pl.BlockSpec((1, t, KV_LORA), lambda bi, i: (bi, 0, 0)),
            pl.BlockSpec((1, DSA_HEADS, QBLOCK, KV_LORA), lambda bi, i: (bi * nblk + i, 0, 0, 0)),
        ],
        out_specs=pl.BlockSpec((1, DSA_HEADS, QBLOCK, KV_LORA), lambda bi, i: (bi * nblk + i, 0, 0, 0)),
        out_shape=jax.ShapeDtypeStruct((b * nblk, DSA_HEADS, QBLOCK, KV_LORA), BF16),
        scratch_shapes=[pltpu.VMEM((t // kt, kt, QBLOCK), jnp.int32), pltpu.VMEM((8, QBLOCK), jnp.int32),
                        pltpu.VMEM((8, QBLOCK), F32),
                        pltpu.VMEM((QBLOCK, kt), F32),
                        pltpu.VMEM((rows, kt), F32), pltpu.VMEM((rows, kt), BF16),
                        pltpu.VMEM((rows, LANES), F32), pltpu.VMEM((rows, LANES), F32),
                        pltpu.VMEM((rows, KV_LORA), F32)],
        compiler_params=_params("parallel", "arbitrary"),
        name="dsa_core",
    )(q_all.reshape(b, t, -1), widx.reshape(b, t, LANES), kidx.reshape(b, t, IDX_DIM),
      ckv.reshape(b, t, KV_LORA), qabs)


def dsa_layer(h2, u, b, t, next_gain, w_in, q_norm, kv_norm, kidx_norm, w_uq, w_uk, w_uv, w_out):
    width = Q_LORA + KV_LORA + IDX_DIM + LANES
    w_in_p = jnp.pad(w_in, ((0, 0), (0, width - w_in.shape[1]))).astype(BF16)
    qlat, ckv, kidx, widx = dsa_in_proj(u, w_in_p, q_norm, kv_norm, kidx_norm)
    q_all = matmul(qlat, w_uq.astype(BF16), BF16)
    qabs = q_absorb(q_all, w_uk.astype(BF16))
    o_lat = dsa_core(q_all, widx, kidx, ckv, qabs, b, t)
    o = v_up(o_lat, w_uv.astype(BF16))
    return matmul_res_norm(o, w_out.astype(BF16), h2, next_gain, BF16)


def mlp(h2, u, next_gain, u_dtype, w_up, w_down):
    a = matmul(u, w_up.astype(BF16), BF16, relu2=True, tn=2048)
    if jnp.dtype(u_dtype).itemsize == 2:
        return matmul_res_norm(a, w_down.astype(BF16), h2, next_gain, u_dtype, tm=1024, tk=1024, acc_in_h=True)
    return matmul_res_norm(a, w_down.astype(BF16), h2, next_gain, u_dtype)


def kernel(x, norm_mix, norm_mlp, norm_final, dn_w_in, dn_conv_w, dn_a_log, dn_dt_bias, dn_out_norm, dn_w_out, dsa_w_in, dsa_q_norm, dsa_kv_norm, dsa_kidx_norm, dsa_w_uq, dsa_w_uk, dsa_w_uv, dsa_w_out, mlp_w_up, mlp_w_down):
    b, t, d = x.shape
    depth = norm_mix.shape[0]
    h2 = x.reshape(b * t, d)
    u = rms_norm(h2, norm_mix[0], BF16)
    for i in range(depth):
        j = i // 2
        if i % 2 == 0:
            h2, u = gated_deltanet_layer(h2, u, b, t, norm_mlp[i], dn_w_in[j], dn_conv_w[j], dn_a_log[j],
                                         dn_dt_bias[j], dn_out_norm[j], dn_w_out[j])
        else:
            h2, u = dsa_layer(h2, u, b, t, norm_mlp[i], dsa_w_in[j], dsa_q_norm[j], dsa_kv_norm[j],
                              dsa_kidx_norm[j], dsa_w_uq[j], dsa_w_uk[j], dsa_w_uv[j], dsa_w_out[j])
        last = i == depth - 1
        h2, u = mlp(h2, u, norm_final if last else norm_mix[i + 1], x.dtype if last else BF16,
                    mlp_w_up[i], mlp_w_down[i])
    return u.reshape(b, t, d)
```

```python
import functools

import jax
import jax.numpy as jnp
from jax import lax
from jax.experimental import pallas as pl
from jax.experimental.pallas import tpu as pltpu

F32 = jnp.float32
BF16 = jnp.bfloat16
EPS = 1e-6
CHUNK = 64
QBLOCK = 128
DN_QK_HEADS = 16
DN_V_HEADS = 32
DN_HEAD_DIM = 128
DN_KEY_DIM = DN_QK_HEADS * DN_HEAD_DIM
DN_VAL_DIM = DN_V_HEADS * DN_HEAD_DIM
DN_CONV_DIM = 2 * DN_KEY_DIM + DN_VAL_DIM
DSA_HEADS = 16
DSA_HEAD_DIM = 128
Q_LORA = 512
KV_LORA = 256
IDX_HEADS = 16
IDX_DIM = 128
IDX_TOPK = 256
LANES = 128
VMEM_LIMIT_BYTES = 56 * 1024 * 1024
INT_MIN = -(2 ** 31)
MASKED = -1e30
M_INIT = -1e20
LOG2_E = 1.4426950408889634


def _params(*sem):
    return pltpu.CompilerParams(dimension_semantics=sem, vmem_limit_bytes=VMEM_LIMIT_BYTES)


def _sigmoid(x):
    return 1.0 / (1.0 + jnp.exp2(x * (-LOG2_E)))


def _norm_body(x_ref, g_ref, o_ref):
    x = x_ref[...]
    y = x * lax.rsqrt(jnp.mean(x * x, axis=-1, keepdims=True) + EPS) * g_ref[...]
    o_ref[...] = y.astype(o_ref.dtype)


def rms_norm(x, g, out_dtype, tm=512):
    n, d = x.shape
    return pl.pallas_call(
        _norm_body,
        grid=(n // tm,),
        in_specs=[pl.BlockSpec((tm, d), lambda i: (i, 0)), pl.BlockSpec((1, d), lambda i: (0, 0))],
        out_specs=pl.BlockSpec((tm, d), lambda i: (i, 0)),
        out_shape=jax.ShapeDtypeStruct((n, d), out_dtype),
        compiler_params=_params("parallel"),
        name="rms_norm",
    )(x, g.reshape(1, d))


MM_SUB_COLS = 512


def _mm_body(a_ref, w_ref, *rest, nk, relu2, residual):
    if residual:
        r_ref, o_ref = rest[0], rest[1]
    else:
        r_ref, o_ref = None, rest[0]
    acc_ref = rest[-1] if nk > 1 else None

    def finish(acc, cols=slice(None)):
        if relu2:
            acc = jnp.square(jnp.maximum(acc, 0.0))
        if residual:
            acc = acc + r_ref[:, cols]
        o_ref[:, cols] = acc.astype(o_ref.dtype)

    sub = min(MM_SUB_COLS, o_ref.shape[1])
    blocks = [slice(c0, c0 + sub) for c0 in range(0, o_ref.shape[1], sub)]

    def k_step(first, last):
        for cols in blocks:
            part = jnp.dot(a_ref[...], w_ref[:, cols], preferred_element_type=F32)
            if not first:
                part = part + acc_ref[:, cols]
            if last:
                finish(part, cols)
            else:
                acc_ref[:, cols] = part

    if nk == 1:
        k_step(True, True)
        return
    k = pl.program_id(2)
    pl.when(k == 0)(functools.partial(k_step, True, False))
    pl.when((k > 0) & (k < nk - 1))(functools.partial(k_step, False, False))
    pl.when(k == nk - 1)(functools.partial(k_step, False, True))


def matmul(a, w, out_dtype, *, relu2=False, residual=None, tm=1024, tn=1024, tk=2048):
    m, kdim = a.shape
    n = w.shape[1]
    tm, tn, tk = min(tm, m), min(tn, n), min(tk, kdim)
    assert m % tm == 0 and n % tn == 0 and kdim % tk == 0
    nk = kdim // tk
    in_specs = [pl.BlockSpec((tm, tk), lambda i, j, k: (i, k)), pl.BlockSpec((tk, tn), lambda i, j, k: (k, j))]
    args = [a, w]
    if residual is not None:
        in_specs.append(pl.BlockSpec((tm, tn), lambda i, j, k: (i, j)))
        args.append(residual)
    return pl.pallas_call(
        functools.partial(_mm_body, nk=nk, relu2=relu2, residual=residual is not None),
        grid=(m // tm, n // tn, nk),
        in_specs=in_specs,
        out_specs=pl.BlockSpec((tm, tn), lambda i, j, k: (i, j)),
        out_shape=jax.ShapeDtypeStruct((m, n), out_dtype),
        scratch_shapes=[pltpu.VMEM((tm, tn), F32)] if nk > 1 else [],
        compiler_params=_params("parallel", "parallel", "arbitrary"),
        name="matmul",
    )(*args)


def _mm_res_norm_body(a_ref, w_ref, r_ref, g_ref, h_ref, u_ref, *scratch, nk):
    acc_ref = scratch[0] if scratch else None
    n = h_ref.shape[1]
    sub = min(MM_SUB_COLS, n)
    blocks = [slice(c0, c0 + sub) for c0 in range(0, n, sub)]

    def k_step(first, last):
        for cols in blocks:
            part = jnp.dot(a_ref[...], w_ref[:, cols], preferred_element_type=F32)
            if acc_ref is None:
                h_ref[:, cols] = part + (r_ref[:, cols] if first else h_ref[:, cols])
                continue
            if not first:
                part = part + acc_ref[:, cols]
            if last:
                h_ref[:, cols] = part + r_ref[:, cols]
            else:
                acc_ref[:, cols] = part
        if last:
            h = h_ref[...]
            scale = lax.rsqrt(jnp.mean(h * h, axis=-1, keepdims=True) + EPS)
            u_ref[...] = (h * scale * g_ref[...]).astype(u_ref.dtype)

    if nk == 1:
        k_step(True, True)
        return
    k = pl.program_id(1)
    pl.when(k == 0)(functools.partial(k_step, True, False))
    pl.when((k > 0) & (k < nk - 1))(functools.partial(k_step, False, False))
    pl.when(k == nk - 1)(functools.partial(k_step, False, True))


def matmul_res_norm(a, w, residual, gain, u_dtype, *, tm=512, tk=2048, acc_in_h=False):
    m, kdim = a.shape
    n = w.shape[1]
    tm, tk = min(tm, m), min(tk, kdim)
    assert m % tm == 0 and kdim % tk == 0
    nk = kdim // tk
    row = pl.BlockSpec((tm, n), lambda i, k: (i, 0))
    return pl.pallas_call(
        functools.partial(_mm_res_norm_body, nk=nk),
        grid=(m // tm, nk),
        in_specs=[pl.BlockSpec((tm, tk), lambda i, k: (i, k)), pl.BlockSpec((tk, n), lambda i, k: (k, 0)),
                  row, pl.BlockSpec((1, n), lambda i, k: (0, 0))],
        out_specs=[row, row],
        out_shape=[jax.ShapeDtypeStruct((m, n), F32), jax.ShapeDtypeStruct((m, n), u_dtype)],
        scratch_shapes=[pltpu.VMEM((tm, n), F32)] if nk > 1 and not acc_in_h else [],
        compiler_params=_params("parallel", "arbitrary"),
        name="matmul_res_norm",
    )(a, w, residual, gain.reshape(1, n))


def _dn_gates_body(u_ref, w_ref, alog_ref, dt_ref, o_ref):
    x = jnp.dot(u_ref[...], w_ref[...], preferred_element_type=F32)
    g = (-LOG2_E) * jnp.exp(alog_ref[...]) * (jnp.maximum(x + dt_ref[...], 0.0)
                                              + jnp.log(1.0 + jnp.exp(-jnp.abs(x + dt_ref[...]))))
    pos = lax.broadcasted_iota(jnp.int32, x.shape, 0) % CHUNK
    s = 1
    while s < CHUNK:
        g = g + jnp.where(pos >= s, pltpu.roll(g, s, axis=0), 0.0)
        s *= 2
    lane = lax.broadcasted_iota(jnp.int32, x.shape, 1)
    o_ref[...] = jnp.where(lane < DN_V_HEADS, _sigmoid(x), g)


def dn_gates(u, w_ba, alog_pad, dt_pad, tm=512):
    n, d = u.shape
    return pl.pallas_call(
        _dn_gates_body,
        grid=(n // tm,),
        in_specs=[pl.BlockSpec((tm, d), lambda i: (i, 0)),
                  pl.BlockSpec((d, LANES), lambda i: (0, 0)),
                  pl.BlockSpec((1, LANES), lambda i: (0, 0)),
                  pl.BlockSpec((1, LANES), lambda i: (0, 0))],
        out_specs=pl.BlockSpec((tm, LANES), lambda i: (i, 0)),
        out_shape=jax.ShapeDtypeStruct((n, LANES), F32),
        compiler_params=_params("parallel"),
        name="dn_gates",
    )(u, w_ba, alog_pad, dt_pad)


def _conv_silu(ref, cw_ref, t0, tt):
    cw = cw_ref[...]
    acc = ref[0, t0:t0 + tt, :] * cw[3:4, :]
    for s in (1, 2, 3):
        if t0 == 0:
            x = ref[0, 0:tt, :]
            rows = lax.broadcasted_iota(jnp.int32, x.shape, 0)
            xs = jnp.where(rows >= s, pltpu.roll(x, s, axis=0), 0.0)
        else:
            xs = ref[0, t0 - s:t0 - s + tt, :]
        acc = acc + xs * cw[3 - s:4 - s, :]
    return acc * _sigmoid(acc)


def _l2norm(t, scale=1.0):
    return t * (lax.rsqrt(jnp.sum(t * t, axis=-1, keepdims=True) + EPS) * scale)


def _bmm(a, b):
    return jnp.einsum("nij,njk->nik", a.astype(BF16), b.astype(BF16), preferred_element_type=F32)


def _bmm_nt(a, b):
    return jnp.einsum("nid,njd->nij", a.astype(BF16), b.astype(BF16), preferred_element_type=F32)


DN_GROUP = 32


def _dn_prep_body(q_ref, k_ref, v_ref, cwq_ref, cwk_ref, cwv_ref, gates_ref, gcr_ref,
                  u_ref, w_ref, qd_ref, kd_ref, a_ref, qs_ref, ks_ref, vs_ref, bb_ref, gb_ref, *, t_len):
    j = pl.program_id(1)
    tt = 256
    hd = DN_HEAD_DIM
    for ti in range(t_len // tt):
        t0 = ti * tt
        qs_ref[t0:t0 + tt, :] = _l2norm(_conv_silu(q_ref, cwq_ref, t0, tt), hd ** -0.5)
        ks_ref[t0:t0 + tt, :] = _l2norm(_conv_silu(k_ref, cwk_ref, t0, tt))
        vs_ref[t0:t0 + tt, :] = _conv_silu(v_ref, cwv_ref, t0, tt)
        gt = gates_ref[0, t0:t0 + tt, :]
        lane = lax.broadcasted_iota(jnp.int32, gt.shape, 1)
        for hh in range(2):
            col = 2 * j + hh
            bsel = jnp.sum(jnp.where(lane == col, gt, 0.0), axis=1, keepdims=True)
            gsel = jnp.sum(jnp.where(lane == col + DN_V_HEADS, gt, 0.0), axis=1, keepdims=True)
            bb_ref[hh, t0:t0 + tt, :] = jnp.broadcast_to(bsel, (tt, LANES))
            gb_ref[hh, t0:t0 + tt, :] = jnp.broadcast_to(gsel, (tt, LANES))

    c = CHUNK
    g = min(DN_GROUP, t_len // c)
    rows = g * c
    ri = lax.broadcasted_iota(jnp.int32, (c, c), 0)
    ci = lax.broadcasted_iota(jnp.int32, (c, c), 1)
    lower = ri >= ci
    strict = ri > ci
    eye = jnp.where(ri == ci, 1.0, 0.0)
    pair_mask = strict & ((ri // 2) == (ci // 2))
    merge_masks = [((ri // (2 * s)) == (ci // (2 * s))) & ((ri // s) != (ci // s)) & strict
                   for s in (2, 4, 8, 16, 32)]

    def group_step(gi, carry):
        base = pl.multiple_of(gi * rows, rows)
        kc = ks_ref[pl.ds(base, rows), :].reshape(g, c, hd)
        qc = qs_ref[pl.ds(base, rows), :].reshape(g, c, hd)
        kk1 = _bmm_nt(kc, kc)
        qk1 = _bmm_nt(qc, kc)
        two = lambda x: jnp.concatenate([x, x], axis=0)
        kk, qk, kc2, qc2 = two(kk1), two(qk1), two(kc), two(qc)
        beta_b = jnp.concatenate([bb_ref[hh, pl.ds(base, rows), :].reshape(g, c, LANES) for hh in range(2)], axis=0)
        gcc_b = jnp.concatenate([gb_ref[hh, pl.ds(base, rows), :].reshape(g, c, LANES) for hh in range(2)], axis=0)
        grow = gcr_ref[0, 0, :, pl.ds(base, rows)]
        gcr = jnp.stack([grow[hh:hh + 1, n * c:(n + 1) * c] for hh in range(2) for n in range(g)], axis=0)
        vc = jnp.concatenate([vs_ref[pl.ds(base, rows), hh * hd:(hh + 1) * hd].reshape(g, c, hd)
                              for hh in range(2)], axis=0)
        gl_b = gcc_b[:, c - 1:c, :]
        dm = gcc_b[:, :, :c] - gcr
        decay = jnp.where(lower, jnp.exp2(dm), 0.0)
        lm = jnp.where(strict, kk * beta_b[:, :, :c] * decay, 0.0)
        tb = (eye - jnp.where(pair_mask, lm, 0.0)).astype(BF16)
        lmb = lm.astype(BF16)
        zero_b = jnp.zeros_like(lmb)
        for mk in merge_masks:
            tb = tb - _bmm(_bmm(tb, jnp.where(mk, lmb, zero_b)), tb).astype(BF16)
        eg = jnp.exp2(gcc_b)
        uw = _bmm(tb, jnp.concatenate([vc * beta_b, kc2 * (beta_b * eg)], axis=2))
        qd = qc2 * eg
        kd = kc2 * jnp.exp2(gl_b - gcc_b)
        am = qk * decay
        for hh in range(2):
            cols = slice(hh * hd, (hh + 1) * hd)
            inst = slice(hh * g, (hh + 1) * g)
            u_ref[0, pl.ds(base, rows), cols] = uw[inst, :, :hd].reshape(rows, hd).astype(u_ref.dtype)
            w_ref[0, pl.ds(base, rows), cols] = uw[inst, :, hd:].reshape(rows, hd).astype(w_ref.dtype)
            qd_ref[0, pl.ds(base, rows), cols] = qd[inst].reshape(rows, hd).astype(qd_ref.dtype)
            kd_ref[0, pl.ds(base, rows), cols] = kd[inst].reshape(rows, hd).astype(kd_ref.dtype)
            a_ref[0, pl.ds(base, rows), hh * c:(hh + 1) * c] = am[inst].reshape(rows, c).astype(a_ref.dtype)
        return carry

    lax.fori_loop(0, t_len // rows, group_step, 0)


def dn_prep(proj, conv_w, gates, gcr):
    b, t, _ = proj.shape
    hd = DN_HEAD_DIM
    nq = DN_QK_HEADS
    v_blk0 = 2 * DN_KEY_DIM // (2 * hd)
    wide = pl.BlockSpec((1, t, 2 * hd), lambda bi, j: (bi, 0, j))
    big = jax.ShapeDtypeStruct((b, t, DN_VAL_DIM), BF16)
    return pl.pallas_call(
        functools.partial(_dn_prep_body, t_len=t),
        grid=(b, nq),
        in_specs=[
            pl.BlockSpec((1, t, hd), lambda bi, j: (bi, 0, j)),
            pl.BlockSpec((1, t, hd), lambda bi, j: (bi, 0, nq + j)),
            pl.BlockSpec((1, t, 2 * hd), lambda bi, j: (bi, 0, v_blk0 + j)),
            pl.BlockSpec((4, hd), lambda bi, j: (0, j)),
            pl.BlockSpec((4, hd), lambda bi, j: (0, nq + j)),
            pl.BlockSpec((4, 2 * hd), lambda bi, j: (0, v_blk0 + j)),
            pl.BlockSpec((1, t, LANES), lambda bi, j: (bi, 0, 0)),
            pl.BlockSpec((1, 1, 2, t), lambda bi, j: (bi, j, 0, 0)),
        ],
        out_specs=[wide, wide, wide, wide, pl.BlockSpec((1, t, 2 * CHUNK), lambda bi, j: (bi, 0, j))],
        out_shape=[big, big, big, big, jax.ShapeDtypeStruct((b, t, DN_V_HEADS * CHUNK), BF16)],
        scratch_shapes=[
            pltpu.VMEM((t, hd), F32), pltpu.VMEM((t, hd), F32), pltpu.VMEM((t, 2 * hd), F32),
            pltpu.VMEM((2, t, LANES), F32), pltpu.VMEM((2, t, LANES), F32),
        ],
        compiler_params=_params("parallel", "arbitrary"),
        name="dn_prep",
    )(proj, proj, proj, conv_w, conv_w, conv_w, gates, gcr)


DN_REC_HEADS = 32
DN_REC_ROWS = 256


def _dn_rec_body(u_ref, w_ref, qd_ref, kd_ref, a_ref, z_ref, gl_ref, onorm_ref, o_ref, s_ref, *, t_len):
    hd = DN_HEAD_DIM
    c = CHUNK

    @pl.when(pl.program_id(2) == 0)
    def _():
        s_ref[...] = jnp.zeros_like(s_ref)

    onorm = onorm_ref[...]

    def step(ci, carry):
        r0 = pl.multiple_of(ci * c, c)
        egl = jnp.exp2(gl_ref[0, 0, ci])
        heads = range(DN_REC_HEADS)
        cols = [slice(h * hd, (h + 1) * hd) for h in heads]
        st = [s_ref[h] for h in heads]
        r = [jnp.dot(jnp.concatenate([w_ref[0, pl.ds(r0, c), cols[h]], qd_ref[0, pl.ds(r0, c), cols[h]]], axis=0),
                     st[h].astype(BF16), preferred_element_type=F32) for h in heads]
        vb = [(u_ref[0, pl.ds(r0, c), cols[h]].astype(F32) - r[h][:c]).astype(BF16) for h in heads]
        upd = [lax.dot_general(kd_ref[0, pl.ds(r0, c), cols[h]], vb[h], (((0,), (0,)), ((), ())),
                               preferred_element_type=F32) for h in heads]
        o = [r[h][c:] + jnp.dot(a_ref[0, pl.ds(r0, c), h * c:(h + 1) * c], vb[h], preferred_element_type=F32)
             for h in heads]
        for h in heads:
            s_ref[h] = st[h] * egl[:, h:h + 1] + upd[h]
        for h in heads:
            y = o[h] * lax.rsqrt(jnp.mean(o[h] * o[h], axis=-1, keepdims=True) + EPS) * onorm
            zz = z_ref[0, pl.ds(r0, c), cols[h]]
            o_ref[0, pl.ds(r0, c), cols[h]] = (y * (zz * _sigmoid(zz))).astype(o_ref.dtype)
        return carry

    lax.fori_loop(0, t_len // c, step, 0)


def dn_rec(u, w, qd, kd, a, proj, gl, out_norm):
    b, t, _ = u.shape
    hd = DN_HEAD_DIM
    hb = DN_REC_HEADS
    ng = DN_V_HEADS // hb
    tt = min(DN_REC_ROWS, t)
    z_blk0 = DN_CONV_DIM // (hb * hd)
    wide = pl.BlockSpec((1, tt, hb * hd), lambda bi, g, ti: (bi, ti, g))
    return pl.pallas_call(
        functools.partial(_dn_rec_body, t_len=tt),
        grid=(b, ng, t // tt),
        in_specs=[
            wide, wide, wide, wide,
            pl.BlockSpec((1, tt, hb * CHUNK), lambda bi, g, ti: (bi, ti, g)),
            pl.BlockSpec((1, tt, hb * hd), lambda bi, g, ti: (bi, ti, z_blk0 + g)),
            pl.BlockSpec((1, 1, tt // CHUNK, 1, hb), lambda bi, g, ti: (bi, g, ti, 0, 0)),
            pl.BlockSpec((1, hd), lambda bi, g, ti: (0, 0)),
        ],
        out_specs=wide,
        out_shape=jax.ShapeDtypeStruct((b, t, DN_VAL_DIM), BF16),
        scratch_shapes=[pltpu.VMEM((hb, hd, hd), F32)],
        compiler_params=_params("parallel", "parallel", "arbitrary"),
        name="dn_rec",
    )(u, w, qd, kd, a, proj, gl, out_norm.reshape(1, hd))


def gated_deltanet_layer(h2, u, b, t, next_gain, w_in, conv_w, a_log, dt_bias, out_norm, w_out):
    n = b * t
    n_qkvz = DN_CONV_DIM + DN_VAL_DIM
    proj = matmul(u, w_in[:, :n_qkvz].astype(BF16), F32, tn=2048).reshape(b, t, n_qkvz)
    w_ba = jnp.pad(w_in[:, n_qkvz:], ((0, 0), (0, LANES - 2 * DN_V_HEADS))).astype(BF16)
    pad = lambda p: jnp.pad(p.astype(F32), (DN_V_HEADS, LANES - 2 * DN_V_HEADS)).reshape(1, LANES)
    gates = dn_gates(u, w_ba, pad(a_log), pad(dt_bias))
    gc = gates[:, DN_V_HEADS:2 * DN_V_HEADS].reshape(b, t, DN_V_HEADS)
    gcr = gc.reshape(b, t, DN_QK_HEADS, 2).transpose(0, 2, 3, 1)
    ng = DN_V_HEADS // DN_REC_HEADS
    gl = gc[:, CHUNK - 1::CHUNK, :].reshape(b, t // CHUNK, ng, 1, DN_REC_HEADS).transpose(0, 2, 1, 3, 4)
    uu, ww, qd, kd, am = dn_prep(proj, conv_w, gates.reshape(b, t, LANES), gcr)
    o = dn_rec(uu, ww, qd, kd, am, proj, gl, out_norm)
    return matmul_res_norm(o.reshape(n, DN_VAL_DIM), w_out.astype(BF16), h2, next_gain, BF16,
                           tm=1024, tk=1024, acc_in_h=True)


def _dsa_in_body(u_ref, w_ref, qn_ref, kvn_ref, kin_ref, q_out, kv_out, ki_out, w_out):
    def nrm(x, g):
        return x * lax.rsqrt(jnp.mean(x * x, axis=-1, keepdims=True) + EPS) * g

    a, b2, c2 = Q_LORA, Q_LORA + KV_LORA, Q_LORA + KV_LORA + IDX_DIM
    dot = lambda lo, hi: jnp.dot(u_ref[...], w_ref[:, lo:hi], preferred_element_type=F32)
    q_out[...] = nrm(dot(0, a), qn_ref[...]).astype(q_out.dtype)
    kv_out[...] = nrm(dot(a, b2), kvn_ref[...]).astype(kv_out.dtype)
    ki_out[...] = nrm(dot(b2, c2), kin_ref[...]).astype(ki_out.dtype)
    w_out[...] = dot(c2, c2 + LANES) * (IDX_HEADS ** -0.5 * IDX_DIM ** -0.5)


def dsa_in_proj(u, w_in_p, q_norm, kv_norm, kidx_norm, tm=1024):
    n, d = u.shape
    width = w_in_p.shape[1]
    row = lambda i: (i, 0)
    fix = lambda i: (0, 0)
    return pl.pallas_call(
        _dsa_in_body,
        grid=(n // tm,),
        in_specs=[pl.BlockSpec((tm, d), row), pl.BlockSpec((d, width), fix), pl.BlockSpec((1, Q_LORA), fix),
                  pl.BlockSpec((1, KV_LORA), fix), pl.BlockSpec((1, IDX_DIM), fix)],
        out_specs=[pl.BlockSpec((tm, Q_LORA), row), pl.BlockSpec((tm, KV_LORA), row),
                   pl.BlockSpec((tm, IDX_DIM), row), pl.BlockSpec((tm, LANES), row)],
        out_shape=[jax.ShapeDtypeStruct((n, Q_LORA), BF16), jax.ShapeDtypeStruct((n, KV_LORA), BF16),
                   jax.ShapeDtypeStruct((n, IDX_DIM), BF16), jax.ShapeDtypeStruct((n, LANES), F32)],
        compiler_params=_params("parallel"),
        name="dsa_in_proj",
    )(u, w_in_p, q_norm.reshape(1, -1), kv_norm.reshape(1, -1), kidx_norm.reshape(1, -1))


def _qabs_body(q_ref, w_ref, o_ref, *, r):
    for h in range(DSA_HEADS):
        res = jnp.dot(q_ref[:, h * DSA_HEAD_DIM:(h + 1) * DSA_HEAD_DIM], w_ref[h],
                      preferred_element_type=F32) * (DSA_HEAD_DIM ** -0.5 * LOG2_E)
        for rr in range(r):
            o_ref[rr, h] = res[rr * QBLOCK:(rr + 1) * QBLOCK].astype(o_ref.dtype)


def q_absorb(q_all, w_uk, tm=1024):
    n = q_all.shape[0]
    r = tm // QBLOCK
    hw = DSA_HEADS * DSA_HEAD_DIM
    return pl.pallas_call(
        functools.partial(_qabs_body, r=r),
        grid=(n // tm,),
        in_specs=[pl.BlockSpec((tm, hw), lambda i: (i, 0)),
                  pl.BlockSpec((DSA_HEADS, DSA_HEAD_DIM, KV_LORA), lambda i: (0, 0, 0))],
        out_specs=pl.BlockSpec((r, DSA_HEADS, QBLOCK, KV_LORA), lambda i: (i, 0, 0, 0)),
        out_shape=jax.ShapeDtypeStruct((n // QBLOCK, DSA_HEADS, QBLOCK, KV_LORA), BF16),
        compiler_params=_params("parallel"),
        name="q_absorb",
    )(q_all, w_uk)


def _vup_body(o_ref, w_ref, out_ref, *, r):
    for h in range(DSA_HEADS):
        x = o_ref[:, h].reshape(r * QBLOCK, KV_LORA)
        out_ref[:, h * DSA_HEAD_DIM:(h + 1) * DSA_HEAD_DIM] = jnp.dot(
            x, w_ref[h], preferred_element_type=F32).astype(out_ref.dtype)


def v_up(o_lat, w_uv, tm=1024):
    nb = o_lat.shape[0]
    r = tm // QBLOCK
    hw = DSA_HEADS * DSA_HEAD_DIM
    return pl.pallas_call(
        functools.partial(_vup_body, r=r),
        grid=(nb // r,),
        in_specs=[pl.BlockSpec((r, DSA_HEADS, QBLOCK, KV_LORA), lambda i: (i, 0, 0, 0)),
                  pl.BlockSpec((DSA_HEADS, KV_LORA, DSA_HEAD_DIM), lambda i: (0, 0, 0))],
        out_specs=pl.BlockSpec((tm, hw), lambda i: (i, 0)),
        out_shape=jax.ShapeDtypeStruct((nb * QBLOCK, hw), BF16),
        compiler_params=_params("parallel"),
        name="v_up",
    )(o_lat, w_uv)


def _dsa_core_body(qidx_ref, widx_ref, kidx_ref, ckv_ref, qabs_ref, o_ref,
                   keys_ref, tau_ref, count_ref, bias_ref, s_ref, p_ref, m_ref, l_ref, acc_ref, *, top_k, kt):
    i = pl.program_id(1)
    nq = QBLOCK
    nkt = keys_ref.shape[0]
    n_act = ((i + 1) * nq + kt - 1) // kt
    wt = widx_ref[0].T
    krow = lax.broadcasted_iota(jnp.int32, (kt, nq), 0)
    qcol = lax.broadcasted_iota(jnp.int32, (kt, nq), 1)
    q_chunk = (i * nq + qcol) // CHUNK
    tn = (((1,), (1,)), ((), ()))
    sub = 8

    def index_tile(jt, carry):
        k0 = pl.multiple_of(jt * kt, kt)
        kblk = kidx_ref[0, pl.ds(k0, kt), :]
        sc = jnp.zeros((kt, nq), F32)
        for h in range(IDX_HEADS):
            d = lax.dot_general(kblk, qidx_ref[0, :, h * IDX_DIM:(h + 1) * IDX_DIM], tn,
                                preferred_element_type=F32)
            sc = sc + jnp.maximum(d, 0.0) * wt[h:h + 1, :]
        sc = jnp.where(sc == 0.0, 0.0, sc)
        bits = pltpu.bitcast(sc, jnp.int32)
        key = bits ^ ((bits >> 31) & 0x7FFFFFFF)
        adm = ((k0 + krow) // CHUNK) <= q_chunk
        keys_ref[jt] = jnp.where(adm, key, INT_MIN)
        return carry

    lax.fori_loop(0, n_act, index_tile, 0)

    for jt in range(nkt):
        @pl.when(jt >= n_act)
        def _():
            keys_ref[jt] = jnp.full((kt, nq), INT_MIN, jnp.int32)

    def bisect_over(ntiles):
        def bisect(it, carry):
            prefix, at_prefix = carry
            cand = prefix | lax.shift_left(jnp.int32(1), 31 - it)
            cand_s = cand ^ INT_MIN
            cnt = jnp.zeros((sub, nq), F32)
            for jt in range(ntiles):
                hit = jnp.where(keys_ref[jt] >= cand_s, 1.0, 0.0)
                cnt = cnt + jnp.sum(hit.reshape(kt // sub, sub, nq), axis=0)
            cnt = jnp.sum(cnt, axis=0, keepdims=True)
            take = cnt >= top_k
            return jnp.where(take, cand, prefix), jnp.where(take, cnt, at_prefix)

        prefix, at_prefix = lax.fori_loop(0, 32, bisect,
                                          (jnp.zeros((1, nq), jnp.int32), jnp.zeros((1, nq), F32)))
        tau_ref[...] = jnp.broadcast_to(jnp.maximum(prefix ^ INT_MIN, INT_MIN + 1), tau_ref.shape)
        count_ref[...] = jnp.broadcast_to(at_prefix, count_ref.shape)

    search = (i + 1) * nq > top_k
    step = 2 if nkt % 2 == 0 else 1
    for ntiles in range(step, nkt + 1, step):
        @pl.when(search & (n_act > ntiles - step) & (n_act <= ntiles))
        def _():
            bisect_over(ntiles)

    @pl.when(jnp.logical_not(search))
    def _():
        tau_ref[...] = jnp.full(tau_ref.shape, INT_MIN + 1, jnp.int32)
        count_ref[...] = jnp.zeros_like(count_ref)

    tau = tau_ref[0:1, :]

    @pl.when(jnp.max(count_ref[0:1, :]) > top_k)
    def _():
        def count_above(jt, gt):
            return gt + jnp.sum(jnp.where(keys_ref[jt] > tau, 1.0, 0.0).reshape(kt // sub, sub, nq), axis=0)

        gt8 = lax.fori_loop(0, n_act, count_above, jnp.zeros((sub, nq), F32))
        need = top_k - jnp.sum(gt8, axis=0, keepdims=True)
        before_row = (lax.broadcasted_iota(jnp.int32, (kt, kt), 1)
                      < lax.broadcasted_iota(jnp.int32, (kt, kt), 0))
        tri = jnp.where(before_row, 1.0, 0.0).astype(BF16)

        def drop_surplus(jt, seen):
            kj = keys_ref[jt]
            eq = kj == tau
            eqf = jnp.where(eq, 1.0, 0.0)
            earlier = jnp.dot(tri, eqf.astype(BF16), preferred_element_type=F32) + seen
            keys_ref[jt] = jnp.where(eq & (earlier >= need), INT_MIN, kj)
            return seen + jnp.sum(eqf, axis=0, keepdims=True)

        lax.fori_loop(0, n_act, drop_surplus, jnp.zeros((1, nq), F32))

    rows = DSA_HEADS * nq
    rep = kt // LANES
    def attend_tile(jt, first):
        k0 = pl.multiple_of(jt * kt, kt)
        ck = ckv_ref[0, pl.ds(k0, kt), :]
        bias_ref[...] = jnp.where(keys_ref[jt] >= tau, 0.0, MASKED).T
        s_ref[...] = lax.dot_general(qabs_ref[0].reshape(rows, KV_LORA), ck, tn, preferred_element_type=F32)

        for h in range(DSA_HEADS):
            hr = slice(h * nq, (h + 1) * nq)
            s = s_ref[hr, :] + bias_ref[...]
            m_old = jnp.full((nq, LANES), M_INIT, F32) if first else m_ref[hr, :]
            m_new = jnp.maximum(m_old, jnp.max(s, axis=1, keepdims=True))
            p = jnp.exp2(s - jnp.concatenate([m_new] * rep, axis=1))
            psum = jnp.sum(p, axis=1, keepdims=True)
            if first:
                l_ref[hr, :] = jnp.broadcast_to(psum, (nq, LANES))
            else:
                alpha = jnp.exp2(m_old - m_new)
                l_ref[hr, :] = alpha * l_ref[hr, :] + psum
                acc_ref[hr, :] = acc_ref[hr, :] * jnp.concatenate([alpha] * (KV_LORA // LANES), axis=1)
            m_ref[hr, :] = m_new
            p_ref[hr, :] = p.astype(BF16)
        pv = jnp.dot(p_ref[...], ck, preferred_element_type=F32)
        if first:
            acc_ref[...] = pv
        else:
            acc_ref[...] += pv

    attend_tile(0, True)

    def later_tile(jt, carry):
        attend_tile(jt, False)
        return carry

    lax.fori_loop(1, n_act, later_tile, 0)
    inv_l = 1.0 / l_ref[...]
    out = acc_ref[...] * jnp.concatenate([inv_l] * (KV_LORA // LANES), axis=1)
    o_ref[0] = out.reshape(DSA_HEADS, nq, KV_LORA).astype(o_ref.dtype)


def dsa_core(q_all, widx, kidx, ckv, qabs, b, t, kt=256):
    nblk = t // QBLOCK
    top_k = min(IDX_TOPK, t // 4)
    hw = IDX_HEADS * IDX_DIM
    rows = DSA_HEADS * QBLOCK
    return pl.pallas_call(
        functools.partial(_dsa_core_body, top_k=top_k, kt=kt),
        grid=(b, nblk),
        in_specs=[
            pl.BlockSpec((1, QBLOCK, hw), lambda bi, i: (bi, i, 1)),
            pl.BlockSpec((1, QBLOCK, LANES), lambda bi, i: (bi, i, 0)),
            pl.BlockSpec((1, t, IDX_DIM), lambda bi, i: (bi, 0, 0)),
            pl.BlockSpec((1, t, KV_LORA), lambda bi, i: (bi, 0, 0)),
            pl.BlockSpec((1, DSA_HEADS, QBLOCK, KV_LORA), lambda bi, i: (bi * nblk + i, 0, 0, 0)),
        ],
        out_specs=pl.BlockSpec((1, DSA_HEADS, QBLOCK, KV_LORA), lambda bi, i: (bi * nblk + i, 0, 0, 0)),
        out_shape=jax.ShapeDtypeStruct((b * nblk, DSA_HEADS, QBLOCK, KV_LORA), BF16),
        scratch_shapes=[pltpu.VMEM((t // kt, kt, QBLOCK), jnp.int32), pltpu.VMEM((8, QBLOCK), jnp.int32),
                        pltpu.VMEM((8, QBLOCK), F32),
                        pltpu.VMEM((QBLOCK, kt), F32),
                        pltpu.VMEM((rows, kt), F32), pltpu.VMEM((rows, kt), BF16),
                        pltpu.VMEM((rows, LANES), F32), pltpu.VMEM((rows, LANES), F32),
                        pltpu.VMEM((rows, KV_LORA), F32)],
        compiler_params=_params("parallel", "arbitrary"),
        name="dsa_core",
    )(q_all.reshape(b, t, -1), widx.reshape(b, t, LANES), kidx.reshape(b, t, IDX_DIM),
      ckv.reshape(b, t, KV_LORA), qabs)


def dsa_layer(h2, u, b, t, next_gain, w_in, q_norm, kv_norm, kidx_norm, w_uq, w_uk, w_uv, w_out):
    width = Q_LORA + KV_LORA + IDX_DIM + LANES
    w_in_p = jnp.pad(w_in, ((0, 0), (0, width - w_in.shape[1]))).astype(BF16)
    qlat, ckv, kidx, widx = dsa_in_proj(u, w_in_p, q_norm, kv_norm, kidx_norm)
    q_all = matmul(qlat, w_uq.astype(BF16), BF16)
    qabs = q_absorb(q_all, w_uk.astype(BF16))
    o_lat = dsa_core(q_all, widx, kidx, ckv, qabs, b, t)
    o = v_up(o_lat, w_uv.astype(BF16))
    return matmul_res_norm(o, w_out.astype(BF16), h2, next_gain, BF16)


def mlp(h2, u, next_gain, u_dtype, w_up, w_down):
    a = matmul(u, w_up.astype(BF16), BF16, relu2=True, tn=2048)
    if jnp.dtype(u_dtype).itemsize == 2:
        return matmul_res_norm(a, w_down.astype(BF16), h2, next_gain, u_dtype, tm=1024, tk=1024, acc_in_h=True)
    return matmul_res_norm(a, w_down.astype(BF16), h2, next_gain, u_dtype)


def kernel(x, norm_mix, norm_mlp, norm_final, dn_w_in, dn_conv_w, dn_a_log, dn_dt_bias, dn_out_norm, dn_w_out, dsa_w_in, dsa_q_norm, dsa_kv_norm, dsa_kidx_norm, dsa_w_uq, dsa_w_uk, dsa_w_uv, dsa_w_out, mlp_w_up, mlp_w_down):
    b, t, d = x.shape
    depth = norm_mix.shape[0]
    h2 = x.reshape(b * t, d)
    u = rms_norm(h2, norm_mix[0], BF16)
    for i in range(depth):
        j = i // 2
        if i % 2 == 0:
            h2, u = gated_deltanet_layer(h2, u, b, t, norm_mlp[i], dn_w_in[j], dn_conv_w[j], dn_a_log[j],
                                         dn_dt_bias[j], dn_out_norm[j], dn_w_out[j])
        else:
            h2, u = dsa_layer(h2, u, b, t, norm_mlp[i], dsa_w_in[j], dsa_q_norm[j], dsa_kv_norm[j],
                              dsa_kidx_norm[j], dsa_w_uq[j], dsa_w_uk[j], dsa_w_uv[j], dsa_w_out[j])
        last = i == depth - 1
        h2, u = mlp(h2, u, norm_final if last else norm_mix[i + 1], x.dtype if last else BF16,
                    mlp_w_up[i], mlp_w_down[i])
    return u.reshape(b, t, d)
```
